```python
import jax, jax.numpy as jnp
from jax import lax
import numpy as np

D_MODEL = 2048
BATCH = 8
SEQ = 8192
DEPTH = 2

ATT_HEAD_DIM = 64
ATT_Q_HEADS = 16
ATT_KV_HEADS = 4
ATT_GROUP = ATT_Q_HEADS // ATT_KV_HEADS
WINDOW = 128
ATT_BLOCK = WINDOW
ATT_WIDTH = ATT_Q_HEADS * ATT_HEAD_DIM
KV_WIDTH = ATT_KV_HEADS * ATT_HEAD_DIM

POOL_WINDOWS = (2, 4, 8, 16)
POOL_GROUPS = len(POOL_WINDOWS)
POOL_WIDTH = D_MODEL // 2
POOL_GROUP_DIM = POOL_WIDTH // POOL_GROUPS

D_INNER = D_MODEL
SSM_HEAD_DIM = 64
SSM_HEADS = D_INNER // SSM_HEAD_DIM
SSM_GROUPS = 4
HEADS_PER_GROUP = SSM_HEADS // SSM_GROUPS
D_STATE = 128
CONV_K = 4
CHUNK = 128
CONV_CH = D_INNER + 2 * SSM_GROUPS * D_STATE

N_BRANCH = 3
D_FF = -(-8 * D_MODEL // (3 * 256)) * 256
EPS = 1e-6

IN_WIDTHS = (ATT_WIDTH, KV_WIDTH, KV_WIDTH, POOL_WIDTH, D_INNER, CONV_CH, SSM_HEADS, N_BRANCH * D_MODEL)
IN_COLS = sum(IN_WIDTHS)

kernel_name = 'hybrid_gated_swa_pool_ssd_block'


def split_points(widths):
    pts, acc = [], 0
    for w in widths[:-1]:
        acc += w
        pts.append(acc)
    return pts


def rmsnorm(x, w):
    xf = x.astype(jnp.float32)
    y = xf * lax.rsqrt(jnp.mean(xf * xf, axis=-1, keepdims=True) + EPS)
    return (y * w.astype(jnp.float32)).astype(x.dtype)


def sink_window_attention(q, k, v, sink):
    b, l = q.shape[0], q.shape[1]
    nb = l // ATT_BLOCK
    qb = q.reshape(b, nb, ATT_BLOCK, ATT_KV_HEADS, ATT_GROUP, ATT_HEAD_DIM)
    kb = k.reshape(b, nb, ATT_BLOCK, ATT_KV_HEADS, ATT_HEAD_DIM)
    vb = v.reshape(b, nb, ATT_BLOCK, ATT_KV_HEADS, ATT_HEAD_DIM)

    def with_prev(t):
        prev = jnp.pad(t[:, :-1], ((0, 0), (1, 0), (0, 0), (0, 0), (0, 0)))
        return jnp.concatenate([prev, t], axis=2)

    kk, vv = with_prev(kb), with_prev(vb)
    scores = jnp.einsum('bnqkgd,bnskd->bnkgqs', qb, kk).astype(jnp.float32) * (ATT_HEAD_DIM ** -0.5)
    blk = jnp.arange(nb)[:, None, None]
    qpos = blk * ATT_BLOCK + jnp.arange(ATT_BLOCK)[None, :, None]
    kpos = (blk - 1) * ATT_BLOCK + jnp.arange(2 * ATT_BLOCK)[None, None, :]
    diff = qpos - kpos
    mask = (diff >= 0) & (diff < WINDOW) & (kpos >= 0)
    scores = jnp.where(mask[None, :, None, None], scores, -jnp.inf)
    sink_l = sink.astype(jnp.float32).reshape(ATT_KV_HEADS, ATT_GROUP)[None, None, :, :, None, None]
    sink_l = jnp.broadcast_to(sink_l, scores.shape[:-1] + (1,))
    probs = jax.nn.softmax(jnp.concatenate([scores, sink_l], axis=-1), axis=-1)[..., :-1]
    out = jnp.einsum('bnkgqs,bnskd->bnqkgd', probs.astype(v.dtype), vv)
    return out.reshape(b, l, ATT_WIDTH)


def multiscale_pool(u, pool_w, pool_scale):
    b, l, _ = u.shape
    ug = u.reshape(b, l, POOL_GROUPS, POOL_GROUP_DIM).astype(jnp.float32)
    cs = jnp.cumsum(ug, axis=1)
    t = jnp.arange(l)
    means = []
    for gi, w in enumerate(POOL_WINDOWS):
        c = cs[:, :, gi]
        shifted = jnp.pad(c, ((0, 0), (w, 0), (0, 0)))[:, :l]
        cnt = jnp.minimum(t + 1, w).astype(jnp.float32)[None, :, None]
        means.append((c - shifted) / cnt)
    mixed = (jnp.stack(means, axis=2) - ug).astype(u.dtype)
    y = jnp.einsum('blgc,gcd->blgd', mixed, pool_w).reshape(b, l, POOL_WIDTH)
    return y * pool_scale


def ssd_mixer(z, xbc, dt_raw, conv_w, conv_b, dt_bias, a_log, d_skip, norm_w):
    b, l, _ = xbc.shape
    xbc = lax.conv_general_dilated(xbc, conv_w[:, None, :].astype(xbc.dtype), window_strides=(1,),
                                   padding=[(CONV_K - 1, 0)], dimension_numbers=('NWC', 'WIO', 'NWC'),
                                   feature_group_count=CONV_CH) + conv_b
    xbc = jax.nn.silu(xbc)
    nc = l // CHUNK
    xs = xbc[..., :D_INNER].astype(jnp.float32).reshape(b, nc, CHUNK, SSM_GROUPS, HEADS_PER_GROUP, SSM_HEAD_DIM)
    bm = xbc[..., D_INNER:D_INNER + SSM_GROUPS * D_STATE].astype(jnp.float32).reshape(b, nc, CHUNK, SSM_GROUPS, D_STATE)
    cm = xbc[..., D_INNER + SSM_GROUPS * D_STATE:].astype(jnp.float32).reshape(b, nc, CHUNK, SSM_GROUPS, D_STATE)
    dt = jax.nn.softplus(dt_raw.astype(jnp.float32) + dt_bias.astype(jnp.float32))
    dt = dt.reshape(b, nc, CHUNK, SSM_GROUPS, HEADS_PER_GROUP)
    a = -jnp.exp(a_log.astype(jnp.float32)).reshape(SSM_GROUPS, HEADS_PER_GROUP)
    a_cs = jnp.cumsum(dt * a, axis=2)
    xdt = xs * dt[..., None]
    seg = a_cs[:, :, :, None] - a_cs[:, :, None, :]
    causal = jnp.tril(jnp.ones((CHUNK, CHUNK), dtype=bool))
    decay = jnp.exp(jnp.where(causal[:, :, None, None], seg, -jnp.inf))
    cb = jnp.einsum('bclgn,bcsgn->bclsg', cm, bm)
    y_diag = jnp.einsum('bclsg,bclsge,bcsgep->bclgep', cb, decay, xdt)
    decay_to_end = jnp.exp(a_cs[:, :, -1:] - a_cs)
    states = jnp.einsum('bcsgn,bcsge,bcsgep->bcgepn', bm, decay_to_end, xdt)
    chunk_decay = jnp.exp(a_cs[:, :, -1])

    def step(h, inp):
        st, dec = inp
        return h * dec[..., None, None] + st, h

    h0 = jnp.zeros((b, SSM_GROUPS, HEADS_PER_GROUP, SSM_HEAD_DIM, D_STATE), jnp.float32)
    _, prev = lax.scan(step, h0, (jnp.moveaxis(states, 1, 0), jnp.moveaxis(chunk_decay, 1, 0)))
    prev = jnp.moveaxis(prev, 0, 1)
    y_off = jnp.einsum('bclgn,bcgepn,bclge->bclgep', cm, prev, jnp.exp(a_cs))
    y = y_diag + y_off + d_skip.astype(jnp.float32).reshape(SSM_GROUPS, HEADS_PER_GROUP)[:, :, None] * xs
    y = y.reshape(b, l, D_INNER)
    g = (y * jax.nn.silu(z.astype(jnp.float32))).reshape(b, l, SSM_GROUPS, D_INNER // SSM_GROUPS)
    g = g * lax.rsqrt(jnp.mean(g * g, axis=-1, keepdims=True) + EPS)
    return (g.reshape(b, l, D_INNER) * norm_w.astype(jnp.float32)).astype(z.dtype)


def _fwd_setup_inputs(seed: int = 0) -> dict:
    key = jax.random.key(seed)
    ks = jax.random.split(key, 24)
    f32 = jnp.float32

    def nrm(k, shape, scale):
        return jax.random.normal(k, shape, f32) * scale

    dt0 = jnp.exp(jax.random.uniform(ks[6], (DEPTH, SSM_HEADS), f32, np.log(1e-3), np.log(1e-1)))
    dt_bias = dt0 + jnp.log(-jnp.expm1(-dt0))
    a_log = jnp.log(jax.random.uniform(ks[7], (DEPTH, SSM_HEADS), f32, 1.0, 16.0))
    return {
        'x': nrm(ks[0], (BATCH, SEQ, D_MODEL), 1.0),
        'ln1_w': 1.0 + nrm(ks[1], (DEPTH, D_MODEL), 0.05),
        'w_in': nrm(ks[2], (DEPTH, D_MODEL, IN_COLS), D_MODEL ** -0.5),
        'attn_sink': nrm(ks[3], (DEPTH, ATT_Q_HEADS), 0.5),
        'conv_w': nrm(ks[4], (DEPTH, CONV_K, CONV_CH), CONV_K ** -0.5),
        'conv_b': nrm(ks[5], (DEPTH, CONV_CH), 0.02),
        'dt_bias': dt_bias,
        'a_log': a_log,
        'd_skip': 1.0 + nrm(ks[8], (DEPTH, SSM_HEADS), 0.1),
        'ssm_norm_w': 1.0 + nrm(ks[9], (DEPTH, D_INNER), 0.05),
        'pool_w': nrm(ks[10], (DEPTH, POOL_GROUPS, POOL_GROUP_DIM, POOL_GROUP_DIM), POOL_GROUP_DIM ** -0.5),
        'pool_scale': 1.0 + nrm(ks[11], (DEPTH, POOL_WIDTH), 0.1),
        'w_attn_br': nrm(ks[12], (DEPTH, ATT_WIDTH, D_MODEL), ATT_WIDTH ** -0.5),
        'w_pool_br': nrm(ks[13], (DEPTH, POOL_WIDTH, D_MODEL), POOL_WIDTH ** -0.5),
        'w_ssm_br': nrm(ks[14], (DEPTH, D_INNER, D_MODEL), D_INNER ** -0.5),
        'w_out': nrm(ks[15], (DEPTH, D_MODEL, D_MODEL), D_MODEL ** -0.5),
        'ln2_w': 1.0 + nrm(ks[16], (DEPTH, D_MODEL), 0.05),
        'w_gate_up': nrm(ks[17], (DEPTH, D_MODEL, 2 * D_FF), D_MODEL ** -0.5),
        'w_down': nrm(ks[18], (DEPTH, D_FF, D_MODEL), D_FF ** -0.5),
        'final_w': 1.0 + nrm(ks[19], (D_MODEL,), 0.05),
    }


def _fwd_reference(x, ln1_w, w_in, attn_sink, conv_w, conv_b, dt_bias, a_log, d_skip, ssm_norm_w,
              pool_w, pool_scale, w_attn_br, w_pool_br, w_ssm_br, w_out, ln2_w, w_gate_up,
              w_down, final_w):
    b, l, _ = x.shape
    pts = split_points(IN_WIDTHS)
    for i in range(DEPTH):
        h = rmsnorm(x, ln1_w[i])
        proj = h @ w_in[i]
        q, k, v, u, z, xbc, dt_raw, gate_logits = jnp.split(proj, pts, axis=-1)
        att = sink_window_attention(q.reshape(b, l, ATT_Q_HEADS, ATT_HEAD_DIM),
                                    k.reshape(b, l, ATT_KV_HEADS, ATT_HEAD_DIM),
                                    v.reshape(b, l, ATT_KV_HEADS, ATT_HEAD_DIM), attn_sink[i])
        pool = multiscale_pool(u, pool_w[i], pool_scale[i])
        ssm = ssd_mixer(z, xbc, dt_raw, conv_w[i], conv_b[i], dt_bias[i], a_log[i], d_skip[i], ssm_norm_w[i])
        gates = jax.nn.sigmoid(gate_logits.astype(jnp.float32)).astype(x.dtype).reshape(b, l, N_BRANCH, D_MODEL)
        merged = (gates[:, :, 0] * (att @ w_attn_br[i])
                  + gates[:, :, 1] * (pool @ w_pool_br[i])
                  + gates[:, :, 2] * (ssm @ w_ssm_br[i]))
        x = x + merged @ w_out[i]
        h2 = rmsnorm(x, ln2_w[i])
        gu = h2 @ w_gate_up[i]
        x = x + (jax.nn.silu(gu[..., :D_FF]) * gu[..., D_FF:]) @ w_down[i]
    return rmsnorm(x, final_w)


import jax as _jax
import jax.numpy as _jnp

TWIN_FORMAT = 'train_step'
FWD_PARAMS = ['x', 'ln1_w', 'w_in', 'attn_sink', 'conv_w', 'conv_b', 'dt_bias', 'a_log', 'd_skip', 'ssm_norm_w', 'pool_w', 'pool_scale', 'w_attn_br', 'w_pool_br', 'w_ssm_br', 'w_out', 'ln2_w', 'w_gate_up', 'w_down', 'final_w']
TWIN_WEIGHTS = ['ln1_w', 'w_in', 'attn_sink', 'conv_w', 'conv_b', 'dt_bias', 'a_log', 'd_skip', 'ssm_norm_w', 'pool_w', 'pool_scale', 'w_attn_br', 'w_pool_br', 'w_ssm_br', 'w_out', 'ln2_w', 'w_gate_up', 'w_down', 'final_w']
TWIN_DIFF_INPUT = 'x'
TWIN_INPUTS = ['x', 'ln1_w', 'w_in', 'attn_sink', 'conv_w', 'conv_b', 'dt_bias', 'a_log', 'd_skip', 'ssm_norm_w', 'pool_w', 'pool_scale', 'w_attn_br', 'w_pool_br', 'w_ssm_br', 'w_out', 'ln2_w', 'w_gate_up', 'w_down', 'final_w', 'loss_target', 'm_ln1_w', 'm_w_in', 'm_attn_sink', 'm_conv_w', 'm_conv_b', 'm_dt_bias', 'm_a_log', 'm_d_skip', 'm_ssm_norm_w', 'm_pool_w', 'm_pool_scale', 'm_w_attn_br', 'm_w_pool_br', 'm_w_ssm_br', 'm_w_out', 'm_ln2_w', 'm_w_gate_up', 'm_w_down', 'm_final_w', 'v_ln1_w', 'v_w_in', 'v_attn_sink', 'v_conv_w', 'v_conv_b', 'v_dt_bias', 'v_a_log', 'v_d_skip', 'v_ssm_norm_w', 'v_pool_w', 'v_pool_scale', 'v_w_attn_br', 'v_w_pool_br', 'v_w_ssm_br', 'v_w_out', 'v_ln2_w', 'v_w_gate_up', 'v_w_down', 'v_final_w']
TWIN_OUTPUTS = ['loss', 'grad_x', 'grad_ln1_w', 'grad_w_in', 'grad_attn_sink', 'grad_conv_w', 'grad_conv_b', 'grad_dt_bias', 'grad_a_log', 'grad_d_skip', 'grad_ssm_norm_w', 'grad_pool_w', 'grad_pool_scale', 'grad_w_attn_br', 'grad_w_pool_br', 'grad_w_ssm_br', 'grad_w_out', 'grad_ln2_w', 'grad_w_gate_up', 'grad_w_down', 'grad_final_w', 'delta_ln1_w', 'delta_w_in', 'delta_attn_sink', 'delta_conv_w', 'delta_conv_b', 'delta_dt_bias', 'delta_a_log', 'delta_d_skip', 'delta_ssm_norm_w', 'delta_pool_w', 'delta_pool_scale', 'delta_w_attn_br', 'delta_w_pool_br', 'delta_w_ssm_br', 'delta_w_out', 'delta_ln2_w', 'delta_w_gate_up', 'delta_w_down', 'delta_final_w', 'new_m_ln1_w', 'new_m_w_in', 'new_m_attn_sink', 'new_m_conv_w', 'new_m_conv_b', 'new_m_dt_bias', 'new_m_a_log', 'new_m_d_skip', 'new_m_ssm_norm_w', 'new_m_pool_w', 'new_m_pool_scale', 'new_m_w_attn_br', 'new_m_w_pool_br', 'new_m_w_ssm_br', 'new_m_w_out', 'new_m_ln2_w', 'new_m_w_gate_up', 'new_m_w_down', 'new_m_final_w', 'new_v_ln1_w', 'new_v_w_in', 'new_v_attn_sink', 'new_v_conv_w', 'new_v_conv_b', 'new_v_dt_bias', 'new_v_a_log', 'new_v_d_skip', 'new_v_ssm_norm_w', 'new_v_pool_w', 'new_v_pool_scale', 'new_v_w_attn_br', 'new_v_w_pool_br', 'new_v_w_ssm_br', 'new_v_w_out', 'new_v_ln2_w', 'new_v_w_gate_up', 'new_v_w_down', 'new_v_final_w']
TWIN_LEAF_KINDS = {'loss': 'loss', 'grad_x': 'grad_x', 'grad_ln1_w': 'grad_w', 'grad_w_in': 'grad_w', 'grad_attn_sink': 'grad_w', 'grad_conv_w': 'grad_w', 'grad_conv_b': 'grad_w', 'grad_dt_bias': 'grad_w', 'grad_a_log': 'grad_w', 'grad_d_skip': 'grad_w', 'grad_ssm_norm_w': 'grad_w', 'grad_pool_w': 'grad_w', 'grad_pool_scale': 'grad_w', 'grad_w_attn_br': 'grad_w', 'grad_w_pool_br': 'grad_w', 'grad_w_ssm_br': 'grad_w', 'grad_w_out': 'grad_w', 'grad_ln2_w': 'grad_w', 'grad_w_gate_up': 'grad_w', 'grad_w_down': 'grad_w', 'grad_final_w': 'grad_w', 'delta_ln1_w': 'delta_w', 'delta_w_in': 'delta_w', 'delta_attn_sink': 'delta_w', 'delta_conv_w': 'delta_w', 'delta_conv_b': 'delta_w', 'delta_dt_bias': 'delta_w', 'delta_a_log': 'delta_w', 'delta_d_skip': 'delta_w', 'delta_ssm_norm_w': 'delta_w', 'delta_pool_w': 'delta_w', 'delta_pool_scale': 'delta_w', 'delta_w_attn_br': 'delta_w', 'delta_w_pool_br': 'delta_w', 'delta_w_ssm_br': 'delta_w', 'delta_w_out': 'delta_w', 'delta_ln2_w': 'delta_w', 'delta_w_gate_up': 'delta_w', 'delta_w_down': 'delta_w', 'delta_final_w': 'delta_w', 'new_m_ln1_w': 'new_m', 'new_m_w_in': 'new_m', 'new_m_attn_sink': 'new_m', 'new_m_conv_w': 'new_m', 'new_m_conv_b': 'new_m', 'new_m_dt_bias': 'new_m', 'new_m_a_log': 'new_m', 'new_m_d_skip': 'new_m', 'new_m_ssm_norm_w': 'new_m', 'new_m_pool_w': 'new_m', 'new_m_pool_scale': 'new_m', 'new_m_w_attn_br': 'new_m', 'new_m_w_pool_br': 'new_m', 'new_m_w_ssm_br': 'new_m', 'new_m_w_out': 'new_m', 'new_m_ln2_w': 'new_m', 'new_m_w_gate_up': 'new_m', 'new_m_w_down': 'new_m', 'new_m_final_w': 'new_m', 'new_v_ln1_w': 'new_v', 'new_v_w_in': 'new_v', 'new_v_attn_sink': 'new_v', 'new_v_conv_w': 'new_v', 'new_v_conv_b': 'new_v', 'new_v_dt_bias': 'new_v', 'new_v_a_log': 'new_v', 'new_v_d_skip': 'new_v', 'new_v_ssm_norm_w': 'new_v', 'new_v_pool_w': 'new_v', 'new_v_pool_scale': 'new_v', 'new_v_w_attn_br': 'new_v', 'new_v_w_pool_br': 'new_v', 'new_v_w_ssm_br': 'new_v', 'new_v_w_out': 'new_v', 'new_v_ln2_w': 'new_v', 'new_v_w_gate_up': 'new_v', 'new_v_w_down': 'new_v', 'new_v_final_w': 'new_v'}


def _forward(args):
    return _fwd_reference(*[args[k] for k in FWD_PARAMS])


def _output_shape():
    def fwd():
        inp = _fwd_setup_inputs(0)
        return _fwd_reference(*[inp[k] for k in FWD_PARAMS])
    out = _jax.eval_shape(fwd)
    return out.shape, out.dtype

N_MICROBATCH = 1
ADAM_LR = 0.001
ADAM_B1 = 0.9
ADAM_B2 = 0.999
ADAM_EPS = 1e-08
ADAM_WD = 0.01
ADAM_STEP = 10
PER_EXAMPLE_BATCH_AXIS = {'x': 0, 'loss_target': 0}
SHARED_INPUTS = []
_WEIGHT_DTYPES = {'ln1_w': _jnp.float32, 'w_in': _jnp.float32, 'attn_sink': _jnp.float32, 'conv_w': _jnp.float32, 'conv_b': _jnp.float32, 'dt_bias': _jnp.float32, 'a_log': _jnp.float32, 'd_skip': _jnp.float32, 'ssm_norm_w': _jnp.float32, 'pool_w': _jnp.float32, 'pool_scale': _jnp.float32, 'w_attn_br': _jnp.float32, 'w_pool_br': _jnp.float32, 'w_ssm_br': _jnp.float32, 'w_out': _jnp.float32, 'ln2_w': _jnp.float32, 'w_gate_up': _jnp.float32, 'w_down': _jnp.float32, 'final_w': _jnp.float32}
MOMENT_SCALE = {'ln1_w': 1.115289e-01, 'w_in': 4.276691e-02, 'attn_sink': 1.547721e-02, 'conv_w': 5.188006e-02, 'conv_b': 8.155737e-02, 'dt_bias': 1.365940e-01, 'a_log': 4.790965e-01, 'd_skip': 3.260099e-01, 'ssm_norm_w': 6.027357e-02, 'pool_w': 7.601787e-02, 'pool_scale': 7.631869e-02, 'w_attn_br': 1.274120e-02, 'w_pool_br': 5.386231e-02, 'w_ssm_br': 6.094843e-02, 'w_out': 8.160791e-02, 'ln2_w': 8.014433e-02, 'w_gate_up': 3.398211e-02, 'w_down': 5.571806e-02, 'final_w': 3.200821e+01}


def _to_microbatches(a, axis):
    t = _jnp.moveaxis(a, axis, 0)
    t = t.reshape((N_MICROBATCH, t.shape[0] // N_MICROBATCH) + t.shape[1:])
    return _jnp.moveaxis(t, 1, axis + 1)


def setup_inputs(seed: int = 0) -> dict:
    inp = _fwd_setup_inputs(seed)
    key = _jax.random.fold_in(_jax.random.key(seed), 7919)
    shape, _ = _output_shape()
    out = dict(inp)
    out["loss_target"] = _jax.random.normal(_jax.random.fold_in(key, 0), shape, _jnp.float32)
    for i, name in enumerate(TWIN_WEIGHTS):
        w = inp[name].astype(_jnp.float32)
        if MOMENT_SCALE is None:
            s = _jnp.sqrt(_jnp.mean(_jnp.square(w)) + 1e-30)
        else:
            s = MOMENT_SCALE[name]
        km, kv = _jax.random.split(_jax.random.fold_in(key, i + 1))
        out[name] = w
        out["m_" + name] = s * _jax.random.normal(km, w.shape, _jnp.float32)
        out["v_" + name] = (s * s) * _jax.random.uniform(kv, w.shape, _jnp.float32, 0.5, 1.5)
    if N_MICROBATCH > 1:
        for name, axis in PER_EXAMPLE_BATCH_AXIS.items():
            out[name] = _to_microbatches(out[name], axis)
    return {'x': out['x'], 'ln1_w': out['ln1_w'], 'w_in': out['w_in'], 'attn_sink': out['attn_sink'], 'conv_w': out['conv_w'], 'conv_b': out['conv_b'], 'dt_bias': out['dt_bias'], 'a_log': out['a_log'], 'd_skip': out['d_skip'], 'ssm_norm_w': out['ssm_norm_w'], 'pool_w': out['pool_w'], 'pool_scale': out['pool_scale'], 'w_attn_br': out['w_attn_br'], 'w_pool_br': out['w_pool_br'], 'w_ssm_br': out['w_ssm_br'], 'w_out': out['w_out'], 'ln2_w': out['ln2_w'], 'w_gate_up': out['w_gate_up'], 'w_down': out['w_down'], 'final_w': out['final_w'], 'loss_target': out['loss_target'], 'm_ln1_w': out['m_ln1_w'], 'm_w_in': out['m_w_in'], 'm_attn_sink': out['m_attn_sink'], 'm_conv_w': out['m_conv_w'], 'm_conv_b': out['m_conv_b'], 'm_dt_bias': out['m_dt_bias'], 'm_a_log': out['m_a_log'], 'm_d_skip': out['m_d_skip'], 'm_ssm_norm_w': out['m_ssm_norm_w'], 'm_pool_w': out['m_pool_w'], 'm_pool_scale': out['m_pool_scale'], 'm_w_attn_br': out['m_w_attn_br'], 'm_w_pool_br': out['m_w_pool_br'], 'm_w_ssm_br': out['m_w_ssm_br'], 'm_w_out': out['m_w_out'], 'm_ln2_w': out['m_ln2_w'], 'm_w_gate_up': out['m_w_gate_up'], 'm_w_down': out['m_w_down'], 'm_final_w': out['m_final_w'], 'v_ln1_w': out['v_ln1_w'], 'v_w_in': out['v_w_in'], 'v_attn_sink': out['v_attn_sink'], 'v_conv_w': out['v_conv_w'], 'v_conv_b': out['v_conv_b'], 'v_dt_bias': out['v_dt_bias'], 'v_a_log': out['v_a_log'], 'v_d_skip': out['v_d_skip'], 'v_ssm_norm_w': out['v_ssm_norm_w'], 'v_pool_w': out['v_pool_w'], 'v_pool_scale': out['v_pool_scale'], 'v_w_attn_br': out['v_w_attn_br'], 'v_w_pool_br': out['v_w_pool_br'], 'v_w_ssm_br': out['v_w_ssm_br'], 'v_w_out': out['v_w_out'], 'v_ln2_w': out['v_ln2_w'], 'v_w_gate_up': out['v_w_gate_up'], 'v_w_down': out['v_w_down'], 'v_final_w': out['v_final_w']}


def _loss(weights, diff, rest, loss_target):
    with _jax.named_scope("forward"):
        args = {**rest, TWIN_DIFF_INPUT: diff, **{k: w.astype(_WEIGHT_DTYPES[k]) for k, w in weights.items()}}
        y = _forward(args)
    with _jax.named_scope("loss_head"):
        err = _jnp.square(y.astype(_jnp.float32) - loss_target)
        return 0.5 * _jnp.sum(_jnp.mean(err, axis=-1)) if err.ndim else 0.5 * err


def _adamw(w, g, m, v):
    m = ADAM_B1 * m + (1.0 - ADAM_B1) * g
    v = ADAM_B2 * v + (1.0 - ADAM_B2) * _jnp.square(g)
    m_hat = m / (1.0 - ADAM_B1 ** ADAM_STEP)
    v_hat = v / (1.0 - ADAM_B2 ** ADAM_STEP)
    delta = -ADAM_LR * (m_hat / (_jnp.sqrt(v_hat) + ADAM_EPS) + ADAM_WD * w)
    return delta, m, v


def reference(x, ln1_w, w_in, attn_sink, conv_w, conv_b, dt_bias, a_log, d_skip, ssm_norm_w, pool_w, pool_scale, w_attn_br, w_pool_br, w_ssm_br, w_out, ln2_w, w_gate_up, w_down, final_w, loss_target, m_ln1_w, m_w_in, m_attn_sink, m_conv_w, m_conv_b, m_dt_bias, m_a_log, m_d_skip, m_ssm_norm_w, m_pool_w, m_pool_scale, m_w_attn_br, m_w_pool_br, m_w_ssm_br, m_w_out, m_ln2_w, m_w_gate_up, m_w_down, m_final_w, v_ln1_w, v_w_in, v_attn_sink, v_conv_w, v_conv_b, v_dt_bias, v_a_log, v_d_skip, v_ssm_norm_w, v_pool_w, v_pool_scale, v_w_attn_br, v_w_pool_br, v_w_ssm_br, v_w_out, v_ln2_w, v_w_gate_up, v_w_down, v_final_w):
    given = dict(x=x, ln1_w=ln1_w, w_in=w_in, attn_sink=attn_sink, conv_w=conv_w, conv_b=conv_b, dt_bias=dt_bias, a_log=a_log, d_skip=d_skip, ssm_norm_w=ssm_norm_w, pool_w=pool_w, pool_scale=pool_scale, w_attn_br=w_attn_br, w_pool_br=w_pool_br, w_ssm_br=w_ssm_br, w_out=w_out, ln2_w=ln2_w, w_gate_up=w_gate_up, w_down=w_down, final_w=final_w, loss_target=loss_target, m_ln1_w=m_ln1_w, m_w_in=m_w_in, m_attn_sink=m_attn_sink, m_conv_w=m_conv_w, m_conv_b=m_conv_b, m_dt_bias=m_dt_bias, m_a_log=m_a_log, m_d_skip=m_d_skip, m_ssm_norm_w=m_ssm_norm_w, m_pool_w=m_pool_w, m_pool_scale=m_pool_scale, m_w_attn_br=m_w_attn_br, m_w_pool_br=m_w_pool_br, m_w_ssm_br=m_w_ssm_br, m_w_out=m_w_out, m_ln2_w=m_ln2_w, m_w_gate_up=m_w_gate_up, m_w_down=m_w_down, m_final_w=m_final_w, v_ln1_w=v_ln1_w, v_w_in=v_w_in, v_attn_sink=v_attn_sink, v_conv_w=v_conv_w, v_conv_b=v_conv_b, v_dt_bias=v_dt_bias, v_a_log=v_a_log, v_d_skip=v_d_skip, v_ssm_norm_w=v_ssm_norm_w, v_pool_w=v_pool_w, v_pool_scale=v_pool_scale, v_w_attn_br=v_w_attn_br, v_w_pool_br=v_w_pool_br, v_w_ssm_br=v_w_ssm_br, v_w_out=v_w_out, v_ln2_w=v_ln2_w, v_w_gate_up=v_w_gate_up, v_w_down=v_w_down, v_final_w=v_final_w)
    weights = {n: given[n] for n in TWIN_WEIGHTS}
    shared = {n: given[n] for n in SHARED_INPUTS}
    per_example = {n: given[n] for n in ['x']}
    grad_fn = _jax.value_and_grad(_loss, argnums=(0, 1))

    def one_microbatch(ex, loss_target):
        ex = dict(ex)
        diff = ex.pop(TWIN_DIFF_INPUT)
        return grad_fn(weights, diff, {**shared, **ex}, loss_target)

    if N_MICROBATCH == 1:
        loss, (grad_w, grad_x) = one_microbatch(per_example, given["loss_target"])
    else:
        def body(carry, xs):
            loss_sum, grad_sum = carry
            l_k, (gw_k, gx_k) = one_microbatch(xs[0], xs[1])
            with _jax.named_scope("update"):
                return (loss_sum + l_k, _jax.tree.map(_jnp.add, grad_sum, gw_k)), gx_k

        init = (_jnp.zeros((), _jnp.float32), _jax.tree.map(_jnp.zeros_like, weights))
        (loss, grad_w), grad_x = _jax.lax.scan(body, init, (per_example, given["loss_target"]))
    with _jax.named_scope("update"):
        delta_w, new_m, new_v = {}, {}, {}
        for n in TWIN_WEIGHTS:
            delta_w[n], new_m[n], new_v[n] = _adamw(weights[n], grad_w[n], given["m_" + n], given["v_" + n])
    return (loss, grad_x, *[grad_w[n] for n in TWIN_WEIGHTS], *[delta_w[n] for n in TWIN_WEIGHTS],
            *[new_m[n] for n in TWIN_WEIGHTS], *[new_v[n] for n in TWIN_WEIGHTS])
```

```python
import functools

import jax
import jax.numpy as jnp
from jax import lax
from jax.experimental import pallas as pl
from jax.experimental.pallas import tpu as pltpu

D_MODEL = 2048
DEPTH = 2
ATT_HEAD_DIM = 64
ATT_Q_HEADS = 16
ATT_KV_HEADS = 4
WINDOW = 128
POOL_WINDOWS = (2, 4, 8, 16)
POOL_WIDTH = D_MODEL // 2
D_INNER = D_MODEL
SSM_HEAD_DIM = 64
SSM_GROUPS = 4
D_STATE = 128
CONV_K = 4
CHUNK = 128
N_BRANCH = 3
D_FF = 5632
EPS = 1e-6

ADAM_LR = 0.001
ADAM_B1 = 0.9
ADAM_B2 = 0.999
ADAM_EPS = 1e-08
ADAM_WD = 0.01
ADAM_STEP = 10

N_DEV = 8
F32 = jnp.float32
BF16 = jnp.bfloat16
MXU_DTYPE = jnp.bfloat16
NEG = -1e30
VMEM_CAP = 60 * 2**20
PACK_W = 1024
COMM_CHUNKS = 5
CONV_HALO = 8
POOL_HALO = 16
MESH_T = pl.DeviceIdType.MESH


class Dims:
    def __init__(self):
        self.D = D_MODEL
        self.AW = ATT_Q_HEADS * ATT_HEAD_DIM
        self.KVW = ATT_KV_HEADS * ATT_HEAD_DIM
        self.GQ = ATT_Q_HEADS // ATT_KV_HEADS
        self.PW = POOL_WIDTH
        self.PG = len(POOL_WINDOWS)
        self.PC = POOL_WIDTH // len(POOL_WINDOWS)
        self.DI = D_INNER
        self.H = D_INNER // SSM_HEAD_DIM
        self.P = SSM_HEAD_DIM
        self.G = SSM_GROUPS
        self.HPG = self.H // SSM_GROUPS
        self.GW = D_INNER // SSM_GROUPS
        self.N = D_STATE
        self.GN = SSM_GROUPS * D_STATE
        self.CC = D_INNER + 2 * SSM_GROUPS * D_STATE
        self.DTP = -(-self.H // 256) * 256
        self.DFF = D_FF
        self.in_widths = (self.AW, self.KVW, self.KVW, self.PW, self.DI, self.CC, self.H, N_BRANCH * self.D)
        self.IN_COLS = sum(self.in_widths)
        self.QKV = self.AW + 2 * self.KVW
        self.seg = (self.QKV, self.PW, self.DI, self.CC, self.DTP, N_BRANCH * self.D)
        self.IN_INT = sum(self.seg)


def _tile(n, cap, mult=128):
    if n <= cap:
        return n
    best = 0
    for t in range(mult, cap + 1, mult):
        if n % t == 0:
            best = t
    assert best, (n, cap, mult)
    return best


def _nbytes(shape, dtype):
    n = 1
    for s in shape:
        n *= s
    return n * jnp.dtype(dtype).itemsize


def _params(est_bytes, sem=None):
    limit = int(min(VMEM_CAP, max(32 * 2**20, est_bytes * 3 // 2 + 8 * 2**20)))
    kw = dict(vmem_limit_bytes=limit)
    if sem is not None:
        kw["dimension_semantics"] = sem
    return pltpu.CompilerParams(**kw)


def _raw_dot(a, b, ca, cb):
    return lax.dot_general(a.astype(MXU_DTYPE), b.astype(MXU_DTYPE), (((ca,), (cb,)), ((), ())),
                           preferred_element_type=F32)


@functools.partial(jax.custom_vjp, nondiff_argnums=(2, 3))
def bdot(a, b, ca, cb):
    return _raw_dot(a, b, ca, cb)


def _bdot_fwd(a, b, ca, cb):
    return _raw_dot(a, b, ca, cb), (a, b)


def _bdot_bwd(ca, cb, res, g):
    a, b = res
    if (ca, cb) == (1, 0):
        da, db = bdot(g, b, 1, 1), bdot(a, g, 0, 0)
    elif (ca, cb) == (1, 1):
        da, db = bdot(g, b, 1, 0), bdot(g, a, 0, 0)
    else:
        assert (ca, cb) == (0, 0)
        da, db = bdot(b, g, 1, 1), bdot(a, g, 1, 0)
    return da.astype(a.dtype), db.astype(b.dtype)


bdot.defvjp(_bdot_fwd, _bdot_bwd)


def hdot(a, b, ca=1, cb=0):
    return lax.dot_general(a, b, (((ca,), (cb,)), ((), ())), precision=lax.Precision.HIGHEST,
                           preferred_element_type=F32)


def _silu(x):
    return x * jax.nn.sigmoid(x)


def _softplus(x):
    return jnp.maximum(x, 0.0) + jnp.log1p(jnp.exp(-jnp.abs(x)))


def matmul(name, a, b, mode, out_dtype, add=None):
    if mode == "nn":
        (M, K), (K2, N) = a.shape, b.shape
    elif mode == "nt":
        (M, K), (N, K2) = a.shape, b.shape
    else:
        (K, M), (K2, N) = a.shape, b.shape
    assert K == K2, (name, a.shape, b.shape, mode)
    tm, tn = _tile(M, 1024), _tile(N, 1024)
    tk = _tile(K, 2048 if mode != "tn" else 1024)
    nk = K // tk
    dims = {"nn": (1, 0), "nt": (1, 1), "tn": (0, 0)}[mode]
    a_spec = (pl.BlockSpec((tk, tm), lambda i, j, k: (k, i)) if mode == "tn"
              else pl.BlockSpec((tm, tk), lambda i, j, k: (i, k)))
    b_spec = (pl.BlockSpec((tn, tk), lambda i, j, k: (j, k)) if mode == "nt"
              else pl.BlockSpec((tk, tn), lambda i, j, k: (k, j)))
    o_spec = pl.BlockSpec((tm, tn), lambda i, j, k: (i, j))
    has_add = add is not None

    def body(*refs):
        a_ref, b_ref = refs[0], refs[1]
        c_ref = refs[2] if has_add else None
        o_ref = refs[3] if has_add else refs[2]
        p = _raw_dot(a_ref[...], b_ref[...], *dims)
        if nk == 1:
            if has_add:
                p = p + c_ref[...].astype(F32)
            o_ref[...] = p.astype(o_ref.dtype)
            return
        acc = refs[-1]
        k = pl.program_id(2)

        @pl.when(k == 0)
        def _():
            acc[...] = p + c_ref[...].astype(F32) if has_add else p

        @pl.when(k > 0)
        def _():
            acc[...] += p

        @pl.when(k == nk - 1)
        def _():
            o_ref[...] = acc[...].astype(o_ref.dtype)

    est = 2 * (_nbytes((tm, tk), a.dtype) + _nbytes((tk, tn), b.dtype) + _nbytes((tm, tn), out_dtype))
    est += 3 * _nbytes((tm, tn), F32) + _nbytes((tm, tk), MXU_DTYPE) + _nbytes((tk, tn), MXU_DTYPE)
    if has_add:
        est += 2 * _nbytes((tm, tn), add.dtype)
    args = (a, b) + ((add,) if has_add else ())
    in_specs = [a_spec, b_spec] + ([o_spec] if has_add else [])
    return pl.pallas_call(
        body, name=name, grid=(M // tm, N // tn, nk),
        in_specs=in_specs, out_specs=o_spec,
        out_shape=jax.ShapeDtypeStruct((M, N), out_dtype),
        scratch_shapes=[pltpu.VMEM((tm, tn), F32)] if nk > 1 else [],
        compiler_params=_params(est, ("parallel", "parallel", "arbitrary")),
    )(*args)


def row_call(name, f, rows, pars, outs, tm):
    L = rows[0].shape[0]
    tm = min(tm, L)
    nr, npar = len(rows), len(pars)

    def body(*refs):
        vals = [r[...] for r in refs[:nr + npar]]
        res = f(*vals)
        for o_ref, v in zip(refs[nr + npar:], res):
            o_ref[...] = v.astype(o_ref.dtype)

    est = 2 * sum(_nbytes((tm, a.shape[1]), a.dtype) for a in rows) + 2 * sum(_nbytes((tm, w), d) for w, d in outs)
    est += 4 * sum(_nbytes((tm, a.shape[1]), F32) for a in rows)
    res = pl.pallas_call(
        body, name=name, grid=(L // tm,),
        in_specs=[pl.BlockSpec((tm, a.shape[1]), lambda i: (i, 0)) for a in rows]
        + [pl.BlockSpec(p.shape, lambda i: (0, 0)) for p in pars],
        out_specs=[pl.BlockSpec((tm, w), lambda i: (i, 0)) for w, _ in outs],
        out_shape=[jax.ShapeDtypeStruct((L, w), d) for w, d in outs],
        compiler_params=_params(est, ("parallel",)),
    )(*rows, *pars)
    return res


def row_vjp_call(name, f, rows, pars, cots, row_grad_dtypes, par_grads, tm, adds=None):
    L = rows[0].shape[0]
    tm = min(tm, L)
    nr, npar, nc = len(rows), len(pars), len(cots)
    adds = adds or {}
    add_idx = sorted(adds)
    rg_idx = [i for i, d in enumerate(row_grad_dtypes) if d is not None]
    pg_idx = [i for i, w in enumerate(par_grads) if w]
    diff_idx = rg_idx + [nr + i for i in pg_idx]

    def body(*refs):
        vals = [r[...] for r in refs[:nr + npar]]
        cvals = [r[...] for r in refs[nr + npar:nr + npar + nc]]
        avals = [r[...] for r in refs[nr + npar + nc:nr + npar + nc + len(add_idx)]]
        out_refs = refs[nr + npar + nc + len(add_idx):]

        def g(*dv):
            full = list(vals)
            for i, v in zip(diff_idx, dv):
                full[i] = v
            return tuple(f(*full))

        res, vjp = jax.vjp(g, *[vals[i] for i in diff_idx])
        grads = vjp(tuple(c.astype(r.dtype) for c, r in zip(cvals, res)))
        for n, i in enumerate(rg_idx):
            gval = grads[n].astype(F32)
            if i in adds:
                gval = gval + avals[add_idx.index(i)].astype(F32)
            out_refs[n][...] = gval.astype(out_refs[n].dtype)
        first = pl.program_id(0) == 0
        for n, i in enumerate(pg_idx):
            o_ref = out_refs[len(rg_idx) + n]
            gval = grads[len(rg_idx) + n].astype(F32)

            @pl.when(first)
            def _(o_ref=o_ref, gval=gval):
                o_ref[...] = gval

            @pl.when(jnp.logical_not(first))
            def _(o_ref=o_ref, gval=gval):
                o_ref[...] += gval

    row_spec = lambda a: pl.BlockSpec((tm, a.shape[1]), lambda i: (i, 0))
    est = 2 * sum(_nbytes((tm, a.shape[1]), a.dtype) for a in list(rows) + list(cots))
    est += 10 * sum(_nbytes((tm, a.shape[1]), F32) for a in rows)
    res = pl.pallas_call(
        body, name=name, grid=(L // tm,),
        in_specs=[row_spec(a) for a in rows] + [pl.BlockSpec(p.shape, lambda i: (0, 0)) for p in pars]
        + [row_spec(c) for c in cots] + [row_spec(adds[i]) for i in add_idx],
        out_specs=[row_spec(rows[i]) for i in rg_idx] + [pl.BlockSpec(pars[i].shape, lambda i_: (0, 0)) for i in pg_idx],
        out_shape=[jax.ShapeDtypeStruct(rows[i].shape, row_grad_dtypes[i]) for i in rg_idx]
        + [jax.ShapeDtypeStruct(pars[i].shape, F32) for i in pg_idx],
        compiler_params=_params(est, ("arbitrary",)),
    )(*rows, *pars, *cots, *[adds[i] for i in add_idx])
    return list(res[:len(rg_idx)]), list(res[len(rg_idx):])


def _halo_specs(rows, T, nt, rev):
    specs = []
    for (_, w, ci, hs) in rows:
        if rev:
            specs.append(pl.BlockSpec((T, w), lambda j, ci=ci: (nt - 1 - j, ci)))
        else:
            specs.append(pl.BlockSpec((T, w), lambda i, ci=ci: (i, ci)))
        if hs:
            r = T // hs
            if rev:
                specs.append(pl.BlockSpec((hs, w), lambda j, ci=ci, r=r: (jnp.maximum((nt - 1 - j) * r - 1, 0), ci)))
            else:
                specs.append(pl.BlockSpec((hs, w), lambda i, ci=ci, r=r: (jnp.maximum(i * r - 1, 0), ci)))
    return specs


def _halo_args(rows):
    args = []
    for (a, _, _, hs) in rows:
        args.append(a)
        if hs:
            args.append(a)
    return args


def _halo_vals(rows, refs):
    vals, n = [], 0
    for (_, _, _, hs) in rows:
        if hs:
            vals.append((refs[n + 1][...], refs[n][...]))
            n += 2
        else:
            vals.append(refs[n][...])
            n += 1
    return vals, n


def halo_call(name, f, rows, pars, outs, T):
    L = rows[0][0].shape[0]
    T = min(T, L)
    nt = L // T
    npar = len(pars)

    def body(*refs):
        i = pl.program_id(0)
        vals, n = _halo_vals(rows, refs)
        pv = [r[...] for r in refs[n:n + npar]]
        res = f(i == 0, i * T, *vals, *pv)
        for o_ref, v in zip(refs[n + npar:], res):
            o_ref[...] = v.astype(o_ref.dtype)

    est = 2 * sum(_nbytes((T, w), a.dtype) for a, w, _, _ in rows) + 2 * sum(_nbytes((T, w), d) for w, d in outs)
    est += 8 * sum(_nbytes((T, w), F32) for _, w, _, _ in rows)
    return pl.pallas_call(
        body, name=name, grid=(nt,),
        in_specs=_halo_specs(rows, T, nt, False) + [pl.BlockSpec(p.shape, lambda i, nd=p.ndim: (0,) * nd) for p in pars],
        out_specs=[pl.BlockSpec((T, w), lambda i: (i, 0)) for w, _ in outs],
        out_shape=[jax.ShapeDtypeStruct((L, w), d) for w, d in outs],
        compiler_params=_params(est, ("parallel",)),
    )(*_halo_args(rows), *pars)


def halo_vjp_call(name, f, rows, pars, cots, row_grad_dtypes, par_grads, T):
    L = rows[0][0].shape[0]
    T = min(T, L)
    nt = L // T
    nr, npar, nc = len(rows), len(pars), len(cots)
    pg_idx = [i for i, w in enumerate(par_grads) if w]
    halo_idx = [i for i, r in enumerate(rows) if r[3]]

    def body(*refs):
        j = pl.program_id(0)
        i = nt - 1 - j
        vals, n = _halo_vals(rows, refs)
        pv = [r[...] for r in refs[n:n + npar]]
        cv = [r[...] for r in refs[n + npar:n + npar + nc]]
        out_refs = refs[n + npar + nc:n + npar + nc + nr + len(pg_idx)]
        carries = refs[n + npar + nc + nr + len(pg_idx):]
        first = i == 0

        def g(vals_, pv_):
            return tuple(f(first, i * T, *vals_, *pv_))

        res, vjp = jax.vjp(g, vals, pv)
        dvals, dpv = vjp(tuple(c.astype(r.dtype) for c, r in zip(cv, res)))

        @pl.when(j == 0)
        def _():
            for c_ref in carries:
                c_ref[...] = jnp.zeros_like(c_ref)

        for k in range(nr):
            hs = rows[k][3]
            o_ref = out_refs[k]
            if hs:
                dh, dc = dvals[k]
                c_ref = carries[halo_idx.index(k)]
                dc = dc.astype(F32)
                if hs == T:
                    o_ref[...] = (dc + c_ref[...]).astype(o_ref.dtype)
                else:
                    o_ref[0:T - hs, :] = dc[0:T - hs].astype(o_ref.dtype)
                    o_ref[T - hs:T, :] = (dc[T - hs:T] + c_ref[...]).astype(o_ref.dtype)
                c_ref[...] = dh.astype(F32)
            else:
                o_ref[...] = dvals[k].astype(o_ref.dtype)
        for m, k in enumerate(pg_idx):
            o_ref = out_refs[nr + m]
            gval = dpv[k].astype(F32)

            @pl.when(j == 0)
            def _(o_ref=o_ref, gval=gval):
                o_ref[...] = gval

            @pl.when(j > 0)
            def _(o_ref=o_ref, gval=gval):
                o_ref[...] += gval

    est = 2 * sum(_nbytes((T, w), a.dtype) for a, w, _, _ in rows) + 2 * sum(_nbytes((T, c.shape[1]), c.dtype) for c in cots)
    est += 12 * sum(_nbytes((T, w), F32) for _, w, _, _ in rows)
    res = pl.pallas_call(
        body, name=name, grid=(nt,),
        in_specs=_halo_specs(rows, T, nt, True) + [pl.BlockSpec(p.shape, lambda j, nd=p.ndim: (0,) * nd) for p in pars]
        + [pl.BlockSpec((T, c.shape[1]), lambda j: (nt - 1 - j, 0)) for c in cots],
        out_specs=[pl.BlockSpec((T, w), lambda j: (nt - 1 - j, 0)) for _, w, _, _ in rows]
        + [pl.BlockSpec(pars[k].shape, lambda j, nd=pars[k].ndim: (0,) * nd) for k in pg_idx],
        out_shape=[jax.ShapeDtypeStruct((L, w), row_grad_dtypes[k]) for k, (_, w, _, _) in enumerate(rows)]
        + [jax.ShapeDtypeStruct(pars[k].shape, F32) for k in pg_idx],
        scratch_shapes=[pltpu.VMEM((rows[k][3], rows[k][1]), F32) for k in halo_idx],
        compiler_params=_params(est, ("arbitrary",)),
    )(*_halo_args(rows), *pars, *cots)
    return list(res[:nr]), list(res[nr:])


def f_rms(x, w):
    x = x.astype(F32)
    return (x * lax.rsqrt(jnp.mean(x * x, axis=-1, keepdims=True) + EPS) * w,)


def f_swiglu(gu):
    dff = gu.shape[1] // 2
    return (_silu(gu[:, :dff]) * gu[:, dff:],)


def f_merge(gl, a, p, s):
    d = a.shape[1]
    g = jax.nn.sigmoid(gl.astype(F32))
    return (g[:, :d] * a.astype(F32) + g[:, d:2 * d] * p.astype(F32) + g[:, 2 * d:] * s.astype(F32),)


def f_gnorm(y, z, nw):
    dm = Dims()
    g = y * _silu(z.astype(F32))
    outs = []
    for gi in range(dm.G):
        gg = g[:, gi * dm.GW:(gi + 1) * dm.GW]
        outs.append(gg * lax.rsqrt(jnp.mean(gg * gg, axis=-1, keepdims=True) + EPS))
    return (jnp.concatenate(outs, axis=1) * nw,)


def f_conv(first, row0, xs, w, b):
    halo, cur = xs
    halo = jnp.where(first, 0.0, halo)
    ext = jnp.concatenate([halo, cur], axis=0)
    T = cur.shape[0]
    base = CONV_HALO - (CONV_K - 1)
    pre = b
    for k in range(CONV_K):
        pre = pre + w[k:k + 1, :] * ext[base + k:base + k + T]
    return (_silu(pre),)


def f_pool(first, row0, us, pw, scale):
    dm = Dims()
    halo, cur = us
    halo = jnp.where(first, 0.0, halo)
    ext = jnp.concatenate([halo, cur], axis=0)
    T = cur.shape[0]
    t = row0 + lax.broadcasted_iota(jnp.int32, (T, 1), 0)
    outs = []
    for gi, w in enumerate(POOL_WINDOWS):
        assert w & (w - 1) == 0 and w <= POOL_HALO
        s = ext[:, gi * dm.PC:(gi + 1) * dm.PC]
        sh = 1
        while sh < w:
            s = s + jnp.concatenate([jnp.zeros((sh, dm.PC), F32), s[:-sh]], axis=0)
            sh *= 2
        cnt = jnp.minimum(t + 1, w).astype(F32)
        mixed = s[POOL_HALO:] / cnt - cur[:, gi * dm.PC:(gi + 1) * dm.PC]
        outs.append(bdot(mixed, pw[gi], 1, 0))
    return (jnp.concatenate(outs, axis=1) * scale,)


def f_attn(first, row0, q, ks, vs, sink):
    dm = Dims()
    kp, kc = ks
    vp, vc = vs
    T = q.shape[0]
    hd = ATT_HEAD_DIM
    qi = lax.broadcasted_iota(jnp.int32, (T, 2 * T), 0)
    si = lax.broadcasted_iota(jnp.int32, (T, 2 * T), 1)
    diff = qi + T - si
    ok = (diff >= 0) & (diff < WINDOW) & ((si >= T) | jnp.logical_not(first))
    mask = jnp.concatenate([ok] * dm.GQ, axis=0)
    outs = []
    for k in range(ATT_KV_HEADS):
        kk = jnp.concatenate([kp[:, k * hd:(k + 1) * hd], kc[:, k * hd:(k + 1) * hd]], axis=0)
        vv = jnp.concatenate([vp[:, k * hd:(k + 1) * hd], vc[:, k * hd:(k + 1) * hd]], axis=0)
        heads = [k * dm.GQ + g for g in range(dm.GQ)]
        qs = jnp.concatenate([q[:, h * hd:(h + 1) * hd] for h in heads], axis=0)
        s = bdot(qs, kk, 1, 1) * (hd ** -0.5)
        s = jnp.where(mask, s, NEG)
        sk = jnp.concatenate([jnp.broadcast_to(sink[:, h:h + 1], (T, 1)) for h in heads], axis=0)
        m = jnp.maximum(jnp.max(s, axis=-1, keepdims=True), sk)
        p = jnp.exp(s - m)
        den = jnp.sum(p, axis=-1, keepdims=True) + jnp.exp(sk - m)
        o = bdot(p / den, vv, 1, 0)
        outs += [o[g * T:(g + 1) * T] for g in range(dm.GQ)]
    return (jnp.concatenate(outs, axis=1),)


def f_ssd(xc, dtr, ht, dt_bias, a_log, d_skip):
    dm = Dims()
    Q = xc.shape[0]
    xs = xc[:, :dm.DI]
    bm = xc[:, dm.DI:dm.DI + dm.GN]
    cm = xc[:, dm.DI + dm.GN:]
    expand = (lax.broadcasted_iota(jnp.int32, (dm.H, dm.DI), 1) // dm.P
              == lax.broadcasted_iota(jnp.int32, (dm.H, dm.DI), 0)).astype(F32)
    ri = lax.broadcasted_iota(jnp.int32, (Q, Q), 0)
    ci = lax.broadcasted_iota(jnp.int32, (Q, Q), 1)
    causal = ri >= ci
    tinc = causal.astype(F32)
    dt = _softplus(dtr[:, :dm.H] + dt_bias)
    da = dt * (-jnp.exp(a_log))
    acs = hdot(tinc, da)
    acs_t = hdot(da, tinc, 0, 1)
    eacs = jnp.exp(acs)
    dend = jnp.exp(acs[Q - 1:Q, :] - acs)
    ex = hdot(jnp.concatenate([dt, eacs, dend, jnp.broadcast_to(d_skip, (8, dm.H))], axis=0), expand)
    dt_x, eacs_x, dend_x, dsk_x = ex[:Q], ex[Q:2 * Q], ex[2 * Q:3 * Q], ex[3 * Q:3 * Q + 1]
    xdt = xs * dt_x
    ys, hts = [], []
    for g in range(dm.G):
        gs = slice(g * dm.GW, (g + 1) * dm.GW)
        bg = bm[:, g * dm.N:(g + 1) * dm.N]
        cg = cm[:, g * dm.N:(g + 1) * dm.N]
        cb = bdot(cg, bg, 1, 1)
        y_off = bdot(cg, ht[:, gs], 1, 0) * eacs_x[:, gs]
        xg = xdt[:, gs]
        st = bdot(bg, xg * dend_x[:, gs], 0, 0)
        hts.append(ht[:, gs] * eacs_x[Q - 1:Q, gs] + st)
        yd = []
        for e in range(dm.HPG):
            h = g * dm.HPG + e
            seg = acs[:, h:h + 1] - acs_t[h:h + 1, :]
            lm = jnp.exp(jnp.where(causal, seg, NEG))
            yd.append(bdot(cb * lm, xg[:, e * dm.P:(e + 1) * dm.P], 1, 0))
        ys.append(jnp.concatenate(yd, axis=1) + y_off)
    y = jnp.concatenate(ys, axis=1) + dsk_x * xs
    return y, jnp.concatenate(hts, axis=1)


def ssd_fwd(name, xc, dtr, dt_bias, a_log, d_skip):
    dm = Dims()
    L = xc.shape[0]
    Q = CHUNK
    nc = L // Q

    def body(xc_ref, dtr_ref, b_ref, a_ref, s_ref, y_ref, hts_ref, ht):
        @pl.when(pl.program_id(0) == 0)
        def _():
            ht[...] = jnp.zeros_like(ht)

        h0 = ht[...]
        hts_ref[0] = h0
        y, h1 = f_ssd(xc_ref[...], dtr_ref[...], h0, b_ref[...], a_ref[...], s_ref[...])
        y_ref[...] = y
        ht[...] = h1

    par = pl.BlockSpec((1, dm.H), lambda c: (0, 0))
    est = 40 * _nbytes((Q, dm.CC), F32) + 4 * _nbytes((dm.N, dm.DI), F32)
    return pl.pallas_call(
        body, name=name, grid=(nc,),
        in_specs=[pl.BlockSpec((Q, dm.CC), lambda c: (c, 0)), pl.BlockSpec((Q, dm.DTP), lambda c: (c, 0)), par, par, par],
        out_specs=[pl.BlockSpec((Q, dm.DI), lambda c: (c, 0)), pl.BlockSpec((1, dm.N, dm.DI), lambda c: (c, 0, 0))],
        out_shape=[jax.ShapeDtypeStruct((L, dm.DI), F32), jax.ShapeDtypeStruct((nc, dm.N, dm.DI), F32)],
        scratch_shapes=[pltpu.VMEM((dm.N, dm.DI), F32)],
        compiler_params=_params(est, ("arbitrary",)),
    )(xc, dtr, dt_bias, a_log, d_skip)


def ssd_bwd(name, xc, dtr, hts, dt_bias, a_log, d_skip, dy):
    dm = Dims()
    L = xc.shape[0]
    Q = CHUNK
    nc = L // Q

    def body(xc_ref, dtr_ref, hts_ref, b_ref, a_ref, s_ref, dy_ref, dxc_ref, ddtr_ref, db_ref, da_ref, ds_ref, dht):
        j = pl.program_id(0)

        @pl.when(j == 0)
        def _():
            dht[...] = jnp.zeros_like(dht)

        _, vjp = jax.vjp(f_ssd, xc_ref[...], dtr_ref[...], hts_ref[0], b_ref[...], a_ref[...], s_ref[...])
        dxc, ddtr, dh0, db, da, ds = vjp((dy_ref[...], dht[...]))
        dxc_ref[...] = dxc.astype(dxc_ref.dtype)
        ddtr_ref[...] = ddtr.astype(ddtr_ref.dtype)
        dht[...] = dh0
        for o_ref, gval in ((db_ref, db), (da_ref, da), (ds_ref, ds)):
            @pl.when(j == 0)
            def _(o_ref=o_ref, gval=gval):
                o_ref[...] = gval

            @pl.when(j > 0)
            def _(o_ref=o_ref, gval=gval):
                o_ref[...] += gval

    par = pl.BlockSpec((1, dm.H), lambda j: (0, 0))
    rev = lambda w: pl.BlockSpec((Q, w), lambda j: (nc - 1 - j, 0))
    est = 80 * _nbytes((Q, dm.CC), F32) + 6 * _nbytes((dm.N, dm.DI), F32)
    return pl.pallas_call(
        body, name=name, grid=(nc,),
        in_specs=[rev(dm.CC), rev(dm.DTP), pl.BlockSpec((1, dm.N, dm.DI), lambda j: (nc - 1 - j, 0, 0)), par, par, par, rev(dm.DI)],
        out_specs=[rev(dm.CC), rev(dm.DTP), par, par, par],
        out_shape=[jax.ShapeDtypeStruct((L, dm.CC), F32), jax.ShapeDtypeStruct((L, dm.DTP), BF16)]
        + [jax.ShapeDtypeStruct((1, dm.H), F32)] * 3,
        scratch_shapes=[pltpu.VMEM((dm.N, dm.DI), F32)],
        compiler_params=_params(est, ("arbitrary",)),
    )(xc, dtr, hts, dt_bias, a_log, d_skip, dy)


def loss_head(x, w, target, tm):
    L, D = x.shape
    tm = min(tm, L)

    def tile_loss(xv, wv, tv):
        (y,) = f_rms(xv, wv)
        return 0.5 * jnp.sum(jnp.mean(jnp.square(y - tv), axis=-1))

    def body(x_ref, w_ref, t_ref, dx_ref, dw_ref, loss_ref):
        val, (dx, dw) = jax.value_and_grad(tile_loss, argnums=(0, 1))(x_ref[...], w_ref[...], t_ref[...])
        dx_ref[...] = dx
        first = pl.program_id(0) == 0
        lv = jnp.full((8, 128), val, F32)

        @pl.when(first)
        def _():
            dw_ref[...] = dw
            loss_ref[...] = lv

        @pl.when(jnp.logical_not(first))
        def _():
            dw_ref[...] += dw
            loss_ref[...] += lv

    row = pl.BlockSpec((tm, D), lambda i: (i, 0))
    est = 16 * _nbytes((tm, D), F32)
    return pl.pallas_call(
        body, name="loss_head", grid=(L // tm,),
        in_specs=[row, pl.BlockSpec((1, D), lambda i: (0, 0)), row],
        out_specs=[row, pl.BlockSpec((1, D), lambda i: (0, 0)), pl.BlockSpec((8, 128), lambda i: (0, 0))],
        out_shape=[jax.ShapeDtypeStruct((L, D), F32), jax.ShapeDtypeStruct((1, D), F32), jax.ShapeDtypeStruct((8, 128), F32)],
        compiler_params=_params(est, ("arbitrary",)),
    )(x, w, target)


def _adamw(w, g, m, v):
    m = ADAM_B1 * m + (1.0 - ADAM_B1) * g
    v = ADAM_B2 * v + (1.0 - ADAM_B2) * jnp.square(g)
    m_hat = m / (1.0 - ADAM_B1 ** ADAM_STEP)
    v_hat = v / (1.0 - ADAM_B2 ** ADAM_STEP)
    delta = -ADAM_LR * (m_hat / (jnp.sqrt(v_hat) + ADAM_EPS) + ADAM_WD * w)
    return delta, m, v


def adamw_sharded(parts, w, m, v):
    R = w.shape[0]
    tr = _tile(R, 256, 16)

    def body(p_ref, w_ref, m_ref, v_ref, g_ref, d_ref, nm_ref, nv_ref):
        g = p_ref[0].astype(F32)
        for s in range(1, N_DEV):
            g = g + p_ref[s].astype(F32)
        d, nm, nv = _adamw(w_ref[...], g, m_ref[...], v_ref[...])
        g_ref[...] = g
        d_ref[...] = d
        nm_ref[...] = nm
        nv_ref[...] = nv

    row = pl.BlockSpec((tr, PACK_W), lambda i: (i, 0))
    est = 2 * _nbytes((N_DEV, tr, PACK_W), parts.dtype) + 20 * _nbytes((tr, PACK_W), F32)
    return pl.pallas_call(
        body, name="adamw_sharded", grid=(R // tr,),
        in_specs=[pl.BlockSpec((N_DEV, tr, PACK_W), lambda i: (0, i, 0)), row, row, row],
        out_specs=[row] * 4,
        out_shape=[jax.ShapeDtypeStruct((R, PACK_W), F32)] * 4,
        compiler_params=_params(est, ("parallel",)),
    )(parts, w, m, v)


def _my_place():
    return lax.axis_index("x"), lax.axis_index("y"), lax.axis_index("c")


def all_gather_hbm(name, shard):
    R, W = shard.shape

    def body(x_ref, out_ref, send_sems, recv_sems, local_sem):
        x, y, c = _my_place()
        me, sibling = (x, y, c), (x, y, 1 - c)
        chips = [(1 - x, y), (x, 1 - y), (1 - x, 1 - y)]

        def slot(px, py, pc):
            return out_ref.at[4 * px + 2 * py + pc]

        def copy(k, block, to, src=None):
            return pltpu.make_async_remote_copy(
                src_ref=slot(*block) if src is None else src, dst_ref=slot(*block),
                send_sem=send_sems.at[k], recv_sem=recv_sems.at[k], device_id=to, device_id_type=MESH_T)

        mine = pltpu.make_async_copy(x_ref, slot(*me), local_sem)
        mine.start()
        first = [copy(0, me, sibling, src=x_ref)]
        first += [copy(1 + j, me, (*chip, c), src=x_ref) for j, chip in enumerate(chips)]
        for cp in first:
            cp.start()
        passed = [copy(4 + j, (*chip, c), sibling) for j, chip in enumerate(chips)]
        for j, chip in enumerate(chips):
            copy(1 + j, (*chip, c), me).wait_recv()
            passed[j].start()
        copy(0, sibling, me).wait_recv()
        for j, chip in enumerate(chips):
            copy(4 + j, (*chip, 1 - c), me).wait_recv()
        for cp in first + passed:
            cp.wait_send()
        mine.wait()

    return pl.pallas_call(
        body, name=name,
        out_shape=jax.ShapeDtypeStruct((N_DEV, R, W), shard.dtype),
        in_specs=[pl.BlockSpec(memory_space=pl.ANY)],
        out_specs=pl.BlockSpec(memory_space=pl.ANY),
        scratch_shapes=[pltpu.SemaphoreType.DMA((7,)), pltpu.SemaphoreType.DMA((7,)), pltpu.SemaphoreType.DMA],
    )(shard)


def exchange_hbm(name, blocks):
    n, R, W = blocks.shape
    assert n == N_DEV

    def body(g_ref, out_ref, send_sems, recv_sems, local_sem):
        x, y, c = _my_place()
        me = 4 * x + 2 * y + c
        mine = pltpu.make_async_copy(g_ref.at[me], out_ref.at[me], local_sem)
        mine.start()
        copies, arrivals = [], []
        for k in range(1, N_DEV):
            px, py, pc = (x + (k >> 2)) % 2, (y + ((k >> 1) & 1)) % 2, (c + (k & 1)) % 2
            p = 4 * px + 2 * py + pc
            copies.append(pltpu.make_async_remote_copy(
                src_ref=g_ref.at[p], dst_ref=out_ref.at[me], send_sem=send_sems.at[k - 1],
                recv_sem=recv_sems.at[k - 1], device_id=(px, py, pc), device_id_type=MESH_T))
            arrivals.append(pltpu.make_async_remote_copy(
                src_ref=g_ref.at[p], dst_ref=out_ref.at[p], send_sem=send_sems.at[k - 1],
                recv_sem=recv_sems.at[k - 1], device_id=(px, py, pc), device_id_type=MESH_T))
        for cp in copies:
            cp.start()
        for cp in arrivals:
            cp.wait_recv()
        for cp in copies:
            cp.wait_send()
        mine.wait()

    return pl.pallas_call(
        body, name=name,
        out_shape=jax.ShapeDtypeStruct((N_DEV, R, W), blocks.dtype),
        in_specs=[pl.BlockSpec(memory_space=pl.ANY)],
        out_specs=pl.BlockSpec(memory_space=pl.ANY),
        scratch_shapes=[pltpu.SemaphoreType.DMA((7,)), pltpu.SemaphoreType.DMA((7,)), pltpu.SemaphoreType.DMA],
    )(blocks)


def small_allreduce_adamw(part, w, m, v):
    R, W = part.shape

    def body(x_ref, w_ref, m_ref, v_ref, g_ref, d_ref, nm_ref, nv_ref, all_ref, send_sems, recv_sems, local_sem):
        x, y, c = _my_place()
        me, sibling = (x, y, c), (x, y, 1 - c)
        chips = [(1 - x, y), (x, 1 - y), (1 - x, 1 - y)]

        def slot(px, py, pc):
            return all_ref.at[4 * px + 2 * py + pc]

        def copy(k, block, to, src=None):
            return pltpu.make_async_remote_copy(
                src_ref=slot(*block) if src is None else src, dst_ref=slot(*block),
                send_sem=send_sems.at[k], recv_sem=recv_sems.at[k], device_id=to, device_id_type=MESH_T)

        mine = pltpu.make_async_copy(x_ref, slot(*me), local_sem)
        mine.start()
        first = [copy(0, me, sibling, src=x_ref)]
        first += [copy(1 + j, me, (*chip, c), src=x_ref) for j, chip in enumerate(chips)]
        for cp in first:
            cp.start()
        passed = [copy(4 + j, (*chip, c), sibling) for j, chip in enumerate(chips)]
        for j, chip in enumerate(chips):
            copy(1 + j, (*chip, c), me).wait_recv()
            passed[j].start()
        copy(0, sibling, me).wait_recv()
        for j, chip in enumerate(chips):
            copy(4 + j, (*chip, 1 - c), me).wait_recv()
        for cp in first + passed:
            cp.wait_send()
        mine.wait()
        g = all_ref[0]
        for s in range(1, N_DEV):
            g = g + all_ref[s]
        d, nm, nv = _adamw(w_ref[...], g, m_ref[...], v_ref[...])
        g_ref[...] = g
        d_ref[...] = d
        nm_ref[...] = nm
        nv_ref[...] = nv

    vm = pl.BlockSpec(memory_space=pltpu.VMEM)
    return pl.pallas_call(
        body, name="small_allreduce_adamw",
        out_shape=[jax.ShapeDtypeStruct((R, W), F32)] * 4,
        in_specs=[vm] * 4, out_specs=[vm] * 4,
        scratch_shapes=[pltpu.VMEM((N_DEV, R, W), F32), pltpu.SemaphoreType.DMA((7,)), pltpu.SemaphoreType.DMA((7,)),
                        pltpu.SemaphoreType.DMA],
    )(part, w, m, v)


SHARDED = ("w_in", "conv_w", "pool_w", "w_attn_br", "w_pool_br", "w_ssm_br", "w_out", "w_gate_up", "w_down")
SHARD_AXIS = {"w_in": 1, "conv_w": 1, "pool_w": 1, "w_attn_br": 1, "w_pool_br": 1, "w_ssm_br": 0, "w_out": 0,
              "w_gate_up": 1, "w_down": 0}
REPLICATED = ("ln1_w", "attn_sink", "conv_b", "dt_bias", "a_log", "d_skip", "ssm_norm_w", "pool_scale", "ln2_w")


def _rows_of(size):
    return -(-size // (16 * PACK_W)) * 16


def _pack(arrs, rows_total, dtype):
    parts, used = [], 0
    for a in arrs:
        r = _rows_of(a.size)
        flat = a.reshape(-1).astype(dtype)
        parts.append(jnp.pad(flat, (0, r * PACK_W - a.size)).reshape(r, PACK_W))
        used += r
    if rows_total > used:
        parts.append(jnp.zeros((rows_total - used, PACK_W), dtype))
    return jnp.concatenate(parts, axis=0)


def _unpack(buf, shapes):
    lead = buf.shape[:-2]
    out, r0 = [], 0
    for shp in shapes:
        size = 1
        for s in shp:
            size *= s
        r = _rows_of(size)
        flat = buf[..., r0:r0 + r, :].reshape(lead + (r * PACK_W,))[..., :size]
        out.append(flat.reshape(lead + tuple(shp)))
        r0 += r
    return out


def _chunked_rows(shapes):
    used = sum(_rows_of(_size(s)) for s in shapes)
    per = -(-used // COMM_CHUNKS)
    per = -(-per // 16) * 16
    return per * COMM_CHUNKS, per


def _size(shape):
    n = 1
    for s in shape:
        n *= s
    return n


def _layer_forward(dm, x, wts, rep, li):
    tag = f"l{li}_"
    sv = {"x": x}
    (h,) = row_call(tag + "rms1", f_rms, [x], [rep["ln1_w"]], [(dm.D, BF16)], 256)
    sv["h"] = h
    qkv = matmul(tag + "p_qkv", h, wts["w_qkv"], "nn", BF16)
    u = matmul(tag + "p_u", h, wts["w_u"], "nn", F32)
    z = matmul(tag + "p_z", h, wts["w_z"], "nn", F32)
    xbc = matmul(tag + "p_xbc", h, wts["w_xbc"], "nn", F32)
    dtr = matmul(tag + "p_dt", h, wts["w_dt"], "nn", F32)
    gl = matmul(tag + "p_gl", h, wts["w_gl"], "nn", F32)
    sv.update(qkv=qkv, u=u, z=z, xbc=xbc, dtr=dtr, gl=gl)
    kvi = dm.AW // dm.KVW
    (att,) = halo_call(tag + "attn", f_attn,
                       [(qkv, dm.AW, 0, 0), (qkv, dm.KVW, kvi, WINDOW), (qkv, dm.KVW, kvi + 1, WINDOW)],
                       [rep["attn_sink"]], [(dm.AW, BF16)], WINDOW)
    (pool,) = halo_call(tag + "pool", f_pool, [(u, dm.PW, 0, POOL_HALO)], [wts["pool_w"], rep["pool_scale"]],
                        [(dm.PW, BF16)], 256)
    (xc,) = halo_call(tag + "conv", f_conv, [(xbc, dm.CC, 0, CONV_HALO)], [wts["conv_w"], rep["conv_b"]],
                      [(dm.CC, F32)], 256)
    y, hts = ssd_fwd(tag + "ssd", xc, dtr, rep["dt_bias"], rep["a_log"], rep["d_skip"])
    (ssm,) = row_call(tag + "gnorm", f_gnorm, [y, z], [rep["ssm_norm_w"]], [(dm.DI, BF16)], 256)
    sv.update(att=att, pool=pool, xc=xc, y=y, hts=hts, ssm=ssm)
    ba = matmul(tag + "br_a", att, wts["w_attn_br"], "nn", F32)
    bp = matmul(tag + "br_p", pool, wts["w_pool_br"], "nn", F32)
    bs = matmul(tag + "br_s", ssm, wts["w_ssm_br"], "nn", F32)
    (merged,) = row_call(tag + "merge", f_merge, [gl, ba, bp, bs], [], [(dm.D, BF16)], 256)
    x1 = matmul(tag + "out", merged, wts["w_out"], "nn", F32, add=x)
    (h2,) = row_call(tag + "rms2", f_rms, [x1], [rep["ln2_w"]], [(dm.D, BF16)], 256)
    gu = matmul(tag + "gu", h2, wts["w_gate_up"], "nn", F32)
    (act,) = row_call(tag + "swiglu", f_swiglu, [gu], [], [(dm.DFF, BF16)], 256)
    x2 = matmul(tag + "down", act, wts["w_down"], "nn", F32, add=x1)
    sv.update(ba=ba, bp=bp, bs=bs, merged=merged, x1=x1, h2=h2, gu=gu, act=act)
    return x2, sv


def _layer_backward(dm, dx2, sv, wts, rep, li):
    tag = f"l{li}_b_"
    gw, gr = {}, {}
    dact = matmul(tag + "d_act", dx2, wts["w_down"], "nt", F32)
    gw["w_down"] = matmul(tag + "g_down", sv["act"], dx2, "tn", BF16)
    (dgu,), _ = row_vjp_call(tag + "swiglu", f_swiglu, [sv["gu"]], [], [dact], [BF16], [], 256)
    dh2 = matmul(tag + "d_h2", dgu, wts["w_gate_up"], "nt", F32)
    gw["w_gate_up"] = matmul(tag + "g_gu", sv["h2"], dgu, "tn", BF16)
    (dx1,), (gr["ln2_w"],) = row_vjp_call(tag + "rms2", f_rms, [sv["x1"]], [rep["ln2_w"]], [dh2], [F32], [True], 256,
                                          adds={0: dx2})
    dmerged = matmul(tag + "d_merged", dx1, wts["w_out"], "nt", F32)
    gw["w_out"] = matmul(tag + "g_out", sv["merged"], dx1, "tn", BF16)
    (dgl, dba, dbp, dbs), _ = row_vjp_call(tag + "merge", f_merge, [sv["gl"], sv["ba"], sv["bp"], sv["bs"]], [],
                                           [dmerged], [BF16, BF16, BF16, BF16], [], 256)
    datt = matmul(tag + "d_att", dba, wts["w_attn_br"], "nt", F32)
    gw["w_attn_br"] = matmul(tag + "g_br_a", sv["att"], dba, "tn", BF16)
    dpool = matmul(tag + "d_pool", dbp, wts["w_pool_br"], "nt", F32)
    gw["w_pool_br"] = matmul(tag + "g_br_p", sv["pool"], dbp, "tn", BF16)
    dssm = matmul(tag + "d_ssm", dbs, wts["w_ssm_br"], "nt", F32)
    gw["w_ssm_br"] = matmul(tag + "g_br_s", sv["ssm"], dbs, "tn", BF16)
    (dy, dz), (gr["ssm_norm_w"],) = row_vjp_call(tag + "gnorm", f_gnorm, [sv["y"], sv["z"]], [rep["ssm_norm_w"]],
                                                 [dssm], [F32, BF16], [True], 256)
    dxc, ddtr, gr["dt_bias"], gr["a_log"], gr["d_skip"] = ssd_bwd(
        tag + "ssd", sv["xc"], sv["dtr"], sv["hts"], rep["dt_bias"], rep["a_log"], rep["d_skip"], dy)
    (dxbc,), (gw["conv_w"], gr["conv_b"]) = halo_vjp_call(
        tag + "conv", f_conv, [(sv["xbc"], dm.CC, 0, CONV_HALO)], [wts["conv_w"], rep["conv_b"]], [dxc],
        [BF16], [True, True], 256)
    (du,), (gw["pool_w"], gr["pool_scale"]) = halo_vjp_call(
        tag + "pool", f_pool, [(sv["u"], dm.PW, 0, POOL_HALO)], [wts["pool_w"], rep["pool_scale"]], [dpool],
        [BF16], [True, True], 256)
    kvi = dm.AW // dm.KVW
    qkv = sv["qkv"]
    (dq, dk, dv), (gr["attn_sink"],) = halo_vjp_call(
        tag + "attn", f_attn, [(qkv, dm.AW, 0, 0), (qkv, dm.KVW, kvi, WINDOW), (qkv, dm.KVW, kvi + 1, WINDOW)],
        [rep["attn_sink"]], [datt], [BF16, BF16, BF16], [True], WINDOW)
    dproj = jnp.concatenate([dq, dk, dv, du, dz, dxbc, ddtr, dgl], axis=1)
    dh = matmul(tag + "d_h", dproj, wts["w_in_int"], "nt", F32)
    gw["w_in_int"] = matmul(tag + "g_in", sv["h"], dproj, "tn", BF16)
    (dx,), (gr["ln1_w"],) = row_vjp_call(tag + "rms1", f_rms, [sv["x"]], [rep["ln1_w"]], [dh], [F32], [True], 256,
                                         adds={0: dx1})
    return dx, gw, gr


def kernel(x, ln1_w, w_in, attn_sink, conv_w, conv_b, dt_bias, a_log, d_skip, ssm_norm_w, pool_w, pool_scale, w_attn_br, w_pool_br, w_ssm_br, w_out, ln2_w, w_gate_up, w_down, final_w, loss_target, m_ln1_w, m_w_in, m_attn_sink, m_conv_w, m_conv_b, m_dt_bias, m_a_log, m_d_skip, m_ssm_norm_w, m_pool_w, m_pool_scale, m_w_attn_br, m_w_pool_br, m_w_ssm_br, m_w_out, m_ln2_w, m_w_gate_up, m_w_down, m_final_w, v_ln1_w, v_w_in, v_attn_sink, v_conv_w, v_conv_b, v_dt_bias, v_a_log, v_d_skip, v_ssm_norm_w, v_pool_w, v_pool_scale, v_w_attn_br, v_w_pool_br, v_w_ssm_br, v_w_out, v_ln2_w, v_w_gate_up, v_w_down, v_final_w):
    dm = Dims()
    W = dict(ln1_w=ln1_w, w_in=w_in, attn_sink=attn_sink, conv_w=conv_w, conv_b=conv_b, dt_bias=dt_bias, a_log=a_log,
             d_skip=d_skip, ssm_norm_w=ssm_norm_w, pool_w=pool_w, pool_scale=pool_scale, w_attn_br=w_attn_br,
             w_pool_br=w_pool_br, w_ssm_br=w_ssm_br, w_out=w_out, ln2_w=ln2_w, w_gate_up=w_gate_up, w_down=w_down,
             final_w=final_w)
    M = dict(ln1_w=m_ln1_w, w_in=m_w_in, attn_sink=m_attn_sink, conv_w=m_conv_w, conv_b=m_conv_b, dt_bias=m_dt_bias,
             a_log=m_a_log, d_skip=m_d_skip, ssm_norm_w=m_ssm_norm_w, pool_w=m_pool_w, pool_scale=m_pool_scale,
             w_attn_br=m_w_attn_br, w_pool_br=m_w_pool_br, w_ssm_br=m_w_ssm_br, w_out=m_w_out, ln2_w=m_ln2_w,
             w_gate_up=m_w_gate_up, w_down=m_w_down, final_w=m_final_w)
    V = dict(ln1_w=v_ln1_w, w_in=v_w_in, attn_sink=v_attn_sink, conv_w=v_conv_w, conv_b=v_conv_b, dt_bias=v_dt_bias,
             a_log=v_a_log, d_skip=v_d_skip, ssm_norm_w=v_ssm_norm_w, pool_w=v_pool_w, pool_scale=v_pool_scale,
             w_attn_br=v_w_attn_br, w_pool_br=v_w_pool_br, w_ssm_br=v_w_ssm_br, w_out=v_w_out, ln2_w=v_ln2_w,
             w_gate_up=v_w_gate_up, w_down=v_w_down, final_w=v_final_w)
    xl = x[0]
    target = loss_target[0]

    shard_shapes = [W[n].shape[1:] for n in SHARDED]
    ag_shapes = shard_shapes + [W["conv_w"].shape[1:]]
    ag_rows, ag_per = _chunked_rows(ag_shapes)
    full = []
    for li in range(DEPTH):
        arrs = [W[n][li] for n in SHARDED]
        conv_hi = W["conv_w"][li].astype(BF16)
        arrs.append(W["conv_w"][li] - conv_hi.astype(F32))
        buf = _pack(arrs, ag_rows, BF16)
        got = [all_gather_hbm(f"ag_l{li}_{ci}", buf[ci * ag_per:(ci + 1) * ag_per]) for ci in range(COMM_CHUNKS)]
        gathered = jnp.concatenate(got, axis=1)
        pieces = _unpack(gathered, ag_shapes)
        fw = {}
        for n, p in zip(SHARDED, pieces[:len(SHARDED)]):
            ax = SHARD_AXIS[n]
            fw[n] = jnp.concatenate([p[d] for d in range(N_DEV)], axis=ax)
        fw["pool_w"] = fw["pool_w"].astype(F32)
        conv_lo = jnp.concatenate([pieces[-1][d] for d in range(N_DEV)], axis=1)
        fw["conv_w"] = fw["conv_w"].astype(F32) + conv_lo.astype(F32)
        pts, acc = [], 0
        for wd in dm.in_widths:
            pts.append((acc, acc + wd))
            acc += wd
        win = fw["w_in"]
        cols = lambda k: win[:, pts[k][0]:pts[k][1]]
        fw["w_qkv"] = jnp.concatenate([cols(0), cols(1), cols(2)], axis=1)
        fw["w_u"], fw["w_z"], fw["w_xbc"] = cols(3), cols(4), cols(5)
        fw["w_dt"] = jnp.pad(cols(6), ((0, 0), (0, dm.DTP - dm.H)))
        fw["w_gl"] = cols(7)
        fw["w_in_int"] = jnp.concatenate([fw["w_qkv"], fw["w_u"], fw["w_z"], fw["w_xbc"], fw["w_dt"], fw["w_gl"]], axis=1)
        full.append(fw)

    rep = [{n: W[n][li].reshape(1, -1) for n in REPLICATED} for li in range(DEPTH)]

    saved = []
    xa = xl
    for li in range(DEPTH):
        xa, sv = _layer_forward(dm, xa, full[li], rep[li], li)
        saved.append(sv)
    dxa, g_final, loss_blk = loss_head(xa, W["final_w"].reshape(1, -1), target, 256)

    gws, grs = [None] * DEPTH, [None] * DEPTH
    for li in reversed(range(DEPTH)):
        dxa, gws[li], grs[li] = _layer_backward(dm, dxa, saved[li], full[li], rep[li], li)
    grad_x = dxa[None]

    rs_rows, rs_per = _chunked_rows(shard_shapes)
    g_sh, d_sh, m_sh, v_sh = [], [], [], []
    for li in range(DEPTH):
        gw = gws[li]
        gi = gw["w_in_int"]
        o = [0]
        for wd in dm.seg:
            o.append(o[-1] + wd)
        gw["w_in"] = jnp.concatenate([gi[:, :o[4]], gi[:, o[4]:o[4] + dm.H], gi[:, o[5]:]], axis=1)
        blocks = []
        for d in range(N_DEV):
            arrs = []
            for n, shp in zip(SHARDED, shard_shapes):
                ax = SHARD_AXIS[n]
                arrs.append(lax.slice_in_dim(gw[n], d * shp[ax], (d + 1) * shp[ax], axis=ax))
            blocks.append(_pack(arrs, rs_rows, BF16))
        blocks = jnp.stack(blocks)
        got = [exchange_hbm(f"rs_l{li}_{ci}", blocks[:, ci * rs_per:(ci + 1) * rs_per]) for ci in range(COMM_CHUNKS)]
        parts = jnp.concatenate(got, axis=1)
        wp = _pack([W[n][li] for n in SHARDED], rs_rows, F32)
        mp = _pack([M[n][li] for n in SHARDED], rs_rows, F32)
        vp = _pack([V[n][li] for n in SHARDED], rs_rows, F32)
        res = adamw_sharded(parts, wp, mp, vp)
        for lst, r in zip((g_sh, d_sh, m_sh, v_sh), res):
            lst.append(dict(zip(SHARDED, _unpack(r, shard_shapes))))

    small_names = [(n, li) for li in range(DEPTH) for n in REPLICATED] + [("final_w", None)]
    small_shape = lambda n, li: W[n].shape if li is None else W[n].shape[1:]
    small_rows = [-(-_size(small_shape(n, li)) // (8 * 128)) * 8 for n, li in small_names]
    loss_row = sum(small_rows)

    def small_pack(get, last=None):
        rows = []
        for (n, li), r in zip(small_names, small_rows):
            a = get(n, li).reshape(-1).astype(F32)
            rows.append(jnp.pad(a, (0, r * 128 - a.size)).reshape(r, 128))
        rows.append(jnp.zeros((8, 128), F32) if last is None else last)
        return jnp.concatenate(rows, axis=0)

    part = small_pack(lambda n, li: g_final if li is None else grs[li][n], loss_blk)
    wsm = small_pack(lambda n, li: W[n] if li is None else W[n][li])
    msm = small_pack(lambda n, li: M[n] if li is None else M[n][li])
    vsm = small_pack(lambda n, li: V[n] if li is None else V[n][li])
    sm = small_allreduce_adamw(part, wsm, msm, vsm)
    loss = sm[0][loss_row, 0]

    def small_unpack(buf):
        out, r0 = {}, 0
        for (n, li), r in zip(small_names, small_rows):
            shp = small_shape(n, li)
            out[(n, li)] = buf[r0:r0 + r].reshape(-1)[:_size(shp)].reshape(shp)
            r0 += r
        return out

    sm_g, sm_d, sm_m, sm_v = [small_unpack(b) for b in sm]

    def assemble(sharded_list, small):
        outs = []
        for n in ("ln1_w", "w_in", "attn_sink", "conv_w", "conv_b", "dt_bias", "a_log", "d_skip", "ssm_norm_w", "pool_w",
                  "pool_scale", "w_attn_br", "w_pool_br", "w_ssm_br", "w_out", "ln2_w", "w_gate_up", "w_down"):
            if n in SHARDED:
                outs.append(jnp.stack([sharded_list[li][n] for li in range(DEPTH)]))
            else:
                outs.append(jnp.stack([small[(n, li)] for li in range(DEPTH)]))
        outs.append(small[("final_w", None)])
        return outs

    return (loss, grad_x, *assemble(g_sh, sm_g), *assemble(d_sh, sm_d), *assemble(m_sh, sm_m), *assemble(v_sh, sm_v))
```

```python
import functools

import jax
import jax.numpy as jnp
from jax import lax
from jax.experimental import pallas as pl
from jax.experimental.pallas import tpu as pltpu

D_MODEL = 2048
DEPTH = 2
ATT_HEAD_DIM = 64
ATT_Q_HEADS = 16
ATT_KV_HEADS = 4
WINDOW = 128
POOL_WINDOWS = (2, 4, 8, 16)
POOL_WIDTH = D_MODEL // 2
D_INNER = D_MODEL
SSM_HEAD_DIM = 64
SSM_GROUPS = 4
D_STATE = 128
CONV_K = 4
CHUNK = 128
N_BRANCH = 3
D_FF = 5632
EPS = 1e-6

ADAM_LR = 0.001
ADAM_B1 = 0.9
ADAM_B2 = 0.999
ADAM_EPS = 1e-08
ADAM_WD = 0.01
ADAM_STEP = 10

N_DEV = 8
F32 = jnp.float32
BF16 = jnp.bfloat16
MXU_DTYPE = jnp.bfloat16
NEG = -1e30
VMEM_CAP = 60 * 2**20
PACK_W = 1024
COMM_CHUNKS = 5
CONV_HALO = 8
POOL_HALO = 16
MESH_T = pl.DeviceIdType.MESH


class Dims:
    def __init__(self):
        self.D = D_MODEL
        self.AW = ATT_Q_HEADS * ATT_HEAD_DIM
        self.KVW = ATT_KV_HEADS * ATT_HEAD_DIM
        self.GQ = ATT_Q_HEADS // ATT_KV_HEADS
        self.PW = POOL_WIDTH
        self.PG = len(POOL_WINDOWS)
        self.PC = POOL_WIDTH // len(POOL_WINDOWS)
        self.DI = D_INNER
        self.H = D_INNER // SSM_HEAD_DIM
        self.P = SSM_HEAD_DIM
        self.G = SSM_GROUPS
        self.HPG = self.H // SSM_GROUPS
        self.GW = D_INNER // SSM_GROUPS
        self.N = D_STATE
        self.GN = SSM_GROUPS * D_STATE
        self.CC = D_INNER + 2 * SSM_GROUPS * D_STATE
        self.DTP = -(-self.H // 256) * 256
        self.DFF = D_FF
        self.in_widths = (self.AW, self.KVW, self.KVW, self.PW, self.DI, self.CC, self.H, N_BRANCH * self.D)
        self.IN_COLS = sum(self.in_widths)
        self.QKV = self.AW + 2 * self.KVW
        self.seg = (self.QKV, self.PW, self.DI, self.CC, self.DTP, N_BRANCH * self.D)
        self.IN_INT = sum(self.seg)


def _tile(n, cap, mult=128):
    if n <= cap:
        return n
    best = 0
    for t in range(mult, cap + 1, mult):
        if n % t == 0:
            best = t
    assert best, (n, cap, mult)
    return best


def _nbytes(shape, dtype):
    n = 1
    for s in shape:
        n *= s
    return n * jnp.dtype(dtype).itemsize


def _params(est_bytes, sem=None):
    limit = int(min(VMEM_CAP, max(32 * 2**20, est_bytes * 3 // 2 + 8 * 2**20)))
    kw = dict(vmem_limit_bytes=limit)
    if sem is not None:
        kw["dimension_semantics"] = sem
    return pltpu.CompilerParams(**kw)


def _raw_dot(a, b, ca, cb):
    return lax.dot_general(a.astype(MXU_DTYPE), b.astype(MXU_DTYPE), (((ca,), (cb,)), ((), ())),
                           preferred_element_type=F32)


@functools.partial(jax.custom_vjp, nondiff_argnums=(2, 3))
def bdot(a, b, ca, cb):
    return _raw_dot(a, b, ca, cb)


def _bdot_fwd(a, b, ca, cb):
    return _raw_dot(a, b, ca, cb), (a, b)


def _bdot_bwd(ca, cb, res, g):
    a, b = res
    if (ca, cb) == (1, 0):
        da, db = bdot(g, b, 1, 1), bdot(a, g, 0, 0)
    elif (ca, cb) == (1, 1):
        da, db = bdot(g, b, 1, 0), bdot(g, a, 0, 0)
    else:
        assert (ca, cb) == (0, 0)
        da, db = bdot(b, g, 1, 1), bdot(a, g, 1, 0)
    return da.astype(a.dtype), db.astype(b.dtype)


bdot.defvjp(_bdot_fwd, _bdot_bwd)


def hdot(a, b, ca=1, cb=0):
    return lax.dot_general(a, b, (((ca,), (cb,)), ((), ())), precision=lax.Precision.HIGHEST,
                           preferred_element_type=F32)


def _silu(x):
    return x * jax.nn.sigmoid(x)


def _softplus(x):
    return jnp.maximum(x, 0.0) + jnp.log1p(jnp.exp(-jnp.abs(x)))


def matmul(name, a, b, mode, out_dtype, add=None, b_blocked=False, out_blocks=0):
    if b_blocked:
        nb, rows, n = b.shape
        b_rows, b_cols = rows, nb * n
    else:
        b_rows, b_cols = b.shape
    if mode == "nn":
        (M, K), (K2, N) = a.shape, (b_rows, b_cols)
    elif mode == "nt":
        (M, K), (N, K2) = a.shape, (b_rows, b_cols)
    else:
        (K, M), (K2, N) = a.shape, (b_rows, b_cols)
    assert K == K2, (name, a.shape, b.shape, mode)
    n_blk = n if b_blocked else (N // out_blocks if out_blocks else 0)
    tm = _tile(M, 1024)
    tn = _tile(n_blk if (n_blk and mode != "nt") else N, 1536)
    tk = _tile(n_blk if (n_blk and mode == "nt") else K, 2048 if mode != "tn" else 1024)
    nk = K // tk
    dims = {"nn": (1, 0), "nt": (1, 1), "tn": (0, 0)}[mode]
    a_spec = (pl.BlockSpec((tk, tm), lambda i, j, k: (k, i)) if mode == "tn"
              else pl.BlockSpec((tm, tk), lambda i, j, k: (i, k)))
    if b_blocked and mode == "nt":
        per = n // tk
        b_spec = pl.BlockSpec((None, tn, tk), lambda i, j, k: (k // per, j, k % per))
    elif b_blocked:
        per = n // tn
        b_spec = pl.BlockSpec((None, tk, tn), lambda i, j, k: (j // per, k, j % per))
    elif mode == "nt":
        b_spec = pl.BlockSpec((tn, tk), lambda i, j, k: (j, k))
    else:
        b_spec = pl.BlockSpec((tk, tn), lambda i, j, k: (k, j))
    o_spec = pl.BlockSpec((tm, tn), lambda i, j, k: (i, j))
    out_shape = jax.ShapeDtypeStruct((M, N), out_dtype)
    if out_blocks:
        assert mode == "tn" and add is None
        per_o = n_blk // tn
        o_spec = pl.BlockSpec((None, tm, tn), lambda i, j, k: (j // per_o, i, j % per_o))
        out_shape = jax.ShapeDtypeStruct((out_blocks, M, n_blk), out_dtype)
    has_add = add is not None

    def body(*refs):
        a_ref, b_ref = refs[0], refs[1]
        c_ref = refs[2] if has_add else None
        o_ref = refs[3] if has_add else refs[2]
        p = _raw_dot(a_ref[...], b_ref[...], *dims)
        if nk == 1:
            if has_add:
                p = p + c_ref[...].astype(F32)
            o_ref[...] = p.astype(o_ref.dtype)
            return
        acc = refs[-1]
        k = pl.program_id(2)

        @pl.when(k == 0)
        def _():
            acc[...] = p + c_ref[...].astype(F32) if has_add else p

        @pl.when(k > 0)
        def _():
            acc[...] += p

        @pl.when(k == nk - 1)
        def _():
            o_ref[...] = acc[...].astype(o_ref.dtype)

    est = 2 * (_nbytes((tm, tk), a.dtype) + _nbytes((tk, tn), b.dtype) + _nbytes((tm, tn), out_dtype))
    est += 3 * _nbytes((tm, tn), F32) + _nbytes((tm, tk), MXU_DTYPE) + _nbytes((tk, tn), MXU_DTYPE)
    if has_add:
        est += 2 * _nbytes((tm, tn), add.dtype)
    args = (a, b) + ((add,) if has_add else ())
    in_specs = [a_spec, b_spec] + ([o_spec] if has_add else [])
    return pl.pallas_call(
        body, name=name, grid=(M // tm, N // tn, nk),
        in_specs=in_specs, out_specs=o_spec, out_shape=out_shape,
        scratch_shapes=[pltpu.VMEM((tm, tn), F32)] if nk > 1 else [],
        compiler_params=_params(est, ("parallel", "parallel", "arbitrary")),
    )(*args)


def row_call(name, f, rows, pars, outs, tm):
    L = rows[0].shape[0]
    tm = min(tm, L)
    nr, npar = len(rows), len(pars)

    def body(*refs):
        vals = [r[...] for r in refs[:nr + npar]]
        res = f(*vals)
        for o_ref, v in zip(refs[nr + npar:], res):
            o_ref[...] = v.astype(o_ref.dtype)

    est = 2 * sum(_nbytes((tm, a.shape[1]), a.dtype) for a in rows) + 2 * sum(_nbytes((tm, w), d) for w, d in outs)
    est += 4 * sum(_nbytes((tm, a.shape[1]), F32) for a in rows)
    res = pl.pallas_call(
        body, name=name, grid=(L // tm,),
        in_specs=[pl.BlockSpec((tm, a.shape[1]), lambda i: (i, 0)) for a in rows]
        + [pl.BlockSpec(p.shape, lambda i: (0, 0)) for p in pars],
        out_specs=[pl.BlockSpec((tm, w), lambda i: (i, 0)) for w, _ in outs],
        out_shape=[jax.ShapeDtypeStruct((L, w), d) for w, d in outs],
        compiler_params=_params(est, ("parallel",)),
    )(*rows, *pars)
    return res


def row_vjp_call(name, f, rows, pars, cots, row_grad_dtypes, par_grads, tm, adds=None):
    L = rows[0].shape[0]
    tm = min(tm, L)
    nr, npar, nc = len(rows), len(pars), len(cots)
    adds = adds or {}
    add_idx = sorted(adds)
    rg_idx = [i for i, d in enumerate(row_grad_dtypes) if d is not None]
    pg_idx = [i for i, w in enumerate(par_grads) if w]
    diff_idx = rg_idx + [nr + i for i in pg_idx]

    def body(*refs):
        vals = [r[...] for r in refs[:nr + npar]]
        cvals = [r[...] for r in refs[nr + npar:nr + npar + nc]]
        avals = [r[...] for r in refs[nr + npar + nc:nr + npar + nc + len(add_idx)]]
        out_refs = refs[nr + npar + nc + len(add_idx):]

        def g(*dv):
            full = list(vals)
            for i, v in zip(diff_idx, dv):
                full[i] = v
            return tuple(f(*full))

        res, vjp = jax.vjp(g, *[vals[i] for i in diff_idx])
        grads = vjp(tuple(c.astype(r.dtype) for c, r in zip(cvals, res)))
        for n, i in enumerate(rg_idx):
            gval = grads[n].astype(F32)
            if i in adds:
                gval = gval + avals[add_idx.index(i)].astype(F32)
            out_refs[n][...] = gval.astype(out_refs[n].dtype)
        first = pl.program_id(0) == 0
        for n, i in enumerate(pg_idx):
            o_ref = out_refs[len(rg_idx) + n]
            gval = grads[len(rg_idx) + n].astype(F32)

            @pl.when(first)
            def _(o_ref=o_ref, gval=gval):
                o_ref[...] = gval

            @pl.when(jnp.logical_not(first))
            def _(o_ref=o_ref, gval=gval):
                o_ref[...] += gval

    row_spec = lambda a: pl.BlockSpec((tm, a.shape[1]), lambda i: (i, 0))
    est = 2 * sum(_nbytes((tm, a.shape[1]), a.dtype) for a in list(rows) + list(cots))
    est += 10 * sum(_nbytes((tm, a.shape[1]), F32) for a in rows)
    res = pl.pallas_call(
        body, name=name, grid=(L // tm,),
        in_specs=[row_spec(a) for a in rows] + [pl.BlockSpec(p.shape, lambda i: (0, 0)) for p in pars]
        + [row_spec(c) for c in cots] + [row_spec(adds[i]) for i in add_idx],
        out_specs=[row_spec(rows[i]) for i in rg_idx] + [pl.BlockSpec(pars[i].shape, lambda i_: (0, 0)) for i in pg_idx],
        out_shape=[jax.ShapeDtypeStruct(rows[i].shape, row_grad_dtypes[i]) for i in rg_idx]
        + [jax.ShapeDtypeStruct(pars[i].shape, F32) for i in pg_idx],
        compiler_params=_params(est, ("arbitrary",)),
    )(*rows, *pars, *cots, *[adds[i] for i in add_idx])
    return list(res[:len(rg_idx)]), list(res[len(rg_idx):])


def _halo_specs(rows, T, nt, rev):
    specs = []
    for (_, w, ci, hs) in rows:
        if rev:
            specs.append(pl.BlockSpec((T, w), lambda j, ci=ci: (nt - 1 - j, ci)))
        else:
            specs.append(pl.BlockSpec((T, w), lambda i, ci=ci: (i, ci)))
        if hs:
            r = T // hs
            if rev:
                specs.append(pl.BlockSpec((hs, w), lambda j, ci=ci, r=r: (jnp.maximum((nt - 1 - j) * r - 1, 0), ci)))
            else:
                specs.append(pl.BlockSpec((hs, w), lambda i, ci=ci, r=r: (jnp.maximum(i * r - 1, 0), ci)))
    return specs


def _halo_args(rows):
    args = []
    for (a, _, _, hs) in rows:
        args.append(a)
        if hs:
            args.append(a)
    return args


def _halo_vals(rows, refs):
    vals, n = [], 0
    for (_, _, _, hs) in rows:
        if hs:
            vals.append((refs[n + 1][...], refs[n][...]))
            n += 2
        else:
            vals.append(refs[n][...])
            n += 1
    return vals, n


def halo_call(name, f, rows, pars, outs, T):
    L = rows[0][0].shape[0]
    T = min(T, L)
    nt = L // T
    npar = len(pars)

    def body(*refs):
        i = pl.program_id(0)
        vals, n = _halo_vals(rows, refs)
        pv = [r[...] for r in refs[n:n + npar]]
        res = f(i == 0, i * T, *vals, *pv)
        for o_ref, v in zip(refs[n + npar:], res):
            o_ref[...] = v.astype(o_ref.dtype)

    est = 2 * sum(_nbytes((T, w), a.dtype) for a, w, _, _ in rows) + 2 * sum(_nbytes((T, w), d) for w, d in outs)
    est += 8 * sum(_nbytes((T, w), F32) for _, w, _, _ in rows)
    return pl.pallas_call(
        body, name=name, grid=(nt,),
        in_specs=_halo_specs(rows, T, nt, False) + [pl.BlockSpec(p.shape, lambda i, nd=p.ndim: (0,) * nd) for p in pars],
        out_specs=[pl.BlockSpec((T, w), lambda i: (i, 0)) for w, _ in outs],
        out_shape=[jax.ShapeDtypeStruct((L, w), d) for w, d in outs],
        compiler_params=_params(est, ("parallel",)),
    )(*_halo_args(rows), *pars)


def halo_vjp_call(name, f, rows, pars, cots, row_grad_dtypes, par_grads, T):
    L = rows[0][0].shape[0]
    T = min(T, L)
    nt = L // T
    nr, npar, nc = len(rows), len(pars), len(cots)
    pg_idx = [i for i, w in enumerate(par_grads) if w]
    halo_idx = [i for i, r in enumerate(rows) if r[3]]

    def body(*refs):
        j = pl.program_id(0)
        i = nt - 1 - j
        vals, n = _halo_vals(rows, refs)
        pv = [r[...] for r in refs[n:n + npar]]
        cv = [r[...] for r in refs[n + npar:n + npar + nc]]
        out_refs = refs[n + npar + nc:n + npar + nc + nr + len(pg_idx)]
        carries = refs[n + npar + nc + nr + len(pg_idx):]
        first = i == 0

        def g(vals_, pv_):
            return tuple(f(first, i * T, *vals_, *pv_))

        res, vjp = jax.vjp(g, vals, pv)
        dvals, dpv = vjp(tuple(c.astype(r.dtype) for c, r in zip(cv, res)))

        @pl.when(j == 0)
        def _():
            for c_ref in carries:
                c_ref[...] = jnp.zeros_like(c_ref)

        for k in range(nr):
            hs = rows[k][3]
            o_ref = out_refs[k]
            if hs:
                dh, dc = dvals[k]
                c_ref = carries[halo_idx.index(k)]
                dc = dc.astype(F32)
                if hs == T:
                    o_ref[...] = (dc + c_ref[...]).astype(o_ref.dtype)
                else:
                    o_ref[0:T - hs, :] = dc[0:T - hs].astype(o_ref.dtype)
                    o_ref[T - hs:T, :] = (dc[T - hs:T] + c_ref[...]).astype(o_ref.dtype)
                c_ref[...] = dh.astype(F32)
            else:
                o_ref[...] = dvals[k].astype(o_ref.dtype)
        for m, k in enumerate(pg_idx):
            o_ref = out_refs[nr + m]
            gval = dpv[k].astype(F32)

            @pl.when(j == 0)
            def _(o_ref=o_ref, gval=gval):
                o_ref[...] = gval

            @pl.when(j > 0)
            def _(o_ref=o_ref, gval=gval):
                o_ref[...] += gval

    est = 2 * sum(_nbytes((T, w), a.dtype) for a, w, _, _ in rows) + 2 * sum(_nbytes((T, c.shape[1]), c.dtype) for c in cots)
    est += 12 * sum(_nbytes((T, w), F32) for _, w, _, _ in rows)
    res = pl.pallas_call(
        body, name=name, grid=(nt,),
        in_specs=_halo_specs(rows, T, nt, True) + [pl.BlockSpec(p.shape, lambda j, nd=p.ndim: (0,) * nd) for p in pars]
        + [pl.BlockSpec((T, c.shape[1]), lambda j: (nt - 1 - j, 0)) for c in cots],
        out_specs=[pl.BlockSpec((T, w), lambda j: (nt - 1 - j, 0)) for _, w, _, _ in rows]
        + [pl.BlockSpec(pars[k].shape, lambda j, nd=pars[k].ndim: (0,) * nd) for k in pg_idx],
        out_shape=[jax.ShapeDtypeStruct((L, w), row_grad_dtypes[k]) for k, (_, w, _, _) in enumerate(rows)]
        + [jax.ShapeDtypeStruct(pars[k].shape, F32) for k in pg_idx],
        scratch_shapes=[pltpu.VMEM((rows[k][3], rows[k][1]), F32) for k in halo_idx],
        compiler_params=_params(est, ("arbitrary",)),
    )(*_halo_args(rows), *pars, *cots)
    return list(res[:nr]), list(res[nr:])


def f_rms(x, w):
    x = x.astype(F32)
    return (x * lax.rsqrt(jnp.mean(x * x, axis=-1, keepdims=True) + EPS) * w,)


def f_swiglu(gu):
    dff = gu.shape[1] // 2
    return (_silu(gu[:, :dff]) * gu[:, dff:],)


def f_merge(gl, a, p, s):
    d = a.shape[1]
    g = jax.nn.sigmoid(gl.astype(F32))
    return (g[:, :d] * a.astype(F32) + g[:, d:2 * d] * p.astype(F32) + g[:, 2 * d:] * s.astype(F32),)


def f_gnorm(y, z, nw):
    dm = Dims()
    g = y * _silu(z.astype(F32))
    outs = []
    for gi in range(dm.G):
        gg = g[:, gi * dm.GW:(gi + 1) * dm.GW]
        outs.append(gg * lax.rsqrt(jnp.mean(gg * gg, axis=-1, keepdims=True) + EPS))
    return (jnp.concatenate(outs, axis=1) * nw,)


def f_conv(first, row0, xs, w, b):
    halo, cur = xs
    halo = jnp.where(first, 0.0, halo)
    ext = jnp.concatenate([halo, cur], axis=0)
    T = cur.shape[0]
    base = CONV_HALO - (CONV_K - 1)
    pre = b
    for k in range(CONV_K):
        pre = pre + w[k:k + 1, :] * ext[base + k:base + k + T]
    return (_silu(pre),)


def f_pool(first, row0, us, pw, scale):
    dm = Dims()
    halo, cur = us
    halo = jnp.where(first, 0.0, halo)
    ext = jnp.concatenate([halo, cur], axis=0)
    T = cur.shape[0]
    t = row0 + lax.broadcasted_iota(jnp.int32, (T, 1), 0)
    outs = []
    for gi, w in enumerate(POOL_WINDOWS):
        assert w & (w - 1) == 0 and w <= POOL_HALO
        s = ext[:, gi * dm.PC:(gi + 1) * dm.PC]
        sh = 1
        while sh < w:
            s = s + jnp.concatenate([jnp.zeros((sh, dm.PC), F32), s[:-sh]], axis=0)
            sh *= 2
        cnt = jnp.minimum(t + 1, w).astype(F32)
        mixed = s[POOL_HALO:] / cnt - cur[:, gi * dm.PC:(gi + 1) * dm.PC]
        outs.append(bdot(mixed, pw[gi], 1, 0))
    return (jnp.concatenate(outs, axis=1) * scale,)


def f_attn(first, row0, q, ks, vs, sink):
    dm = Dims()
    kp, kc = ks
    vp, vc = vs
    T = q.shape[0]
    hd = ATT_HEAD_DIM
    qi = lax.broadcasted_iota(jnp.int32, (T, 2 * T), 0)
    si = lax.broadcasted_iota(jnp.int32, (T, 2 * T), 1)
    diff = qi + T - si
    ok = (diff >= 0) & (diff < WINDOW) & ((si >= T) | jnp.logical_not(first))
    mask = jnp.concatenate([ok] * dm.GQ, axis=0)
    outs = []
    for k in range(ATT_KV_HEADS):
        kk = jnp.concatenate([kp[:, k * hd:(k + 1) * hd], kc[:, k * hd:(k + 1) * hd]], axis=0)
        vv = jnp.concatenate([vp[:, k * hd:(k + 1) * hd], vc[:, k * hd:(k + 1) * hd]], axis=0)
        heads = [k * dm.GQ + g for g in range(dm.GQ)]
        qs = jnp.concatenate([q[:, h * hd:(h + 1) * hd] for h in heads], axis=0)
        s = bdot(qs, kk, 1, 1) * (hd ** -0.5)
        s = jnp.where(mask, s, NEG)
        sk = jnp.concatenate([jnp.broadcast_to(sink[:, h:h + 1], (T, 1)) for h in heads], axis=0)
        m = jnp.maximum(jnp.max(s, axis=-1, keepdims=True), sk)
        p = jnp.exp(s - m)
        den = jnp.sum(p, axis=-1, keepdims=True) + jnp.exp(sk - m)
        o = bdot(p / den, vv, 1, 0)
        outs += [o[g * T:(g + 1) * T] for g in range(dm.GQ)]
    return (jnp.concatenate(outs, axis=1),)


def f_ssd(xc, dtr, ht, dt_bias, a_log, d_skip):
    dm = Dims()
    Q = xc.shape[0]
    xs = xc[:, :dm.DI]
    bm = xc[:, dm.DI:dm.DI + dm.GN]
    cm = xc[:, dm.DI + dm.GN:]
    expand = (lax.broadcasted_iota(jnp.int32, (dm.H, dm.DI), 1) // dm.P
              == lax.broadcasted_iota(jnp.int32, (dm.H, dm.DI), 0)).astype(F32)
    ri = lax.broadcasted_iota(jnp.int32, (Q, Q), 0)
    ci = lax.broadcasted_iota(jnp.int32, (Q, Q), 1)
    causal = ri >= ci
    tinc = causal.astype(F32)
    dt = _softplus(dtr[:, :dm.H] + dt_bias)
    da = dt * (-jnp.exp(a_log))
    acs = hdot(tinc, da)
    acs_t = hdot(da, tinc, 0, 1)
    eacs = jnp.exp(acs)
    dend = jnp.exp(acs[Q - 1:Q, :] - acs)
    ex = hdot(jnp.concatenate([dt, eacs, dend, jnp.broadcast_to(d_skip, (8, dm.H))], axis=0), expand)
    dt_x, eacs_x, dend_x, dsk_x = ex[:Q], ex[Q:2 * Q], ex[2 * Q:3 * Q], ex[3 * Q:3 * Q + 1]
    xdt = xs * dt_x
    ys, hts = [], []
    for g in range(dm.G):
        gs = slice(g * dm.GW, (g + 1) * dm.GW)
        bg = bm[:, g * dm.N:(g + 1) * dm.N]
        cg = cm[:, g * dm.N:(g + 1) * dm.N]
        cb = bdot(cg, bg, 1, 1)
        y_off = bdot(cg, ht[:, gs], 1, 0) * eacs_x[:, gs]
        xg = xdt[:, gs]
        st = bdot(bg, xg * dend_x[:, gs], 0, 0)
        hts.append(ht[:, gs] * eacs_x[Q - 1:Q, gs] + st)
        yd = []
        for e in range(dm.HPG):
            h = g * dm.HPG + e
            seg = acs[:, h:h + 1] - acs_t[h:h + 1, :]
            lm = jnp.exp(jnp.where(causal, seg, NEG))
            yd.append(bdot(cb * lm, xg[:, e * dm.P:(e + 1) * dm.P], 1, 0))
        ys.append(jnp.concatenate(yd, axis=1) + y_off)
    y = jnp.concatenate(ys, axis=1) + dsk_x * xs
    return y, jnp.concatenate(hts, axis=1)


def ssd_fwd(name, xc, dtr, dt_bias, a_log, d_skip):
    dm = Dims()
    L = xc.shape[0]
    Q = CHUNK
    nc = L // Q

    def body(xc_ref, dtr_ref, b_ref, a_ref, s_ref, y_ref, hts_ref, ht):
        @pl.when(pl.program_id(0) == 0)
        def _():
            ht[...] = jnp.zeros_like(ht)

        h0 = ht[...]
        hts_ref[0] = h0
        y, h1 = f_ssd(xc_ref[...], dtr_ref[...], h0, b_ref[...], a_ref[...], s_ref[...])
        y_ref[...] = y
        ht[...] = h1

    par = pl.BlockSpec((1, dm.H), lambda c: (0, 0))
    est = 40 * _nbytes((Q, dm.CC), F32) + 4 * _nbytes((dm.N, dm.DI), F32)
    return pl.pallas_call(
        body, name=name, grid=(nc,),
        in_specs=[pl.BlockSpec((Q, dm.CC), lambda c: (c, 0)), pl.BlockSpec((Q, dm.DTP), lambda c: (c, 0)), par, par, par],
        out_specs=[pl.BlockSpec((Q, dm.DI), lambda c: (c, 0)), pl.BlockSpec((1, dm.N, dm.DI), lambda c: (c, 0, 0))],
        out_shape=[jax.ShapeDtypeStruct((L, dm.DI), F32), jax.ShapeDtypeStruct((nc, dm.N, dm.DI), F32)],
        scratch_shapes=[pltpu.VMEM((dm.N, dm.DI), F32)],
        compiler_params=_params(est, ("arbitrary",)),
    )(xc, dtr, dt_bias, a_log, d_skip)


def ssd_bwd(name, xc, dtr, hts, dt_bias, a_log, d_skip, dy):
    dm = Dims()
    L = xc.shape[0]
    Q = CHUNK
    nc = L // Q

    def body(xc_ref, dtr_ref, hts_ref, b_ref, a_ref, s_ref, dy_ref, dxc_ref, ddtr_ref, db_ref, da_ref, ds_ref, dht):
        j = pl.program_id(0)

        @pl.when(j == 0)
        def _():
            dht[...] = jnp.zeros_like(dht)

        _, vjp = jax.vjp(f_ssd, xc_ref[...], dtr_ref[...], hts_ref[0], b_ref[...], a_ref[...], s_ref[...])
        dxc, ddtr, dh0, db, da, ds = vjp((dy_ref[...], dht[...]))
        dxc_ref[...] = dxc.astype(dxc_ref.dtype)
        ddtr_ref[...] = ddtr.astype(ddtr_ref.dtype)
        dht[...] = dh0
        for o_ref, gval in ((db_ref, db), (da_ref, da), (ds_ref, ds)):
            @pl.when(j == 0)
            def _(o_ref=o_ref, gval=gval):
                o_ref[...] = gval

            @pl.when(j > 0)
            def _(o_ref=o_ref, gval=gval):
                o_ref[...] += gval

    par = pl.BlockSpec((1, dm.H), lambda j: (0, 0))
    rev = lambda w: pl.BlockSpec((Q, w), lambda j: (nc - 1 - j, 0))
    est = 80 * _nbytes((Q, dm.CC), F32) + 6 * _nbytes((dm.N, dm.DI), F32)
    return pl.pallas_call(
        body, name=name, grid=(nc,),
        in_specs=[rev(dm.CC), rev(dm.DTP), pl.BlockSpec((1, dm.N, dm.DI), lambda j: (nc - 1 - j, 0, 0)), par, par, par, rev(dm.DI)],
        out_specs=[rev(dm.CC), rev(dm.DTP), par, par, par],
        out_shape=[jax.ShapeDtypeStruct((L, dm.CC), F32), jax.ShapeDtypeStruct((L, dm.DTP), BF16)]
        + [jax.ShapeDtypeStruct((1, dm.H), F32)] * 3,
        scratch_shapes=[pltpu.VMEM((dm.N, dm.DI), F32)],
        compiler_params=_params(est, ("arbitrary",)),
    )(xc, dtr, hts, dt_bias, a_log, d_skip, dy)


def loss_head(x, w, target, tm):
    L, D = x.shape
    tm = min(tm, L)

    def tile_loss(xv, wv, tv):
        (y,) = f_rms(xv, wv)
        return 0.5 * jnp.sum(jnp.mean(jnp.square(y - tv), axis=-1))

    def body(x_ref, w_ref, t_ref, dx_ref, dw_ref, loss_ref):
        val, (dx, dw) = jax.value_and_grad(tile_loss, argnums=(0, 1))(x_ref[...], w_ref[...], t_ref[...])
        dx_ref[...] = dx
        first = pl.program_id(0) == 0
        lv = jnp.full((8, 128), val, F32)

        @pl.when(first)
        def _():
            dw_ref[...] = dw
            loss_ref[...] = lv

        @pl.when(jnp.logical_not(first))
        def _():
            dw_ref[...] += dw
            loss_ref[...] += lv

    row = pl.BlockSpec((tm, D), lambda i: (i, 0))
    est = 16 * _nbytes((tm, D), F32)
    return pl.pallas_call(
        body, name="loss_head", grid=(L // tm,),
        in_specs=[row, pl.BlockSpec((1, D), lambda i: (0, 0)), row],
        out_specs=[row, pl.BlockSpec((1, D), lambda i: (0, 0)), pl.BlockSpec((8, 128), lambda i: (0, 0))],
        out_shape=[jax.ShapeDtypeStruct((L, D), F32), jax.ShapeDtypeStruct((1, D), F32), jax.ShapeDtypeStruct((8, 128), F32)],
        compiler_params=_params(est, ("arbitrary",)),
    )(x, w, target)


def _adamw(w, g, m, v):
    m = ADAM_B1 * m + (1.0 - ADAM_B1) * g
    v = ADAM_B2 * v + (1.0 - ADAM_B2) * jnp.square(g)
    m_hat = m / (1.0 - ADAM_B1 ** ADAM_STEP)
    v_hat = v / (1.0 - ADAM_B2 ** ADAM_STEP)
    delta = -ADAM_LR * (m_hat / (jnp.sqrt(v_hat) + ADAM_EPS) + ADAM_WD * w)
    return delta, m, v


def adamw_sharded(name, parts, w, m, v, li, carried):
    depth, R, C = w.shape
    tr = _tile(R, 128, 16)
    n_in = 4 + (4 if carried else 0)

    def body(*refs):
        p_ref, w_ref, m_ref, v_ref = refs[:4]
        g_ref, d_ref, nm_ref, nv_ref = refs[n_in:n_in + 4]
        g = p_ref[0].astype(F32)
        for s in range(1, N_DEV):
            g = g + p_ref[s].astype(F32)
        d, nm, nv = _adamw(w_ref[...], g, m_ref[...], v_ref[...])
        g_ref[...] = g
        d_ref[...] = d
        nm_ref[...] = nm
        nv_ref[...] = nv

    row = pl.BlockSpec((None, tr, C), lambda i: (li, i, 0))
    est = 2 * _nbytes((N_DEV, tr, C), parts.dtype) + 20 * _nbytes((tr, C), F32)
    return pl.pallas_call(
        body, name=name, grid=(R // tr,),
        in_specs=[pl.BlockSpec((N_DEV, tr, C), lambda i: (0, i, 0)), row, row, row]
        + ([pl.BlockSpec(memory_space=pl.ANY)] * 4 if carried else []),
        out_specs=[row] * 4,
        out_shape=[jax.ShapeDtypeStruct((depth, R, C), F32)] * 4,
        input_output_aliases={4 + k: k for k in range(4)} if carried else {},
        compiler_params=_params(est, ("parallel",)),
    )(parts, w, m, v, *(carried or ()))


def _my_place():
    return lax.axis_index("x"), lax.axis_index("y"), lax.axis_index("c")


def all_gather_hbm(name, shard):
    R, W = shard.shape

    def body(x_ref, out_ref, send_sems, recv_sems, local_sem):
        x, y, c = _my_place()
        me, sibling = (x, y, c), (x, y, 1 - c)
        chips = [(1 - x, y), (x, 1 - y), (1 - x, 1 - y)]

        def slot(px, py, pc):
            return out_ref.at[4 * px + 2 * py + pc]

        def copy(k, block, to, src=None):
            return pltpu.make_async_remote_copy(
                src_ref=slot(*block) if src is None else src, dst_ref=slot(*block),
                send_sem=send_sems.at[k], recv_sem=recv_sems.at[k], device_id=to, device_id_type=MESH_T)

        mine = pltpu.make_async_copy(x_ref, slot(*me), local_sem)
        mine.start()
        first = [copy(0, me, sibling, src=x_ref)]
        first += [copy(1 + j, me, (*chip, c), src=x_ref) for j, chip in enumerate(chips)]
        for cp in first:
            cp.start()
        passed = [copy(4 + j, (*chip, c), sibling) for j, chip in enumerate(chips)]
        for j, chip in enumerate(chips):
            copy(1 + j, (*chip, c), me).wait_recv()
            passed[j].start()
        copy(0, sibling, me).wait_recv()
        for j, chip in enumerate(chips):
            copy(4 + j, (*chip, 1 - c), me).wait_recv()
        for cp in first + passed:
            cp.wait_send()
        mine.wait()

    return pl.pallas_call(
        body, name=name,
        out_shape=jax.ShapeDtypeStruct((N_DEV, R, W), shard.dtype),
        in_specs=[pl.BlockSpec(memory_space=pl.ANY)],
        out_specs=pl.BlockSpec(memory_space=pl.ANY),
        scratch_shapes=[pltpu.SemaphoreType.DMA((7,)), pltpu.SemaphoreType.DMA((7,)), pltpu.SemaphoreType.DMA],
    )(shard)


def exchange_hbm(name, blocks):
    n, R, W = blocks.shape
    assert n == N_DEV

    def body(g_ref, out_ref, send_sems, recv_sems, local_sem):
        x, y, c = _my_place()
        me = 4 * x + 2 * y + c
        mine = pltpu.make_async_copy(g_ref.at[me], out_ref.at[me], local_sem)
        mine.start()
        copies, arrivals = [], []
        for k in range(1, N_DEV):
            px, py, pc = (x + (k >> 2)) % 2, (y + ((k >> 1) & 1)) % 2, (c + (k & 1)) % 2
            p = 4 * px + 2 * py + pc
            copies.append(pltpu.make_async_remote_copy(
                src_ref=g_ref.at[p], dst_ref=out_ref.at[me], send_sem=send_sems.at[k - 1],
                recv_sem=recv_sems.at[k - 1], device_id=(px, py, pc), device_id_type=MESH_T))
            arrivals.append(pltpu.make_async_remote_copy(
                src_ref=g_ref.at[p], dst_ref=out_ref.at[p], send_sem=send_sems.at[k - 1],
                recv_sem=recv_sems.at[k - 1], device_id=(px, py, pc), device_id_type=MESH_T))
        for cp in copies:
            cp.start()
        for cp in arrivals:
            cp.wait_recv()
        for cp in copies:
            cp.wait_send()
        mine.wait()

    return pl.pallas_call(
        body, name=name,
        out_shape=jax.ShapeDtypeStruct((N_DEV, R, W), blocks.dtype),
        in_specs=[pl.BlockSpec(memory_space=pl.ANY)],
        out_specs=pl.BlockSpec(memory_space=pl.ANY),
        scratch_shapes=[pltpu.SemaphoreType.DMA((7,)), pltpu.SemaphoreType.DMA((7,)), pltpu.SemaphoreType.DMA],
    )(blocks)


def small_allreduce_adamw(part, w, m, v):
    R, W = part.shape

    def body(x_ref, w_ref, m_ref, v_ref, g_ref, d_ref, nm_ref, nv_ref, all_ref, send_sems, recv_sems, local_sem):
        x, y, c = _my_place()
        me, sibling = (x, y, c), (x, y, 1 - c)
        chips = [(1 - x, y), (x, 1 - y), (1 - x, 1 - y)]

        def slot(px, py, pc):
            return all_ref.at[4 * px + 2 * py + pc]

        def copy(k, block, to, src=None):
            return pltpu.make_async_remote_copy(
                src_ref=slot(*block) if src is None else src, dst_ref=slot(*block),
                send_sem=send_sems.at[k], recv_sem=recv_sems.at[k], device_id=to, device_id_type=MESH_T)

        mine = pltpu.make_async_copy(x_ref, slot(*me), local_sem)
        mine.start()
        first = [copy(0, me, sibling, src=x_ref)]
        first += [copy(1 + j, me, (*chip, c), src=x_ref) for j, chip in enumerate(chips)]
        for cp in first:
            cp.start()
        passed = [copy(4 + j, (*chip, c), sibling) for j, chip in enumerate(chips)]
        for j, chip in enumerate(chips):
            copy(1 + j, (*chip, c), me).wait_recv()
            passed[j].start()
        copy(0, sibling, me).wait_recv()
        for j, chip in enumerate(chips):
            copy(4 + j, (*chip, 1 - c), me).wait_recv()
        for cp in first + passed:
            cp.wait_send()
        mine.wait()
        g = all_ref[0]
        for s in range(1, N_DEV):
            g = g + all_ref[s]
        d, nm, nv = _adamw(w_ref[...], g, m_ref[...], v_ref[...])
        g_ref[...] = g
        d_ref[...] = d
        nm_ref[...] = nm
        nv_ref[...] = nv

    vm = pl.BlockSpec(memory_space=pltpu.VMEM)
    return pl.pallas_call(
        body, name="small_allreduce_adamw",
        out_shape=[jax.ShapeDtypeStruct((R, W), F32)] * 4,
        in_specs=[vm] * 4, out_specs=[vm] * 4,
        scratch_shapes=[pltpu.VMEM((N_DEV, R, W), F32), pltpu.SemaphoreType.DMA((7,)), pltpu.SemaphoreType.DMA((7,)),
                        pltpu.SemaphoreType.DMA],
    )(part, w, m, v)


SHARDED = ("w_in", "conv_w", "pool_w", "w_attn_br", "w_pool_br", "w_ssm_br", "w_out", "w_gate_up", "w_down")
REPLICATED = ("ln1_w", "attn_sink", "conv_b", "dt_bias", "a_log", "d_skip", "ssm_norm_w", "pool_scale", "ln2_w")
CONV_ROWS = 8


def _gather_weights(dm, W, li):
    tag = f"ag_l{li}_"
    fw = {}
    g_in = all_gather_hbm(tag + "w_in", W["w_in"][li].astype(BF16))
    win = jnp.concatenate([g_in[d] for d in range(N_DEV)], axis=1)
    pts, acc = [], 0
    for wd in dm.in_widths:
        pts.append((acc, acc + wd))
        acc += wd
    cols = lambda k: win[:, pts[k][0]:pts[k][1]]
    fw["w_qkv"] = win[:, :pts[2][1]]
    fw["w_u"], fw["w_z"], fw["w_xbc"], fw["w_gl"] = cols(3), cols(4), cols(5), cols(7)
    fw["w_dt"] = jnp.pad(cols(6), ((0, 0), (0, dm.DTP - dm.H)))
    fw["w_in_int"] = jnp.concatenate([win[:, :pts[5][1]], fw["w_dt"], fw["w_gl"]], axis=1)
    for n in ("w_attn_br", "w_pool_br", "w_gate_up"):
        fw[n] = all_gather_hbm(tag + n, W[n][li].astype(BF16))
    for n in ("w_ssm_br", "w_out", "w_down"):
        g = all_gather_hbm(tag + n, W[n][li].astype(BF16))
        fw[n] = g.reshape(g.shape[0] * g.shape[1], g.shape[2])
    g = all_gather_hbm(tag + "pool_w", W["pool_w"][li].astype(BF16).reshape(dm.PG * dm.PC // N_DEV, dm.PC))
    g = g.reshape(N_DEV, dm.PG, dm.PC // N_DEV, dm.PC).transpose(1, 0, 2, 3)
    fw["pool_w"] = g.reshape(dm.PG, dm.PC, dm.PC).astype(F32)
    g = all_gather_hbm(tag + "conv_w", jnp.pad(W["conv_w"][li], ((0, CONV_ROWS - CONV_K), (0, 0))))
    fw["conv_w"] = g[:, :CONV_K].transpose(1, 0, 2).reshape(CONV_K, dm.CC)
    return fw


def _shard_grads(dm, gw):
    o = [0]
    for wd in dm.seg:
        o.append(o[-1] + wd)
    gi = gw["w_in_int"]
    ref_cols = jnp.concatenate([gi[:, :o[4]], gi[:, o[4]:o[4] + dm.H], gi[:, o[5]:]], axis=1)
    per = dm.IN_COLS // N_DEV
    out = {"w_in": jnp.stack([ref_cols[:, d * per:(d + 1) * per] for d in range(N_DEV)])}
    for n in ("w_attn_br", "w_pool_br", "w_gate_up"):
        out[n] = gw[n]
    for n in ("w_ssm_br", "w_out", "w_down"):
        g = gw[n]
        out[n] = g.reshape(N_DEV, g.shape[0] // N_DEV, g.shape[1])
    g = gw["pool_w"].reshape(dm.PG, N_DEV, dm.PC // N_DEV, dm.PC).transpose(1, 0, 2, 3)
    out["pool_w"] = g.reshape(N_DEV, dm.PG * dm.PC // N_DEV, dm.PC).astype(BF16)
    g = gw["conv_w"].reshape(CONV_K, N_DEV, dm.CC // N_DEV).transpose(1, 0, 2)
    out["conv_w"] = jnp.pad(g, ((0, 0), (0, CONV_ROWS - CONV_K), (0, 0)))
    return out


def _as_rows(n, a):
    if n == "pool_w":
        return a.reshape(a.shape[0], a.shape[1] * a.shape[2], a.shape[3])
    if n == "conv_w":
        return jnp.pad(a, ((0, 0), (0, CONV_ROWS - CONV_K), (0, 0)))
    return a


def _from_rows(n, a, like):
    if n == "pool_w":
        return a.reshape(like.shape)
    if n == "conv_w":
        return a[:, :CONV_K]
    return a


def _size(shape):
    n = 1
    for s in shape:
        n *= s
    return n


def _layer_forward(dm, x, wts, rep, li):
    tag = f"l{li}_"
    sv = {"x": x}
    (h,) = row_call(tag + "rms1", f_rms, [x], [rep["ln1_w"]], [(dm.D, BF16)], 256)
    sv["h"] = h
    qkv = matmul(tag + "p_qkv", h, wts["w_qkv"], "nn", BF16)
    u = matmul(tag + "p_u", h, wts["w_u"], "nn", F32)
    z = matmul(tag + "p_z", h, wts["w_z"], "nn", F32)
    xbc = matmul(tag + "p_xbc", h, wts["w_xbc"], "nn", F32)
    dtr = matmul(tag + "p_dt", h, wts["w_dt"], "nn", F32)
    gl = matmul(tag + "p_gl", h, wts["w_gl"], "nn", F32)
    sv.update(qkv=qkv, u=u, z=z, xbc=xbc, dtr=dtr, gl=gl)
    kvi = dm.AW // dm.KVW
    (att,) = halo_call(tag + "attn", f_attn,
                       [(qkv, dm.AW, 0, 0), (qkv, dm.KVW, kvi, WINDOW), (qkv, dm.KVW, kvi + 1, WINDOW)],
                       [rep["attn_sink"]], [(dm.AW, BF16)], WINDOW)
    (pool,) = halo_call(tag + "pool", f_pool, [(u, dm.PW, 0, POOL_HALO)], [wts["pool_w"], rep["pool_scale"]],
                        [(dm.PW, BF16)], 256)
    (xc,) = halo_call(tag + "conv", f_conv, [(xbc, dm.CC, 0, CONV_HALO)], [wts["conv_w"], rep["conv_b"]],
                      [(dm.CC, F32)], 256)
    y, hts = ssd_fwd(tag + "ssd", xc, dtr, rep["dt_bias"], rep["a_log"], rep["d_skip"])
    (ssm,) = row_call(tag + "gnorm", f_gnorm, [y, z], [rep["ssm_norm_w"]], [(dm.DI, BF16)], 256)
    sv.update(att=att, pool=pool, xc=xc, y=y, hts=hts, ssm=ssm)
    ba = matmul(tag + "br_a", att, wts["w_attn_br"], "nn", F32, b_blocked=True)
    bp = matmul(tag + "br_p", pool, wts["w_pool_br"], "nn", F32, b_blocked=True)
    bs = matmul(tag + "br_s", ssm, wts["w_ssm_br"], "nn", F32)
    (merged,) = row_call(tag + "merge", f_merge, [gl, ba, bp, bs], [], [(dm.D, BF16)], 256)
    x1 = matmul(tag + "out", merged, wts["w_out"], "nn", F32, add=x)
    (h2,) = row_call(tag + "rms2", f_rms, [x1], [rep["ln2_w"]], [(dm.D, BF16)], 256)
    gu = matmul(tag + "gu", h2, wts["w_gate_up"], "nn", F32, b_blocked=True)
    (act,) = row_call(tag + "swiglu", f_swiglu, [gu], [], [(dm.DFF, BF16)], 256)
    x2 = matmul(tag + "down", act, wts["w_down"], "nn", F32, add=x1)
    sv.update(ba=ba, bp=bp, bs=bs, merged=merged, x1=x1, h2=h2, gu=gu, act=act)
    return x2, sv


def _layer_backward(dm, dx2, sv, wts, rep, li):
    tag = f"l{li}_b_"
    gw, gr = {}, {}
    dact = matmul(tag + "d_act", dx2, wts["w_down"], "nt", F32)
    gw["w_down"] = matmul(tag + "g_down", sv["act"], dx2, "tn", BF16)
    (dgu,), _ = row_vjp_call(tag + "swiglu", f_swiglu, [sv["gu"]], [], [dact], [BF16], [], 256)
    dh2 = matmul(tag + "d_h2", dgu, wts["w_gate_up"], "nt", F32, b_blocked=True)
    gw["w_gate_up"] = matmul(tag + "g_gu", sv["h2"], dgu, "tn", BF16, out_blocks=N_DEV)
    (dx1,), (gr["ln2_w"],) = row_vjp_call(tag + "rms2", f_rms, [sv["x1"]], [rep["ln2_w"]], [dh2], [F32], [True], 256,
                                          adds={0: dx2})
    dmerged = matmul(tag + "d_merged", dx1, wts["w_out"], "nt", F32)
    gw["w_out"] = matmul(tag + "g_out", sv["merged"], dx1, "tn", BF16)
    (dgl, dba, dbp, dbs), _ = row_vjp_call(tag + "merge", f_merge, [sv["gl"], sv["ba"], sv["bp"], sv["bs"]], [],
                                           [dmerged], [BF16, BF16, BF16, BF16], [], 256)
    datt = matmul(tag + "d_att", dba, wts["w_attn_br"], "nt", F32, b_blocked=True)
    gw["w_attn_br"] = matmul(tag + "g_br_a", sv["att"], dba, "tn", BF16, out_blocks=N_DEV)
    dpool = matmul(tag + "d_pool", dbp, wts["w_pool_br"], "nt", F32, b_blocked=True)
    gw["w_pool_br"] = matmul(tag + "g_br_p", sv["pool"], dbp, "tn", BF16, out_blocks=N_DEV)
    dssm = matmul(tag + "d_ssm", dbs, wts["w_ssm_br"], "nt", F32)
    gw["w_ssm_br"] = matmul(tag + "g_br_s", sv["ssm"], dbs, "tn", BF16)
    (dy, dz), (gr["ssm_norm_w"],) = row_vjp_call(tag + "gnorm", f_gnorm, [sv["y"], sv["z"]], [rep["ssm_norm_w"]],
                                                 [dssm], [F32, BF16], [True], 256)
    dxc, ddtr, gr["dt_bias"], gr["a_log"], gr["d_skip"] = ssd_bwd(
        tag + "ssd", sv["xc"], sv["dtr"], sv["hts"], rep["dt_bias"], rep["a_log"], rep["d_skip"], dy)
    (dxbc,), (gw["conv_w"], gr["conv_b"]) = halo_vjp_call(
        tag + "conv", f_conv, [(sv["xbc"], dm.CC, 0, CONV_HALO)], [wts["conv_w"], rep["conv_b"]], [dxc],
        [BF16], [True, True], 256)
    (du,), (gw["pool_w"], gr["pool_scale"]) = halo_vjp_call(
        tag + "pool", f_pool, [(sv["u"], dm.PW, 0, POOL_HALO)], [wts["pool_w"], rep["pool_scale"]], [dpool],
        [BF16], [True, True], 256)
    kvi = dm.AW // dm.KVW
    qkv = sv["qkv"]
    (dq, dk, dv), (gr["attn_sink"],) = halo_vjp_call(
        tag + "attn", f_attn, [(qkv, dm.AW, 0, 0), (qkv, dm.KVW, kvi, WINDOW), (qkv, dm.KVW, kvi + 1, WINDOW)],
        [rep["attn_sink"]], [datt], [BF16, BF16, BF16], [True], WINDOW)
    dproj = jnp.concatenate([dq, dk, dv, du, dz, dxbc, ddtr, dgl], axis=1)
    dh = matmul(tag + "d_h", dproj, wts["w_in_int"], "nt", F32)
    gw["w_in_int"] = matmul(tag + "g_in", sv["h"], dproj, "tn", BF16)
    (dx,), (gr["ln1_w"],) = row_vjp_call(tag + "rms1", f_rms, [sv["x"]], [rep["ln1_w"]], [dh], [F32], [True], 256,
                                         adds={0: dx1})
    return dx, gw, gr


def kernel(x, ln1_w, w_in, attn_sink, conv_w, conv_b, dt_bias, a_log, d_skip, ssm_norm_w, pool_w, pool_scale, w_attn_br, w_pool_br, w_ssm_br, w_out, ln2_w, w_gate_up, w_down, final_w, loss_target, m_ln1_w, m_w_in, m_attn_sink, m_conv_w, m_conv_b, m_dt_bias, m_a_log, m_d_skip, m_ssm_norm_w, m_pool_w, m_pool_scale, m_w_attn_br, m_w_pool_br, m_w_ssm_br, m_w_out, m_ln2_w, m_w_gate_up, m_w_down, m_final_w, v_ln1_w, v_w_in, v_attn_sink, v_conv_w, v_conv_b, v_dt_bias, v_a_log, v_d_skip, v_ssm_norm_w, v_pool_w, v_pool_scale, v_w_attn_br, v_w_pool_br, v_w_ssm_br, v_w_out, v_ln2_w, v_w_gate_up, v_w_down, v_final_w):
    dm = Dims()
    W = dict(ln1_w=ln1_w, w_in=w_in, attn_sink=attn_sink, conv_w=conv_w, conv_b=conv_b, dt_bias=dt_bias, a_log=a_log,
             d_skip=d_skip, ssm_norm_w=ssm_norm_w, pool_w=pool_w, pool_scale=pool_scale, w_attn_br=w_attn_br,
             w_pool_br=w_pool_br, w_ssm_br=w_ssm_br, w_out=w_out, ln2_w=ln2_w, w_gate_up=w_gate_up, w_down=w_down,
             final_w=final_w)
    M = dict(ln1_w=m_ln1_w, w_in=m_w_in, attn_sink=m_attn_sink, conv_w=m_conv_w, conv_b=m_conv_b, dt_bias=m_dt_bias,
             a_log=m_a_log, d_skip=m_d_skip, ssm_norm_w=m_ssm_norm_w, pool_w=m_pool_w, pool_scale=m_pool_scale,
             w_attn_br=m_w_attn_br, w_pool_br=m_w_pool_br, w_ssm_br=m_w_ssm_br, w_out=m_w_out, ln2_w=m_ln2_w,
             w_gate_up=m_w_gate_up, w_down=m_w_down, final_w=m_final_w)
    V = dict(ln1_w=v_ln1_w, w_in=v_w_in, attn_sink=v_attn_sink, conv_w=v_conv_w, conv_b=v_conv_b, dt_bias=v_dt_bias,
             a_log=v_a_log, d_skip=v_d_skip, ssm_norm_w=v_ssm_norm_w, pool_w=v_pool_w, pool_scale=v_pool_scale,
             w_attn_br=v_w_attn_br, w_pool_br=v_w_pool_br, w_ssm_br=v_w_ssm_br, w_out=v_w_out, ln2_w=v_ln2_w,
             w_gate_up=v_w_gate_up, w_down=v_w_down, final_w=v_final_w)
    xl = x[0]
    target = loss_target[0]

    full = [_gather_weights(dm, W, li) for li in range(DEPTH)]

    rep =[{n: W[n][li].reshape(1, -1) for n in REPLICATED} for li in range(DEPTH)]

    saved = []
    xa = xl
    for li in range(DEPTH):
        xa, sv = _layer_forward(dm, xa, full[li], rep[li], li)
        saved.append(sv)
    dxa, g_final, loss_blk = loss_head(xa, W["final_w"].reshape(1, -1), target, 256)

    gws, grs = [None] * DEPTH, [None] * DEPTH
    for li in reversed(range(DEPTH)):
        dxa, gws[li], grs[li] = _layer_backward(dm, dxa, saved[li], full[li], rep[li], li)
    grad_x = dxa[None]

    sharded_out = {n: None for n in SHARDED}
    for li in reversed(range(DEPTH)):
        blocks = _shard_grads(dm, gws[li])
        for n in SHARDED:
            parts = exchange_hbm(f"rs_l{li}_{n}", blocks[n])
            sharded_out[n] = adamw_sharded(f"adamw_l{li}_{n}", parts, _as_rows(n, W[n]), _as_rows(n, M[n]),
                                           _as_rows(n, V[n]), li, sharded_out[n])
    g_sh, d_sh, m_sh, v_sh = [{n: _from_rows(n, sharded_out[n][k], W[n]) for n in SHARDED} for k in range(4)]

    small_names = [(n, li) for li in range(DEPTH) for n in REPLICATED] + [("final_w", None)]
    small_shape = lambda n, li: W[n].shape if li is None else W[n].shape[1:]
    small_rows = [-(-_size(small_shape(n, li)) // (8 * 128)) * 8 for n, li in small_names]
    loss_row = sum(small_rows)

    def small_pack(get, last=None):
        rows = []
        for (n, li), r in zip(small_names, small_rows):
            a = get(n, li).reshape(-1).astype(F32)
            rows.append(jnp.pad(a, (0, r * 128 - a.size)).reshape(r, 128))
        rows.append(jnp.zeros((8, 128), F32) if last is None else last)
        return jnp.concatenate(rows, axis=0)

    part = small_pack(lambda n, li: g_final if li is None else grs[li][n], loss_blk)
    wsm = small_pack(lambda n, li: W[n] if li is None else W[n][li])
    msm = small_pack(lambda n, li: M[n] if li is None else M[n][li])
    vsm = small_pack(lambda n, li: V[n] if li is None else V[n][li])
    sm = small_allreduce_adamw(part, wsm, msm, vsm)
    loss = sm[0][loss_row, 0]

    def small_unpack(buf):
        out, r0 = {}, 0
        for (n, li), r in zip(small_names, small_rows):
            shp = small_shape(n, li)
            out[(n, li)] = buf[r0:r0 + r].reshape(-1)[:_size(shp)].reshape(shp)
            r0 += r
        return out

    sm_g, sm_d, sm_m, sm_v = [small_unpack(b) for b in sm]

    def assemble(sharded_list, small):
        outs = []
        for n in ("ln1_w", "w_in", "attn_sink", "conv_w", "conv_b", "dt_bias", "a_log", "d_skip", "ssm_norm_w", "pool_w",
                  "pool_scale", "w_attn_br", "w_pool_br", "w_ssm_br", "w_out", "ln2_w", "w_gate_up", "w_down"):
            if n in SHARDED:
                outs.append(sharded_list[n])
            else:
                outs.append(jnp.stack([small[(n, li)] for li in range(DEPTH)]))
        outs.append(small[("final_w", None)])
        return outs

    return (loss, grad_x, *assemble(g_sh, sm_g), *assemble(d_sh, sm_d), *assemble(m_sh, sm_m), *assemble(v_sh, sm_v))
```

```python
import functools

import jax
import jax.numpy as jnp
from jax import lax
from jax.experimental import pallas as pl
from jax.experimental.pallas import tpu as pltpu

D_MODEL = 2048
DEPTH = 2
ATT_HEAD_DIM = 64
ATT_Q_HEADS = 16
ATT_KV_HEADS = 4
WINDOW = 128
POOL_WINDOWS = (2, 4, 8, 16)
POOL_WIDTH = D_MODEL // 2
D_INNER = D_MODEL
SSM_HEAD_DIM = 64
SSM_GROUPS = 4
D_STATE = 128
CONV_K = 4
CHUNK = 128
N_BRANCH = 3
D_FF = 5632
EPS = 1e-6

ADAM_LR = 0.001
ADAM_B1 = 0.9
ADAM_B2 = 0.999
ADAM_EPS = 1e-08
ADAM_WD = 0.01
ADAM_STEP = 10

N_DEV = 8
F32 = jnp.float32
BF16 = jnp.bfloat16
MXU_DTYPE = jnp.bfloat16
NEG = -1e30
VMEM_CAP = 60 * 2**20
CONV_HALO = 8
POOL_HALO = 16
MESH_T = pl.DeviceIdType.MESH


class Dims:
    def __init__(self):
        self.D = D_MODEL
        self.AW = ATT_Q_HEADS * ATT_HEAD_DIM
        self.KVW = ATT_KV_HEADS * ATT_HEAD_DIM
        self.GQ = ATT_Q_HEADS // ATT_KV_HEADS
        self.PW = POOL_WIDTH
        self.PG = len(POOL_WINDOWS)
        self.PC = POOL_WIDTH // len(POOL_WINDOWS)
        self.DI = D_INNER
        self.H = D_INNER // SSM_HEAD_DIM
        self.P = SSM_HEAD_DIM
        self.G = SSM_GROUPS
        self.HPG = self.H // SSM_GROUPS
        self.GW = D_INNER // SSM_GROUPS
        self.N = D_STATE
        self.GN = SSM_GROUPS * D_STATE
        self.CC = D_INNER + 2 * SSM_GROUPS * D_STATE
        self.DTP = -(-self.H // 256) * 256
        self.DFF = D_FF
        self.in_widths = (self.AW, self.KVW, self.KVW, self.PW, self.DI, self.CC, self.H, N_BRANCH * self.D)
        self.IN_COLS = sum(self.in_widths)
        self.QKV = self.AW + 2 * self.KVW
        self.seg = (self.QKV, self.PW, self.DI, self.CC, self.DTP, N_BRANCH * self.D)
        self.IN_INT = sum(self.seg)


def _tile(n, cap, mult=128):
    if n <= cap:
        return n
    best = 0
    for t in range(mult, cap + 1, mult):
        if n % t == 0:
            best = t
    assert best, (n, cap, mult)
    return best


def _nbytes(shape, dtype):
    n = 1
    for s in shape:
        n *= s
    return n * jnp.dtype(dtype).itemsize


def _params(est_bytes, sem=None):
    limit = int(min(VMEM_CAP, max(32 * 2**20, est_bytes * 3 // 2 + 8 * 2**20)))
    kw = dict(vmem_limit_bytes=limit)
    if sem is not None:
        kw["dimension_semantics"] = sem
    return pltpu.CompilerParams(**kw)


def _raw_dot(a, b, ca, cb):
    return lax.dot_general(a.astype(MXU_DTYPE), b.astype(MXU_DTYPE), (((ca,), (cb,)), ((), ())),
                           preferred_element_type=F32)


@functools.partial(jax.custom_vjp, nondiff_argnums=(2, 3))
def bdot(a, b, ca, cb):
    return _raw_dot(a, b, ca, cb)


def _bdot_fwd(a, b, ca, cb):
    return _raw_dot(a, b, ca, cb), (a, b)


def _bdot_bwd(ca, cb, res, g):
    a, b = res
    if (ca, cb) == (1, 0):
        da, db = bdot(g, b, 1, 1), bdot(a, g, 0, 0)
    elif (ca, cb) == (1, 1):
        da, db = bdot(g, b, 1, 0), bdot(g, a, 0, 0)
    else:
        assert (ca, cb) == (0, 0)
        da, db = bdot(b, g, 1, 1), bdot(a, g, 1, 0)
    return da.astype(a.dtype), db.astype(b.dtype)


bdot.defvjp(_bdot_fwd, _bdot_bwd)


def hdot(a, b, ca=1, cb=0):
    return lax.dot_general(a, b, (((ca,), (cb,)), ((), ())), precision=lax.Precision.HIGHEST,
                           preferred_element_type=F32)


def _silu(x):
    return x * jax.nn.sigmoid(x)


def _softplus(x):
    return jnp.maximum(x, 0.0) + jnp.log1p(jnp.exp(-jnp.abs(x)))


def matmul(name, a, b, mode, out_dtype, add=None, b_blocked=False, out_blocks=0):
    if b_blocked:
        nb, rows, n = b.shape
        b_rows, b_cols = rows, nb * n
    else:
        b_rows, b_cols = b.shape
    if mode == "nn":
        (M, K), (K2, N) = a.shape, (b_rows, b_cols)
    elif mode == "nt":
        (M, K), (N, K2) = a.shape, (b_rows, b_cols)
    else:
        (K, M), (K2, N) = a.shape, (b_rows, b_cols)
    assert K == K2, (name, a.shape, b.shape, mode)
    n_blk = n if b_blocked else (N // out_blocks if out_blocks else 0)
    tm = _tile(M, 1024)
    tn = _tile(n_blk if (n_blk and mode != "nt") else N, 1536)
    tk = _tile(n_blk if (n_blk and mode == "nt") else K, 2048 if mode != "tn" else 1024)
    nk = K // tk
    dims = {"nn": (1, 0), "nt": (1, 1), "tn": (0, 0)}[mode]
    a_spec = (pl.BlockSpec((tk, tm), lambda i, j, k: (k, i)) if mode == "tn"
              else pl.BlockSpec((tm, tk), lambda i, j, k: (i, k)))
    if b_blocked and mode == "nt":
        per = n // tk
        b_spec = pl.BlockSpec((None, tn, tk), lambda i, j, k: (k // per, j, k % per))
    elif b_blocked:
        per = n // tn
        b_spec = pl.BlockSpec((None, tk, tn), lambda i, j, k: (j // per, k, j % per))
    elif mode == "nt":
        b_spec = pl.BlockSpec((tn, tk), lambda i, j, k: (j, k))
    else:
        b_spec = pl.BlockSpec((tk, tn), lambda i, j, k: (k, j))
    o_spec = pl.BlockSpec((tm, tn), lambda i, j, k: (i, j))
    out_shape = jax.ShapeDtypeStruct((M, N), out_dtype)
    if out_blocks:
        assert mode == "tn" and add is None
        per_o = n_blk // tn
        o_spec = pl.BlockSpec((None, tm, tn), lambda i, j, k: (j // per_o, i, j % per_o))
        out_shape = jax.ShapeDtypeStruct((out_blocks, M, n_blk), out_dtype)
    has_add = add is not None

    def body(*refs):
        a_ref, b_ref = refs[0], refs[1]
        c_ref = refs[2] if has_add else None
        o_ref = refs[3] if has_add else refs[2]
        p = _raw_dot(a_ref[...], b_ref[...], *dims)
        if nk == 1:
            if has_add:
                p = p + c_ref[...].astype(F32)
            o_ref[...] = p.astype(o_ref.dtype)
            return
        acc = refs[-1]
        k = pl.program_id(2)

        @pl.when(k == 0)
        def _():
            acc[...] = p + c_ref[...].astype(F32) if has_add else p

        @pl.when(k > 0)
        def _():
            acc[...] += p

        @pl.when(k == nk - 1)
        def _():
            o_ref[...] = acc[...].astype(o_ref.dtype)

    est = 2 * (_nbytes((tm, tk), a.dtype) + _nbytes((tk, tn), b.dtype) + _nbytes((tm, tn), out_dtype))
    est += 3 * _nbytes((tm, tn), F32) + _nbytes((tm, tk), MXU_DTYPE) + _nbytes((tk, tn), MXU_DTYPE)
    if has_add:
        est += 2 * _nbytes((tm, tn), add.dtype)
    args = (a, b) + ((add,) if has_add else ())
    in_specs = [a_spec, b_spec] + ([o_spec] if has_add else [])
    return pl.pallas_call(
        body, name=name, grid=(M // tm, N // tn, nk),
        in_specs=in_specs, out_specs=o_spec, out_shape=out_shape,
        scratch_shapes=[pltpu.VMEM((tm, tn), F32)] if nk > 1 else [],
        compiler_params=_params(est, ("parallel", "parallel", "arbitrary")),
    )(*args)


def row_call(name, f, rows, pars, outs, tm):
    L = rows[0].shape[0]
    tm = min(tm, L)
    nr, npar = len(rows), len(pars)

    def body(*refs):
        vals = [r[...] for r in refs[:nr + npar]]
        res = f(*vals)
        for o_ref, v in zip(refs[nr + npar:], res):
            o_ref[...] = v.astype(o_ref.dtype)

    est = 2 * sum(_nbytes((tm, a.shape[1]), a.dtype) for a in rows) + 2 * sum(_nbytes((tm, w), d) for w, d in outs)
    est += 4 * sum(_nbytes((tm, a.shape[1]), F32) for a in rows)
    res = pl.pallas_call(
        body, name=name, grid=(L // tm,),
        in_specs=[pl.BlockSpec((tm, a.shape[1]), lambda i: (i, 0)) for a in rows]
        + [pl.BlockSpec(p.shape, lambda i: (0, 0)) for p in pars],
        out_specs=[pl.BlockSpec((tm, w), lambda i: (i, 0)) for w, _ in outs],
        out_shape=[jax.ShapeDtypeStruct((L, w), d) for w, d in outs],
        compiler_params=_params(est, ("parallel",)),
    )(*rows, *pars)
    return res


def row_vjp_call(name, f, rows, pars, cots, row_grad_dtypes, par_grads, tm, adds=None):
    L = rows[0].shape[0]
    tm = min(tm, L)
    nr, npar, nc = len(rows), len(pars), len(cots)
    adds = adds or {}
    add_idx = sorted(adds)
    rg_idx = [i for i, d in enumerate(row_grad_dtypes) if d is not None]
    pg_idx = [i for i, w in enumerate(par_grads) if w]
    diff_idx = rg_idx + [nr + i for i in pg_idx]

    def body(*refs):
        vals = [r[...] for r in refs[:nr + npar]]
        cvals = [r[...] for r in refs[nr + npar:nr + npar + nc]]
        avals = [r[...] for r in refs[nr + npar + nc:nr + npar + nc + len(add_idx)]]
        out_refs = refs[nr + npar + nc + len(add_idx):]

        def g(*dv):
            full = list(vals)
            for i, v in zip(diff_idx, dv):
                full[i] = v
            return tuple(f(*full))

        res, vjp = jax.vjp(g, *[vals[i] for i in diff_idx])
        grads = vjp(tuple(c.astype(r.dtype) for c, r in zip(cvals, res)))
        for n, i in enumerate(rg_idx):
            gval = grads[n].astype(F32)
            if i in adds:
                gval = gval + avals[add_idx.index(i)].astype(F32)
            out_refs[n][...] = gval.astype(out_refs[n].dtype)
        first = pl.program_id(0) == 0
        for n, i in enumerate(pg_idx):
            o_ref = out_refs[len(rg_idx) + n]
            gval = grads[len(rg_idx) + n].astype(F32)

            @pl.when(first)
            def _(o_ref=o_ref, gval=gval):
                o_ref[...] = gval

            @pl.when(jnp.logical_not(first))
            def _(o_ref=o_ref, gval=gval):
                o_ref[...] += gval

    row_spec = lambda a: pl.BlockSpec((tm, a.shape[1]), lambda i: (i, 0))
    est = 2 * sum(_nbytes((tm, a.shape[1]), a.dtype) for a in list(rows) + list(cots))
    est += 10 * sum(_nbytes((tm, a.shape[1]), F32) for a in rows)
    res = pl.pallas_call(
        body, name=name, grid=(L // tm,),
        in_specs=[row_spec(a) for a in rows] + [pl.BlockSpec(p.shape, lambda i: (0, 0)) for p in pars]
        + [row_spec(c) for c in cots] + [row_spec(adds[i]) for i in add_idx],
        out_specs=[row_spec(rows[i]) for i in rg_idx] + [pl.BlockSpec(pars[i].shape, lambda i_: (0, 0)) for i in pg_idx],
        out_shape=[jax.ShapeDtypeStruct(rows[i].shape, row_grad_dtypes[i]) for i in rg_idx]
        + [jax.ShapeDtypeStruct(pars[i].shape, F32) for i in pg_idx],
        compiler_params=_params(est, ("arbitrary",)),
    )(*rows, *pars, *cots, *[adds[i] for i in add_idx])
    return list(res[:len(rg_idx)]), list(res[len(rg_idx):])


def _halo_specs(rows, T, nt, rev):
    specs = []
    for (_, w, ci, hs) in rows:
        if rev:
            specs.append(pl.BlockSpec((T, w), lambda j, ci=ci: (nt - 1 - j, ci)))
        else:
            specs.append(pl.BlockSpec((T, w), lambda i, ci=ci: (i, ci)))
        if hs:
            r = T // hs
            if rev:
                specs.append(pl.BlockSpec((hs, w), lambda j, ci=ci, r=r: (jnp.maximum((nt - 1 - j) * r - 1, 0), ci)))
            else:
                specs.append(pl.BlockSpec((hs, w), lambda i, ci=ci, r=r: (jnp.maximum(i * r - 1, 0), ci)))
    return specs


def _halo_args(rows):
    args = []
    for (a, _, _, hs) in rows:
        args.append(a)
        if hs:
            args.append(a)
    return args


def _halo_vals(rows, refs):
    vals, n = [], 0
    for (_, _, _, hs) in rows:
        if hs:
            vals.append((refs[n + 1][...], refs[n][...]))
            n += 2
        else:
            vals.append(refs[n][...])
            n += 1
    return vals, n


def halo_call(name, f, rows, pars, outs, T):
    L = rows[0][0].shape[0]
    T = min(T, L)
    nt = L // T
    npar = len(pars)

    def body(*refs):
        i = pl.program_id(0)
        vals, n = _halo_vals(rows, refs)
        pv = [r[...] for r in refs[n:n + npar]]
        res = f(i == 0, i * T, *vals, *pv)
        for o_ref, v in zip(refs[n + npar:], res):
            o_ref[...] = v.astype(o_ref.dtype)

    est = 2 * sum(_nbytes((T, w), a.dtype) for a, w, _, _ in rows) + 2 * sum(_nbytes((T, w), d) for w, d in outs)
    est += 8 * sum(_nbytes((T, w), F32) for _, w, _, _ in rows)
    return pl.pallas_call(
        body, name=name, grid=(nt,),
        in_specs=_halo_specs(rows, T, nt, False) + [pl.BlockSpec(p.shape, lambda i, nd=p.ndim: (0,) * nd) for p in pars],
        out_specs=[pl.BlockSpec((T, w), lambda i: (i, 0)) for w, _ in outs],
        out_shape=[jax.ShapeDtypeStruct((L, w), d) for w, d in outs],
        compiler_params=_params(est, ("parallel",)),
    )(*_halo_args(rows), *pars)


def halo_vjp_call(name, f, rows, pars, cots, row_grad_dtypes, par_grads, T):
    L = rows[0][0].shape[0]
    T = min(T, L)
    nt = L // T
    nr, npar, nc = len(rows), len(pars), len(cots)
    pg_idx = [i for i, w in enumerate(par_grads) if w]
    halo_idx = [i for i, r in enumerate(rows) if r[3]]

    def body(*refs):
        j = pl.program_id(0)
        i = nt - 1 - j
        vals, n = _halo_vals(rows, refs)
        pv = [r[...] for r in refs[n:n + npar]]
        cv = [r[...] for r in refs[n + npar:n + npar + nc]]
        out_refs = refs[n + npar + nc:n + npar + nc + nr + len(pg_idx)]
        carries = refs[n + npar + nc + nr + len(pg_idx):]
        first = i == 0

        def g(vals_, pv_):
            return tuple(f(first, i * T, *vals_, *pv_))

        res, vjp = jax.vjp(g, vals, pv)
        dvals, dpv = vjp(tuple(c.astype(r.dtype) for c, r in zip(cv, res)))

        @pl.when(j == 0)
        def _():
            for c_ref in carries:
                c_ref[...] = jnp.zeros_like(c_ref)

        for k in range(nr):
            hs = rows[k][3]
            o_ref = out_refs[k]
            if hs:
                dh, dc = dvals[k]
                c_ref = carries[halo_idx.index(k)]
                dc = dc.astype(F32)
                if hs == T:
                    o_ref[...] = (dc + c_ref[...]).astype(o_ref.dtype)
                else:
                    o_ref[0:T - hs, :] = dc[0:T - hs].astype(o_ref.dtype)
                    o_ref[T - hs:T, :] = (dc[T - hs:T] + c_ref[...]).astype(o_ref.dtype)
                c_ref[...] = dh.astype(F32)
            else:
                o_ref[...] = dvals[k].astype(o_ref.dtype)
        for m, k in enumerate(pg_idx):
            o_ref = out_refs[nr + m]
            gval = dpv[k].astype(F32)

            @pl.when(j == 0)
            def _(o_ref=o_ref, gval=gval):
                o_ref[...] = gval

            @pl.when(j > 0)
            def _(o_ref=o_ref, gval=gval):
                o_ref[...] += gval

    est = 2 * sum(_nbytes((T, w), a.dtype) for a, w, _, _ in rows) + 2 * sum(_nbytes((T, c.shape[1]), c.dtype) for c in cots)
    est += 12 * sum(_nbytes((T, w), F32) for _, w, _, _ in rows)
    res = pl.pallas_call(
        body, name=name, grid=(nt,),
        in_specs=_halo_specs(rows, T, nt, True) + [pl.BlockSpec(p.shape, lambda j, nd=p.ndim: (0,) * nd) for p in pars]
        + [pl.BlockSpec((T, c.shape[1]), lambda j: (nt - 1 - j, 0)) for c in cots],
        out_specs=[pl.BlockSpec((T, w), lambda j: (nt - 1 - j, 0)) for _, w, _, _ in rows]
        + [pl.BlockSpec(pars[k].shape, lambda j, nd=pars[k].ndim: (0,) * nd) for k in pg_idx],
        out_shape=[jax.ShapeDtypeStruct((L, w), row_grad_dtypes[k]) for k, (_, w, _, _) in enumerate(rows)]
        + [jax.ShapeDtypeStruct(pars[k].shape, F32) for k in pg_idx],
        scratch_shapes=[pltpu.VMEM((rows[k][3], rows[k][1]), F32) for k in halo_idx],
        compiler_params=_params(est, ("arbitrary",)),
    )(*_halo_args(rows), *pars, *cots)
    return list(res[:nr]), list(res[nr:])


def f_rms(x, w):
    x = x.astype(F32)
    return (x * lax.rsqrt(jnp.mean(x * x, axis=-1, keepdims=True) + EPS) * w,)


def f_swiglu(gu):
    dff = gu.shape[1] // 2
    return (_silu(gu[:, :dff]) * gu[:, dff:],)


def f_merge(gl, a, p, s):
    d = a.shape[1]
    g = jax.nn.sigmoid(gl.astype(F32))
    return (g[:, :d] * a.astype(F32) + g[:, d:2 * d] * p.astype(F32) + g[:, 2 * d:] * s.astype(F32),)


def f_gnorm(y, z, nw):
    dm = Dims()
    g = y * _silu(z.astype(F32))
    outs = []
    for gi in range(dm.G):
        gg = g[:, gi * dm.GW:(gi + 1) * dm.GW]
        outs.append(gg * lax.rsqrt(jnp.mean(gg * gg, axis=-1, keepdims=True) + EPS))
    return (jnp.concatenate(outs, axis=1) * nw,)


def f_conv(first, row0, xs, w, b):
    halo, cur = xs
    halo = jnp.where(first, 0.0, halo)
    ext = jnp.concatenate([halo, cur], axis=0)
    T = cur.shape[0]
    base = CONV_HALO - (CONV_K - 1)
    pre = b
    for k in range(CONV_K):
        pre = pre + w[k:k + 1, :] * ext[base + k:base + k + T]
    return (_silu(pre),)


def f_pool(first, row0, us, pw, scale):
    dm = Dims()
    halo, cur = us
    halo = jnp.where(first, 0.0, halo)
    ext = jnp.concatenate([halo, cur], axis=0)
    T = cur.shape[0]
    t = row0 + lax.broadcasted_iota(jnp.int32, (T, 1), 0)
    outs = []
    for gi, w in enumerate(POOL_WINDOWS):
        assert w & (w - 1) == 0 and w <= POOL_HALO
        s = ext[:, gi * dm.PC:(gi + 1) * dm.PC]
        sh = 1
        while sh < w:
            s = s + jnp.concatenate([jnp.zeros((sh, dm.PC), F32), s[:-sh]], axis=0)
            sh *= 2
        cnt = jnp.minimum(t + 1, w).astype(F32)
        mixed = s[POOL_HALO:] / cnt - cur[:, gi * dm.PC:(gi + 1) * dm.PC]
        outs.append(bdot(mixed, pw[gi], 1, 0))
    return (jnp.concatenate(outs, axis=1) * scale,)


def f_attn(first, row0, q, ks, vs, sink):
    dm = Dims()
    kp, kc = ks
    vp, vc = vs
    T = q.shape[0]
    hd = ATT_HEAD_DIM
    qi = lax.broadcasted_iota(jnp.int32, (T, 2 * T), 0)
    si = lax.broadcasted_iota(jnp.int32, (T, 2 * T), 1)
    diff = qi + T - si
    ok = (diff >= 0) & (diff < WINDOW) & ((si >= T) | jnp.logical_not(first))
    mask = jnp.concatenate([ok] * dm.GQ, axis=0)
    outs = []
    for k in range(ATT_KV_HEADS):
        kk = jnp.concatenate([kp[:, k * hd:(k + 1) * hd], kc[:, k * hd:(k + 1) * hd]], axis=0)
        vv = jnp.concatenate([vp[:, k * hd:(k + 1) * hd], vc[:, k * hd:(k + 1) * hd]], axis=0)
        heads = [k * dm.GQ + g for g in range(dm.GQ)]
        qs = jnp.concatenate([q[:, h * hd:(h + 1) * hd] for h in heads], axis=0)
        s = bdot(qs, kk, 1, 1) * (hd ** -0.5)
        s = jnp.where(mask, s, NEG)
        sk = jnp.concatenate([jnp.broadcast_to(sink[:, h:h + 1], (T, 1)) for h in heads], axis=0)
        m = jnp.maximum(jnp.max(s, axis=-1, keepdims=True), sk)
        p = jnp.exp(s - m)
        den = jnp.sum(p, axis=-1, keepdims=True) + jnp.exp(sk - m)
        o = bdot(p / den, vv, 1, 0)
        outs += [o[g * T:(g + 1) * T] for g in range(dm.GQ)]
    return (jnp.concatenate(outs, axis=1),)


def f_ssd(xc, dtr, ht, dt_bias, a_log, d_skip):
    dm = Dims()
    Q = xc.shape[0]
    xs = xc[:, :dm.DI]
    bm = xc[:, dm.DI:dm.DI + dm.GN]
    cm = xc[:, dm.DI + dm.GN:]
    expand = (lax.broadcasted_iota(jnp.int32, (dm.H, dm.DI), 1) // dm.P
              == lax.broadcasted_iota(jnp.int32, (dm.H, dm.DI), 0)).astype(F32)
    ri = lax.broadcasted_iota(jnp.int32, (Q, Q), 0)
    ci = lax.broadcasted_iota(jnp.int32, (Q, Q), 1)
    causal = ri >= ci
    tinc = causal.astype(F32)
    dt = _softplus(dtr[:, :dm.H] + dt_bias)
    da = dt * (-jnp.exp(a_log))
    acs = hdot(tinc, da)
    acs_t = hdot(da, tinc, 0, 1)
    eacs = jnp.exp(acs)
    dend = jnp.exp(acs[Q - 1:Q, :] - acs)
    ex = hdot(jnp.concatenate([dt, eacs, dend, jnp.broadcast_to(d_skip, (8, dm.H))], axis=0), expand)
    dt_x, eacs_x, dend_x, dsk_x = ex[:Q], ex[Q:2 * Q], ex[2 * Q:3 * Q], ex[3 * Q:3 * Q + 1]
    xdt = xs * dt_x
    ys, hts = [], []
    for g in range(dm.G):
        gs = slice(g * dm.GW, (g + 1) * dm.GW)
        bg = bm[:, g * dm.N:(g + 1) * dm.N]
        cg = cm[:, g * dm.N:(g + 1) * dm.N]
        cb = bdot(cg, bg, 1, 1)
        y_off = bdot(cg, ht[:, gs], 1, 0) * eacs_x[:, gs]
        xg = xdt[:, gs]
        st = bdot(bg, xg * dend_x[:, gs], 0, 0)
        hts.append(ht[:, gs] * eacs_x[Q - 1:Q, gs] + st)
        yd = []
        for e in range(dm.HPG):
            h = g * dm.HPG + e
            seg = acs[:, h:h + 1] - acs_t[h:h + 1, :]
            lm = jnp.exp(jnp.where(causal, seg, NEG))
            yd.append(bdot(cb * lm, xg[:, e * dm.P:(e + 1) * dm.P], 1, 0))
        ys.append(jnp.concatenate(yd, axis=1) + y_off)
    y = jnp.concatenate(ys, axis=1) + dsk_x * xs
    return y, jnp.concatenate(hts, axis=1)


def ssd_fwd(name, xc, dtr, dt_bias, a_log, d_skip):
    dm = Dims()
    L = xc.shape[0]
    Q = CHUNK
    nc = L // Q

    def body(xc_ref, dtr_ref, b_ref, a_ref, s_ref, y_ref, hts_ref, ht):
        @pl.when(pl.program_id(0) == 0)
        def _():
            ht[...] = jnp.zeros_like(ht)

        h0 = ht[...]
        hts_ref[0] = h0
        y, h1 = f_ssd(xc_ref[...], dtr_ref[...], h0, b_ref[...], a_ref[...], s_ref[...])
        y_ref[...] = y
        ht[...] = h1

    par = pl.BlockSpec((1, dm.H), lambda c: (0, 0))
    est = 40 * _nbytes((Q, dm.CC), F32) + 4 * _nbytes((dm.N, dm.DI), F32)
    return pl.pallas_call(
        body, name=name, grid=(nc,),
        in_specs=[pl.BlockSpec((Q, dm.CC), lambda c: (c, 0)), pl.BlockSpec((Q, dm.DTP), lambda c: (c, 0)), par, par, par],
        out_specs=[pl.BlockSpec((Q, dm.DI), lambda c: (c, 0)), pl.BlockSpec((1, dm.N, dm.DI), lambda c: (c, 0, 0))],
        out_shape=[jax.ShapeDtypeStruct((L, dm.DI), F32), jax.ShapeDtypeStruct((nc, dm.N, dm.DI), F32)],
        scratch_shapes=[pltpu.VMEM((dm.N, dm.DI), F32)],
        compiler_params=_params(est, ("arbitrary",)),
    )(xc, dtr, dt_bias, a_log, d_skip)


def ssd_bwd(name, xc, dtr, hts, dt_bias, a_log, d_skip, dy):
    dm = Dims()
    L = xc.shape[0]
    Q = CHUNK
    nc = L // Q

    def body(xc_ref, dtr_ref, hts_ref, b_ref, a_ref, s_ref, dy_ref, dxc_ref, ddtr_ref, db_ref, da_ref, ds_ref, dht):
        j = pl.program_id(0)

        @pl.when(j == 0)
        def _():
            dht[...] = jnp.zeros_like(dht)

        _, vjp = jax.vjp(f_ssd, xc_ref[...], dtr_ref[...], hts_ref[0], b_ref[...], a_ref[...], s_ref[...])
        dxc, ddtr, dh0, db, da, ds = vjp((dy_ref[...], dht[...]))
        dxc_ref[...] = dxc.astype(dxc_ref.dtype)
        ddtr_ref[...] = ddtr.astype(ddtr_ref.dtype)
        dht[...] = dh0
        for o_ref, gval in ((db_ref, db), (da_ref, da), (ds_ref, ds)):
            @pl.when(j == 0)
            def _(o_ref=o_ref, gval=gval):
                o_ref[...] = gval

            @pl.when(j > 0)
            def _(o_ref=o_ref, gval=gval):
                o_ref[...] += gval

    par = pl.BlockSpec((1, dm.H), lambda j: (0, 0))
    rev = lambda w: pl.BlockSpec((Q, w), lambda j: (nc - 1 - j, 0))
    est = 80 * _nbytes((Q, dm.CC), F32) + 6 * _nbytes((dm.N, dm.DI), F32)
    return pl.pallas_call(
        body, name=name, grid=(nc,),
        in_specs=[rev(dm.CC), rev(dm.DTP), pl.BlockSpec((1, dm.N, dm.DI), lambda j: (nc - 1 - j, 0, 0)), par, par, par, rev(dm.DI)],
        out_specs=[rev(dm.CC), rev(dm.DTP), par, par, par],
        out_shape=[jax.ShapeDtypeStruct((L, dm.CC), F32), jax.ShapeDtypeStruct((L, dm.DTP), BF16)]
        + [jax.ShapeDtypeStruct((1, dm.H), F32)] * 3,
        scratch_shapes=[pltpu.VMEM((dm.N, dm.DI), F32)],
        compiler_params=_params(est, ("arbitrary",)),
    )(xc, dtr, hts, dt_bias, a_log, d_skip, dy)


def loss_head(x, w, target, tm):
    L, D = x.shape
    tm = min(tm, L)

    def tile_loss(xv, wv, tv):
        (y,) = f_rms(xv, wv)
        return 0.5 * jnp.sum(jnp.mean(jnp.square(y - tv), axis=-1))

    def body(x_ref, w_ref, t_ref, dx_ref, dw_ref, loss_ref):
        val, (dx, dw) = jax.value_and_grad(tile_loss, argnums=(0, 1))(x_ref[...], w_ref[...], t_ref[...])
        dx_ref[...] = dx
        first = pl.program_id(0) == 0
        lv = jnp.full((8, 128), val, F32)

        @pl.when(first)
        def _():
            dw_ref[...] = dw
            loss_ref[...] = lv

        @pl.when(jnp.logical_not(first))
        def _():
            dw_ref[...] += dw
            loss_ref[...] += lv

    row = pl.BlockSpec((tm, D), lambda i: (i, 0))
    est = 16 * _nbytes((tm, D), F32)
    return pl.pallas_call(
        body, name="loss_head", grid=(L // tm,),
        in_specs=[row, pl.BlockSpec((1, D), lambda i: (0, 0)), row],
        out_specs=[row, pl.BlockSpec((1, D), lambda i: (0, 0)), pl.BlockSpec((8, 128), lambda i: (0, 0))],
        out_shape=[jax.ShapeDtypeStruct((L, D), F32), jax.ShapeDtypeStruct((1, D), F32), jax.ShapeDtypeStruct((8, 128), F32)],
        compiler_params=_params(est, ("arbitrary",)),
    )(x, w, target)


def _adamw(w, g, m, v):
    m = ADAM_B1 * m + (1.0 - ADAM_B1) * g
    v = ADAM_B2 * v + (1.0 - ADAM_B2) * jnp.square(g)
    m_hat = m / (1.0 - ADAM_B1 ** ADAM_STEP)
    v_hat = v / (1.0 - ADAM_B2 ** ADAM_STEP)
    delta = -ADAM_LR * (m_hat / (jnp.sqrt(v_hat) + ADAM_EPS) + ADAM_WD * w)
    return delta, m, v


def adamw_sharded(name, parts, w, m, v, li, carried):
    depth, R, C = w.shape
    tr = _tile(R, 128, 16)
    n_in = 4 + (4 if carried else 0)

    def body(*refs):
        p_ref, w_ref, m_ref, v_ref = refs[:4]
        g_ref, d_ref, nm_ref, nv_ref = refs[n_in:n_in + 4]
        g = p_ref[0].astype(F32)
        for s in range(1, N_DEV):
            g = g + p_ref[s].astype(F32)
        d, nm, nv = _adamw(w_ref[...], g, m_ref[...], v_ref[...])
        g_ref[...] = g
        d_ref[...] = d
        nm_ref[...] = nm
        nv_ref[...] = nv

    row = pl.BlockSpec((None, tr, C), lambda i: (li, i, 0))
    est = 2 * _nbytes((N_DEV, tr, C), parts.dtype) + 20 * _nbytes((tr, C), F32)
    return pl.pallas_call(
        body, name=name, grid=(R // tr,),
        in_specs=[pl.BlockSpec((N_DEV, tr, C), lambda i: (0, i, 0)), row, row, row]
        + ([pl.BlockSpec(memory_space=pl.ANY)] * 4 if carried else []),
        out_specs=[row] * 4,
        out_shape=[jax.ShapeDtypeStruct((depth, R, C), F32)] * 4,
        input_output_aliases={4 + k: k for k in range(4)} if carried else {},
        compiler_params=_params(est, ("parallel",)),
    )(parts, w, m, v, *(carried or ()))


def _my_place():
    return lax.axis_index("x"), lax.axis_index("y"), lax.axis_index("c")


def all_gather_hbm(name, shard):
    R, W = shard.shape

    def body(x_ref, out_ref, send_sems, recv_sems, local_sem):
        x, y, c = _my_place()
        me, sibling = (x, y, c), (x, y, 1 - c)
        chips = [(1 - x, y), (x, 1 - y), (1 - x, 1 - y)]

        def slot(px, py, pc):
            return out_ref.at[4 * px + 2 * py + pc]

        def copy(k, block, to, src=None):
            return pltpu.make_async_remote_copy(
                src_ref=slot(*block) if src is None else src, dst_ref=slot(*block),
                send_sem=send_sems.at[k], recv_sem=recv_sems.at[k], device_id=to, device_id_type=MESH_T)

        mine = pltpu.make_async_copy(x_ref, slot(*me), local_sem)
        mine.start()
        first = [copy(0, me, sibling, src=x_ref)]
        first += [copy(1 + j, me, (*chip, c), src=x_ref) for j, chip in enumerate(chips)]
        for cp in first:
            cp.start()
        passed = [copy(4 + j, (*chip, c), sibling) for j, chip in enumerate(chips)]
        for j, chip in enumerate(chips):
            copy(1 + j, (*chip, c), me).wait_recv()
            passed[j].start()
        copy(0, sibling, me).wait_recv()
        for j, chip in enumerate(chips):
            copy(4 + j, (*chip, 1 - c), me).wait_recv()
        for cp in first + passed:
            cp.wait_send()
        mine.wait()

    return pl.pallas_call(
        body, name=name,
        out_shape=jax.ShapeDtypeStruct((N_DEV, R, W), shard.dtype),
        in_specs=[pl.BlockSpec(memory_space=pl.ANY)],
        out_specs=pl.BlockSpec(memory_space=pl.ANY),
        scratch_shapes=[pltpu.SemaphoreType.DMA((7,)), pltpu.SemaphoreType.DMA((7,)), pltpu.SemaphoreType.DMA],
    )(shard)


HBM_SPEC = pl.BlockSpec(memory_space=pltpu.HBM)
SEM_SPEC = pl.BlockSpec(memory_space=pltpu.SEMAPHORE)
DATAFLOW = pltpu.SideEffectType.DATAFLOW_SIDE_EFFECTING


def _me():
    x, y, c = _my_place()
    return 4 * x + 2 * y + c


def copies_start(name, src, gather):
    blk = src.shape if gather else src.shape[1:]
    mine = src[None] if gather else lax.dynamic_slice_in_dim(src, _me(), 1, axis=0)
    land = lax.dynamic_update_slice(jnp.zeros((N_DEV,) + tuple(blk), src.dtype), mine, (_me(), 0, 0))

    def body(src_ref, land_ref, send_sem, recv_sem, src_thru, land_thru, token):
        x, y, c = _my_place()
        me = 4 * x + 2 * y + c
        for k in range(1, N_DEV):
            px, py, pc = (x + (k >> 2)) % 2, (y + ((k >> 1) & 1)) % 2, (c + (k & 1)) % 2
            pltpu.make_async_remote_copy(
                src_ref=src_ref if gather else src_ref.at[4 * px + 2 * py + pc], dst_ref=land_ref.at[me],
                send_sem=send_sem, recv_sem=recv_sem, device_id=(px, py, pc), device_id_type=MESH_T).start()
        token[...] = jnp.zeros_like(token)

    return pl.pallas_call(
        body, name=name,
        out_shape=(pltpu.SemaphoreType.DMA(()), pltpu.SemaphoreType.DMA(()), pltpu.HBM(src.shape, src.dtype),
                   pltpu.HBM(land.shape, land.dtype), jax.ShapeDtypeStruct((8, 128), F32)),
        in_specs=(HBM_SPEC, HBM_SPEC),
        out_specs=(SEM_SPEC, SEM_SPEC, HBM_SPEC, HBM_SPEC, pl.BlockSpec(memory_space=pltpu.VMEM)),
        input_output_aliases={0: 2, 1: 3},
        compiler_params=pltpu.CompilerParams(has_side_effects=DATAFLOW),
    )(pltpu.with_memory_space_constraint(src, pltpu.HBM), pltpu.with_memory_space_constraint(land, pltpu.HBM))


def copies_wait(name, started, after):
    send_sem, recv_sem, src_thru, land_thru, _ = started

    def body(src_ref, land_ref, send_sem, recv_sem, after_ref, src_dead, got_ref):
        seven = land_ref.at[pl.ds(0, N_DEV - 1)]
        all_seven = pltpu.make_async_remote_copy(src_ref=seven, dst_ref=seven, send_sem=send_sem, recv_sem=recv_sem,
                                                 device_id=_my_place(), device_id_type=MESH_T)
        all_seven.wait_send()
        all_seven.wait_recv()

    return pl.pallas_call(
        body, name=name,
        out_shape=(pltpu.HBM(src_thru.shape, src_thru.dtype), pltpu.HBM(land_thru.shape, land_thru.dtype)),
        in_specs=(HBM_SPEC, HBM_SPEC, SEM_SPEC, SEM_SPEC, pl.BlockSpec(memory_space=pl.ANY)),
        out_specs=(HBM_SPEC, HBM_SPEC), input_output_aliases={0: 0, 1: 1},
        compiler_params=pltpu.CompilerParams(has_side_effects=DATAFLOW),
    )(src_thru, land_thru, send_sem, recv_sem, after)[1]


def small_allreduce_adamw(part, w, m, v):
    R, W = part.shape

    def body(x_ref, w_ref, m_ref, v_ref, g_ref, d_ref, nm_ref, nv_ref, all_ref, send_sems, recv_sems, local_sem):
        x, y, c = _my_place()
        me, sibling = (x, y, c), (x, y, 1 - c)
        chips = [(1 - x, y), (x, 1 - y), (1 - x, 1 - y)]

        def slot(px, py, pc):
            return all_ref.at[4 * px + 2 * py + pc]

        def copy(k, block, to, src=None):
            return pltpu.make_async_remote_copy(
                src_ref=slot(*block) if src is None else src, dst_ref=slot(*block),
                send_sem=send_sems.at[k], recv_sem=recv_sems.at[k], device_id=to, device_id_type=MESH_T)

        mine = pltpu.make_async_copy(x_ref, slot(*me), local_sem)
        mine.start()
        first = [copy(0, me, sibling, src=x_ref)]
        first += [copy(1 + j, me, (*chip, c), src=x_ref) for j, chip in enumerate(chips)]
        for cp in first:
            cp.start()
        passed = [copy(4 + j, (*chip, c), sibling) for j, chip in enumerate(chips)]
        for j, chip in enumerate(chips):
            copy(1 + j, (*chip, c), me).wait_recv()
            passed[j].start()
        copy(0, sibling, me).wait_recv()
        for j, chip in enumerate(chips):
            copy(4 + j, (*chip, 1 - c), me).wait_recv()
        for cp in first + passed:
            cp.wait_send()
        mine.wait()
        g = all_ref[0]
        for s in range(1, N_DEV):
            g = g + all_ref[s]
        d, nm, nv = _adamw(w_ref[...], g, m_ref[...], v_ref[...])
        g_ref[...] = g
        d_ref[...] = d
        nm_ref[...] = nm
        nv_ref[...] = nv

    vm = pl.BlockSpec(memory_space=pltpu.VMEM)
    return pl.pallas_call(
        body, name="small_allreduce_adamw",
        out_shape=[jax.ShapeDtypeStruct((R, W), F32)] * 4,
        in_specs=[vm] * 4, out_specs=[vm] * 4,
        scratch_shapes=[pltpu.VMEM((N_DEV, R, W), F32), pltpu.SemaphoreType.DMA((7,)), pltpu.SemaphoreType.DMA((7,)),
                        pltpu.SemaphoreType.DMA],
    )(part, w, m, v)


SHARDED = ("w_in", "conv_w", "pool_w", "w_attn_br", "w_pool_br", "w_ssm_br", "w_out", "w_gate_up", "w_down")
REPLICATED = ("ln1_w", "attn_sink", "conv_b", "dt_bias", "a_log", "d_skip", "ssm_norm_w", "pool_scale", "ln2_w")
CONV_ROWS = 8


class Pending:
    def __init__(self, name, started, layout):
        self.name, self.started, self.layout = name, started, layout


def need(fw, n, after):
    if n not in fw:
        n_src = "w_in"
        fw.update(fw[n_src].layout(copies_wait(fw[n_src].name, fw[n_src].started, after)))
        del fw[n_src]
    elif isinstance(fw[n], Pending):
        fw[n] = fw[n].layout(copies_wait(fw[n].name, fw[n].started, after))
    return fw[n]


def _w_in_layout(dm, g_in):
    win = jnp.concatenate([g_in[d] for d in range(N_DEV)], axis=1)
    pts, acc = [], 0
    for wd in dm.in_widths:
        pts.append((acc, acc + wd))
        acc += wd
    cols = lambda k: win[:, pts[k][0]:pts[k][1]]
    fw = {"w_qkv": win[:, :pts[2][1]], "w_u": cols(3), "w_z": cols(4), "w_xbc": cols(5), "w_gl": cols(7)}
    fw["w_dt"] = jnp.pad(cols(6), ((0, 0), (0, dm.DTP - dm.H)))
    fw["w_in_int"] = jnp.concatenate([win[:, :pts[5][1]], fw["w_dt"], fw["w_gl"]], axis=1)
    return fw


def _gather_weights(dm, W, li, w_in_now):
    tag = f"ag_l{li}_"
    fw, tokens = {}, []

    def start(n, shard, layout):
        st = copies_start(tag + n + "_start", shard, True)
        tokens.append(st[4])
        fw[n] = Pending(tag + n + "_wait", st, layout)

    if w_in_now:
        fw.update(_w_in_layout(dm, all_gather_hbm(tag + "w_in", W["w_in"][li].astype(BF16))))
    else:
        start("w_in", W["w_in"][li].astype(BF16), functools.partial(_w_in_layout, dm))
    start("pool_w", W["pool_w"][li].astype(BF16).reshape(dm.PG * dm.PC // N_DEV, dm.PC),
          lambda g: g.reshape(N_DEV, dm.PG, dm.PC // N_DEV, dm.PC).transpose(1, 0, 2, 3)
          .reshape(dm.PG, dm.PC, dm.PC).astype(F32))
    start("conv_w", jnp.pad(W["conv_w"][li], ((0, CONV_ROWS - CONV_K), (0, 0))),
          lambda g: g[:, :CONV_K].transpose(1, 0, 2).reshape(CONV_K, dm.CC))
    for n in ("w_attn_br", "w_pool_br", "w_ssm_br", "w_out", "w_gate_up", "w_down"):
        if n in ("w_ssm_br", "w_out", "w_down"):
            layout = lambda g: g.reshape(g.shape[0] * g.shape[1], g.shape[2])
        else:
            layout = lambda g: g
        start(n, W[n][li].astype(BF16), layout)
    return fw, tokens


def _grad_blocks(dm, n, g):
    if n == "w_in":
        o = [0]
        for wd in dm.seg:
            o.append(o[-1] + wd)
        ref_cols = jnp.concatenate([g[:, :o[4]], g[:, o[4]:o[4] + dm.H], g[:, o[5]:]], axis=1)
        per = dm.IN_COLS // N_DEV
        return jnp.stack([ref_cols[:, d * per:(d + 1) * per] for d in range(N_DEV)])
    if n in ("w_attn_br", "w_pool_br", "w_gate_up"):
        return g
    if n in ("w_ssm_br", "w_out", "w_down"):
        return g.reshape(N_DEV, g.shape[0] // N_DEV, g.shape[1])
    if n == "pool_w":
        g = g.reshape(dm.PG, N_DEV, dm.PC // N_DEV, dm.PC).transpose(1, 0, 2, 3)
        return g.reshape(N_DEV, dm.PG * dm.PC // N_DEV, dm.PC).astype(BF16)
    assert n == "conv_w"
    g = g.reshape(CONV_K, N_DEV, dm.CC // N_DEV).transpose(1, 0, 2)
    return jnp.pad(g, ((0, 0), (0, CONV_ROWS - CONV_K), (0, 0)))


def _as_rows(n, a):
    if n == "pool_w":
        return a.reshape(a.shape[0], a.shape[1] * a.shape[2], a.shape[3])
    if n == "conv_w":
        return jnp.pad(a, ((0, 0), (0, CONV_ROWS - CONV_K), (0, 0)))
    return a


def _from_rows(n, a, like):
    if n == "pool_w":
        return a.reshape(like.shape)
    if n == "conv_w":
        return a[:, :CONV_K]
    return a


def _size(shape):
    n = 1
    for s in shape:
        n *= s
    return n


def _layer_forward(dm, x, wts, rep, li):
    tag = f"l{li}_"
    sv = {"x": x}
    (h,) = row_call(tag + "rms1", f_rms, [x], [rep["ln1_w"]], [(dm.D, BF16)], 256)
    sv["h"] = h
    qkv = matmul(tag + "p_qkv", h, need(wts, "w_qkv", x), "nn", BF16)
    u = matmul(tag + "p_u", h, wts["w_u"], "nn", F32)
    z = matmul(tag + "p_z", h, wts["w_z"], "nn", F32)
    xbc = matmul(tag + "p_xbc", h, wts["w_xbc"], "nn", F32)
    dtr = matmul(tag + "p_dt", h, wts["w_dt"], "nn", F32)
    gl = matmul(tag + "p_gl", h, wts["w_gl"], "nn", F32)
    sv.update(qkv=qkv, u=u, z=z, xbc=xbc, dtr=dtr, gl=gl)
    kvi = dm.AW // dm.KVW
    (att,) = halo_call(tag + "attn", f_attn,
                       [(qkv, dm.AW, 0, 0), (qkv, dm.KVW, kvi, WINDOW), (qkv, dm.KVW, kvi + 1, WINDOW)],
                       [rep["attn_sink"]], [(dm.AW, BF16)], WINDOW)
    (pool,) = halo_call(tag + "pool", f_pool, [(u, dm.PW, 0, POOL_HALO)],
                        [need(wts, "pool_w", att), rep["pool_scale"]], [(dm.PW, BF16)], 256)
    (xc,) = halo_call(tag + "conv", f_conv, [(xbc, dm.CC, 0, CONV_HALO)],
                      [need(wts, "conv_w", pool), rep["conv_b"]], [(dm.CC, F32)], 256)
    y, hts = ssd_fwd(tag + "ssd", xc, dtr, rep["dt_bias"], rep["a_log"], rep["d_skip"])
    (ssm,) = row_call(tag + "gnorm", f_gnorm, [y, z], [rep["ssm_norm_w"]], [(dm.DI, BF16)], 256)
    sv.update(att=att, pool=pool, xc=xc, y=y, hts=hts, ssm=ssm)
    ba = matmul(tag + "br_a", att, need(wts, "w_attn_br", ssm), "nn", F32, b_blocked=True)
    bp = matmul(tag + "br_p", pool, need(wts, "w_pool_br", ba), "nn", F32, b_blocked=True)
    bs = matmul(tag + "br_s", ssm, need(wts, "w_ssm_br", bp), "nn", F32)
    (merged,) = row_call(tag + "merge", f_merge, [gl, ba, bp, bs], [], [(dm.D, BF16)], 256)
    x1 = matmul(tag + "out", merged, need(wts, "w_out", merged), "nn", F32, add=x)
    (h2,) = row_call(tag + "rms2", f_rms, [x1], [rep["ln2_w"]], [(dm.D, BF16)], 256)
    gu = matmul(tag + "gu", h2, need(wts, "w_gate_up", h2), "nn", F32, b_blocked=True)
    (act,) = row_call(tag + "swiglu", f_swiglu, [gu], [], [(dm.DFF, BF16)], 256)
    x2 = matmul(tag + "down", act, need(wts, "w_down", act), "nn", F32, add=x1)
    sv.update(ba=ba, bp=bp, bs=bs, merged=merged, x1=x1, h2=h2, gu=gu, act=act)
    return x2, sv


class GradSink:
    def __init__(self, dm, li):
        self.dm, self.li, self.started = dm, li, {}

    def __setitem__(self, n, g):
        self.started[n] = copies_start(f"rs_l{self.li}_{n}_start", _grad_blocks(self.dm, n, g), False)


def _layer_backward(dm, dx2, sv, wts, rep, li):
    tag = f"l{li}_b_"
    gw, gr = GradSink(dm, li), {}
    dact = matmul(tag + "d_act", dx2, wts["w_down"], "nt", F32)
    gw["w_down"] = matmul(tag + "g_down", sv["act"], dx2, "tn", BF16)
    (dgu,), _ = row_vjp_call(tag + "swiglu", f_swiglu, [sv["gu"]], [], [dact], [BF16], [], 256)
    dh2 = matmul(tag + "d_h2", dgu, wts["w_gate_up"], "nt", F32, b_blocked=True)
    gw["w_gate_up"] = matmul(tag + "g_gu", sv["h2"], dgu, "tn", BF16, out_blocks=N_DEV)
    (dx1,), (gr["ln2_w"],) = row_vjp_call(tag + "rms2", f_rms, [sv["x1"]], [rep["ln2_w"]], [dh2], [F32], [True], 256,
                                          adds={0: dx2})
    dmerged = matmul(tag + "d_merged", dx1, wts["w_out"], "nt", F32)
    gw["w_out"] = matmul(tag + "g_out", sv["merged"], dx1, "tn", BF16)
    (dgl, dba, dbp, dbs), _ = row_vjp_call(tag + "merge", f_merge, [sv["gl"], sv["ba"], sv["bp"], sv["bs"]], [],
                                           [dmerged], [BF16, BF16, BF16, BF16], [], 256)
    datt = matmul(tag + "d_att", dba, wts["w_attn_br"], "nt", F32, b_blocked=True)
    gw["w_attn_br"] = matmul(tag + "g_br_a", sv["att"], dba, "tn", BF16, out_blocks=N_DEV)
    dpool = matmul(tag + "d_pool", dbp, wts["w_pool_br"], "nt", F32, b_blocked=True)
    gw["w_pool_br"] = matmul(tag + "g_br_p", sv["pool"], dbp, "tn", BF16, out_blocks=N_DEV)
    dssm = matmul(tag + "d_ssm", dbs, wts["w_ssm_br"], "nt", F32)
    gw["w_ssm_br"] = matmul(tag + "g_br_s", sv["ssm"], dbs, "tn", BF16)
    (dy, dz), (gr["ssm_norm_w"],) = row_vjp_call(tag + "gnorm", f_gnorm, [sv["y"], sv["z"]], [rep["ssm_norm_w"]],
                                                 [dssm], [F32, BF16], [True], 256)
    dxc, ddtr, gr["dt_bias"], gr["a_log"], gr["d_skip"] = ssd_bwd(
        tag + "ssd", sv["xc"], sv["dtr"], sv["hts"], rep["dt_bias"], rep["a_log"], rep["d_skip"], dy)
    (dxbc,), (gw["conv_w"], gr["conv_b"]) = halo_vjp_call(
        tag + "conv", f_conv, [(sv["xbc"], dm.CC, 0, CONV_HALO)], [wts["conv_w"], rep["conv_b"]], [dxc],
        [BF16], [True, True], 256)
    (du,), (gw["pool_w"], gr["pool_scale"]) = halo_vjp_call(
        tag + "pool", f_pool, [(sv["u"], dm.PW, 0, POOL_HALO)], [wts["pool_w"], rep["pool_scale"]], [dpool],
        [BF16], [True, True], 256)
    kvi = dm.AW // dm.KVW
    qkv = sv["qkv"]
    (dq, dk, dv), (gr["attn_sink"],) = halo_vjp_call(
        tag + "attn", f_attn, [(qkv, dm.AW, 0, 0), (qkv, dm.KVW, kvi, WINDOW), (qkv, dm.KVW, kvi + 1, WINDOW)],
        [rep["attn_sink"]], [datt], [BF16, BF16, BF16], [True], WINDOW)
    dproj = jnp.concatenate([dq, dk, dv, du, dz, dxbc, ddtr, dgl], axis=1)
    dh = matmul(tag + "d_h", dproj, wts["w_in_int"], "nt", F32)
    gw["w_in"] = matmul(tag + "g_in", sv["h"], dproj, "tn", BF16)
    (dx,), (gr["ln1_w"],) = row_vjp_call(tag + "rms1", f_rms, [sv["x"]], [rep["ln1_w"]], [dh], [F32], [True], 256,
                                         adds={0: dx1})
    return dx, gw.started, gr


def kernel(x, ln1_w, w_in, attn_sink, conv_w, conv_b, dt_bias, a_log, d_skip, ssm_norm_w, pool_w, pool_scale, w_attn_br, w_pool_br, w_ssm_br, w_out, ln2_w, w_gate_up, w_down, final_w, loss_target, m_ln1_w, m_w_in, m_attn_sink, m_conv_w, m_conv_b, m_dt_bias, m_a_log, m_d_skip, m_ssm_norm_w, m_pool_w, m_pool_scale, m_w_attn_br, m_w_pool_br, m_w_ssm_br, m_w_out, m_ln2_w, m_w_gate_up, m_w_down, m_final_w, v_ln1_w, v_w_in, v_attn_sink, v_conv_w, v_conv_b, v_dt_bias, v_a_log, v_d_skip, v_ssm_norm_w, v_pool_w, v_pool_scale, v_w_attn_br, v_w_pool_br, v_w_ssm_br, v_w_out, v_ln2_w, v_w_gate_up, v_w_down, v_final_w):
    dm = Dims()
    W = dict(ln1_w=ln1_w, w_in=w_in, attn_sink=attn_sink, conv_w=conv_w, conv_b=conv_b, dt_bias=dt_bias, a_log=a_log,
             d_skip=d_skip, ssm_norm_w=ssm_norm_w, pool_w=pool_w, pool_scale=pool_scale, w_attn_br=w_attn_br,
             w_pool_br=w_pool_br, w_ssm_br=w_ssm_br, w_out=w_out, ln2_w=ln2_w, w_gate_up=w_gate_up, w_down=w_down,
             final_w=final_w)
    M = dict(ln1_w=m_ln1_w, w_in=m_w_in, attn_sink=m_attn_sink, conv_w=m_conv_w, conv_b=m_conv_b, dt_bias=m_dt_bias,
             a_log=m_a_log, d_skip=m_d_skip, ssm_norm_w=m_ssm_norm_w, pool_w=m_pool_w, pool_scale=m_pool_scale,
             w_attn_br=m_w_attn_br, w_pool_br=m_w_pool_br, w_ssm_br=m_w_ssm_br, w_out=m_w_out, ln2_w=m_ln2_w,
             w_gate_up=m_w_gate_up, w_down=m_w_down, final_w=m_final_w)
    V = dict(ln1_w=v_ln1_w, w_in=v_w_in, attn_sink=v_attn_sink, conv_w=v_conv_w, conv_b=v_conv_b, dt_bias=v_dt_bias,
             a_log=v_a_log, d_skip=v_d_skip, ssm_norm_w=v_ssm_norm_w, pool_w=v_pool_w, pool_scale=v_pool_scale,
             w_attn_br=v_w_attn_br, w_pool_br=v_w_pool_br, w_ssm_br=v_w_ssm_br, w_out=v_w_out, ln2_w=v_ln2_w,
             w_gate_up=v_w_gate_up, w_down=v_w_down, final_w=v_final_w)
    xl = x[0]
    target = loss_target[0]

    full, tokens = [], []
    for li in range(DEPTH):
        fw, tk = _gather_weights(dm, W, li, w_in_now=(li == 0))
        full.append(fw)
        tokens += tk
    rep = [{n: W[n][li].reshape(1, -1) for n in REPLICATED} for li in range(DEPTH)]
    rep[0]["ln1_w"] = rep[0]["ln1_w"] + sum(t[0, 0] for t in tokens)

    saved = []
    xa = xl
    for li in range(DEPTH):
        xa, sv = _layer_forward(dm, xa, full[li], rep[li], li)
        saved.append(sv)
    dxa, g_final, loss_blk = loss_head(xa, W["final_w"].reshape(1, -1), target, 256)

    gws, grs = [None] * DEPTH, [None] * DEPTH
    for li in reversed(range(DEPTH)):
        dxa, gws[li], grs[li] = _layer_backward(dm, dxa, saved[li], full[li], rep[li], li)
    grad_x = dxa[None]

    sharded_out = {n: None for n in SHARDED}
    for li in reversed(range(DEPTH)):
        for n in SHARDED:
            parts = copies_wait(f"rs_l{li}_{n}_wait", gws[li][n], dxa)
            sharded_out[n] = adamw_sharded(f"adamw_l{li}_{n}", parts, _as_rows(n, W[n]), _as_rows(n, M[n]),
                                           _as_rows(n, V[n]), li, sharded_out[n])
    g_sh, d_sh, m_sh, v_sh = [{n: _from_rows(n, sharded_out[n][k], W[n]) for n in SHARDED} for k in range(4)]

    small_names = [(n, li) for li in range(DEPTH) for n in REPLICATED] + [("final_w", None)]
    small_shape = lambda n, li: W[n].shape if li is None else W[n].shape[1:]
    small_rows = [-(-_size(small_shape(n, li)) // (8 * 128)) * 8 for n, li in small_names]
    loss_row = sum(small_rows)

    def small_pack(get, last=None):
        rows = []
        for (n, li), r in zip(small_names, small_rows):
            a = get(n, li).reshape(-1).astype(F32)
            rows.append(jnp.pad(a, (0, r * 128 - a.size)).reshape(r, 128))
        rows.append(jnp.zeros((8, 128), F32) if last is None else last)
        return jnp.concatenate(rows, axis=0)

    part = small_pack(lambda n, li: g_final if li is None else grs[li][n], loss_blk)
    wsm = small_pack(lambda n, li: W[n] if li is None else W[n][li])
    msm = small_pack(lambda n, li: M[n] if li is None else M[n][li])
    vsm = small_pack(lambda n, li: V[n] if li is None else V[n][li])
    sm = small_allreduce_adamw(part, wsm, msm, vsm)
    loss = sm[0][loss_row, 0]

    def small_unpack(buf):
        out, r0 = {}, 0
        for (n, li), r in zip(small_names, small_rows):
            shp = small_shape(n, li)
            out[(n, li)] = buf[r0:r0 + r].reshape(-1)[:_size(shp)].reshape(shp)
            r0 += r
        return out

    sm_g, sm_d, sm_m, sm_v = [small_unpack(b) for b in sm]

    def assemble(sharded_list, small):
        outs = []
        for n in ("ln1_w", "w_in", "attn_sink", "conv_w", "conv_b", "dt_bias", "a_log", "d_skip", "ssm_norm_w", "pool_w",
                  "pool_scale", "w_attn_br", "w_pool_br", "w_ssm_br", "w_out", "ln2_w", "w_gate_up", "w_down"):
            if n in SHARDED:
                outs.append(sharded_list[n])
            else:
                outs.append(jnp.stack([small[(n, li)] for li in range(DEPTH)]))
        outs.append(small[("final_w", None)])
        return outs

    return (loss, grad_x, *assemble(g_sh, sm_g), *assemble(d_sh, sm_d), *assemble(m_sh, sm_m), *assemble(v_sh, sm_v))
```

```python
import functools

import jax
import jax.numpy as jnp
from jax import lax
from jax.experimental import pallas as pl
from jax.experimental.pallas import tpu as pltpu

D_MODEL = 2048
DEPTH = 2
ATT_HEAD_DIM = 64
ATT_Q_HEADS = 16
ATT_KV_HEADS = 4
WINDOW = 128
POOL_WINDOWS = (2, 4, 8, 16)
POOL_WIDTH = D_MODEL // 2
D_INNER = D_MODEL
SSM_HEAD_DIM = 64
SSM_GROUPS = 4
D_STATE = 128
CONV_K = 4
CHUNK = 128
N_BRANCH = 3
D_FF = 5632
EPS = 1e-6

ADAM_LR = 0.001
ADAM_B1 = 0.9
ADAM_B2 = 0.999
ADAM_EPS = 1e-08
ADAM_WD = 0.01
ADAM_STEP = 10

N_DEV = 8
F32 = jnp.float32
BF16 = jnp.bfloat16
MXU_DTYPE = jnp.bfloat16
NEG = -1e30
VMEM_CAP = 60 * 2**20
CONV_HALO = 8
POOL_HALO = 16
MESH_T = pl.DeviceIdType.MESH


class Dims:
    def __init__(self):
        self.D = D_MODEL
        self.AW = ATT_Q_HEADS * ATT_HEAD_DIM
        self.KVW = ATT_KV_HEADS * ATT_HEAD_DIM
        self.GQ = ATT_Q_HEADS // ATT_KV_HEADS
        self.PW = POOL_WIDTH
        self.PG = len(POOL_WINDOWS)
        self.PC = POOL_WIDTH // len(POOL_WINDOWS)
        self.DI = D_INNER
        self.H = D_INNER // SSM_HEAD_DIM
        self.P = SSM_HEAD_DIM
        self.G = SSM_GROUPS
        self.HPG = self.H // SSM_GROUPS
        self.GW = D_INNER // SSM_GROUPS
        self.N = D_STATE
        self.GN = SSM_GROUPS * D_STATE
        self.CC = D_INNER + 2 * SSM_GROUPS * D_STATE
        self.DTP = -(-self.H // 256) * 256
        self.DFF = D_FF
        self.in_widths = (self.AW, self.KVW, self.KVW, self.PW, self.DI, self.CC, self.H, N_BRANCH * self.D)
        self.IN_COLS = sum(self.in_widths)
        self.QKV = self.AW + 2 * self.KVW
        self.seg = (self.QKV, self.PW, self.DI, self.CC, self.DTP, N_BRANCH * self.D)
        self.IN_INT = sum(self.seg)


def _tile(n, cap, mult=128):
    if n <= cap:
        return n
    best = 0
    for t in range(mult, cap + 1, mult):
        if n % t == 0:
            best = t
    assert best, (n, cap, mult)
    return best


def _nbytes(shape, dtype):
    n = 1
    for s in shape:
        n *= s
    return n * jnp.dtype(dtype).itemsize


def _params(est_bytes, sem=None):
    limit = int(min(VMEM_CAP, max(32 * 2**20, est_bytes * 3 // 2 + 8 * 2**20)))
    kw = dict(vmem_limit_bytes=limit)
    if sem is not None:
        kw["dimension_semantics"] = sem
    return pltpu.CompilerParams(**kw)


def _raw_dot(a, b, ca, cb):
    return lax.dot_general(a.astype(MXU_DTYPE), b.astype(MXU_DTYPE), (((ca,), (cb,)), ((), ())),
                           preferred_element_type=F32)


@functools.partial(jax.custom_vjp, nondiff_argnums=(2, 3))
def bdot(a, b, ca, cb):
    return _raw_dot(a, b, ca, cb)


def _bdot_fwd(a, b, ca, cb):
    return _raw_dot(a, b, ca, cb), (a, b)


def _bdot_bwd(ca, cb, res, g):
    a, b = res
    if (ca, cb) == (1, 0):
        da, db = bdot(g, b, 1, 1), bdot(a, g, 0, 0)
    elif (ca, cb) == (1, 1):
        da, db = bdot(g, b, 1, 0), bdot(g, a, 0, 0)
    else:
        assert (ca, cb) == (0, 0)
        da, db = bdot(b, g, 1, 1), bdot(a, g, 1, 0)
    return da.astype(a.dtype), db.astype(b.dtype)


bdot.defvjp(_bdot_fwd, _bdot_bwd)


def hdot(a, b, ca=1, cb=0):
    return lax.dot_general(a, b, (((ca,), (cb,)), ((), ())), precision=lax.Precision.HIGHEST,
                           preferred_element_type=F32)


def _silu(x):
    return x * jax.nn.sigmoid(x)


def _softplus(x):
    return jnp.maximum(x, 0.0) + jnp.log1p(jnp.exp(-jnp.abs(x)))


def matmul(name, a, b, mode, out_dtype, add=None, b_blocked=False, out_blocks=0, after=()):
    if b_blocked:
        nb, rows, n = b.shape
        b_rows, b_cols = rows, nb * n
    else:
        b_rows, b_cols = b.shape
    if mode == "nn":
        (M, K), (K2, N) = a.shape, (b_rows, b_cols)
    elif mode == "nt":
        (M, K), (N, K2) = a.shape, (b_rows, b_cols)
    else:
        (K, M), (K2, N) = a.shape, (b_rows, b_cols)
    assert K == K2, (name, a.shape, b.shape, mode)
    n_blk = n if b_blocked else (N // out_blocks if out_blocks else 0)
    tm = _tile(M, 1024)
    tn = _tile(n_blk if (n_blk and mode != "nt") else N, 1536)
    tk = _tile(n_blk if (n_blk and mode == "nt") else K, 2048)
    nk = K // tk
    dims = {"nn": (1, 0), "nt": (1, 1), "tn": (0, 0)}[mode]
    a_spec = (pl.BlockSpec((tk, tm), lambda i, j, k: (k, i)) if mode == "tn"
              else pl.BlockSpec((tm, tk), lambda i, j, k: (i, k)))
    if b_blocked and mode == "nt":
        per = n // tk
        b_spec = pl.BlockSpec((None, tn, tk), lambda i, j, k: (k // per, j, k % per))
    elif b_blocked:
        per = n // tn
        b_spec = pl.BlockSpec((None, tk, tn), lambda i, j, k: (j // per, k, j % per))
    elif mode == "nt":
        b_spec = pl.BlockSpec((tn, tk), lambda i, j, k: (j, k))
    else:
        b_spec = pl.BlockSpec((tk, tn), lambda i, j, k: (k, j))
    o_spec = pl.BlockSpec((tm, tn), lambda i, j, k: (i, j))
    out_shape = jax.ShapeDtypeStruct((M, N), out_dtype)
    if out_blocks:
        assert mode == "tn" and add is None
        per_o = n_blk // tn
        o_spec = pl.BlockSpec((None, tm, tn), lambda i, j, k: (j // per_o, i, j % per_o))
        out_shape = jax.ShapeDtypeStruct((out_blocks, M, n_blk), out_dtype)
    has_add = add is not None

    def body(*refs):
        a_ref, b_ref = refs[0], refs[1]
        c_ref = refs[2] if has_add else None
        o_ref = refs[2 + has_add + len(after)]
        p = _raw_dot(a_ref[...], b_ref[...], *dims)
        if nk == 1:
            if has_add:
                p = p + c_ref[...].astype(F32)
            o_ref[...] = p.astype(o_ref.dtype)
            return
        acc = refs[-1]
        k = pl.program_id(2)

        @pl.when(k == 0)
        def _():
            acc[...] = p + c_ref[...].astype(F32) if has_add else p

        @pl.when(k > 0)
        def _():
            acc[...] += p

        @pl.when(k == nk - 1)
        def _():
            o_ref[...] = acc[...].astype(o_ref.dtype)

    est = 2 * (_nbytes((tm, tk), a.dtype) + _nbytes((tk, tn), b.dtype) + _nbytes((tm, tn), out_dtype))
    est += 3 * _nbytes((tm, tn), F32) + _nbytes((tm, tk), MXU_DTYPE) + _nbytes((tk, tn), MXU_DTYPE)
    if has_add:
        est += 2 * _nbytes((tm, tn), add.dtype)
    args = (a, b) + ((add,) if has_add else ()) + tuple(after)
    in_specs = [a_spec, b_spec] + ([o_spec] if has_add else []) + [pl.BlockSpec(memory_space=pl.ANY)] * len(after)
    return pl.pallas_call(
        body, name=name, grid=(M // tm, N // tn, nk),
        in_specs=in_specs, out_specs=o_spec, out_shape=out_shape,
        scratch_shapes=[pltpu.VMEM((tm, tn), F32)] if nk > 1 else [],
        compiler_params=_params(est, ("parallel", "parallel", "arbitrary")),
    )(*args)


def row_call(name, f, rows, pars, outs, tm):
    L = rows[0].shape[0]
    tm = min(tm, L)
    nr, npar = len(rows), len(pars)

    def body(*refs):
        vals = [r[...] for r in refs[:nr + npar]]
        res = f(*vals)
        for o_ref, v in zip(refs[nr + npar:], res):
            o_ref[...] = v.astype(o_ref.dtype)

    est = 2 * sum(_nbytes((tm, a.shape[1]), a.dtype) for a in rows) + 2 * sum(_nbytes((tm, w), d) for w, d in outs)
    est += 4 * sum(_nbytes((tm, a.shape[1]), F32) for a in rows)
    res = pl.pallas_call(
        body, name=name, grid=(L // tm,),
        in_specs=[pl.BlockSpec((tm, a.shape[1]), lambda i: (i, 0)) for a in rows]
        + [pl.BlockSpec(p.shape, lambda i: (0, 0)) for p in pars],
        out_specs=[pl.BlockSpec((tm, w), lambda i: (i, 0)) for w, _ in outs],
        out_shape=[jax.ShapeDtypeStruct((L, w), d) for w, d in outs],
        compiler_params=_params(est, ("parallel",)),
    )(*rows, *pars)
    return res


def row_vjp_call(name, f, rows, pars, cots, row_grad_dtypes, par_grads, tm, adds=None):
    L = rows[0].shape[0]
    tm = min(tm, L)
    nr, npar, nc = len(rows), len(pars), len(cots)
    adds = adds or {}
    add_idx = sorted(adds)
    rg_idx = [i for i, d in enumerate(row_grad_dtypes) if d is not None]
    pg_idx = [i for i, w in enumerate(par_grads) if w]
    diff_idx = rg_idx + [nr + i for i in pg_idx]

    def body(*refs):
        vals = [r[...] for r in refs[:nr + npar]]
        cvals = [r[...] for r in refs[nr + npar:nr + npar + nc]]
        avals = [r[...] for r in refs[nr + npar + nc:nr + npar + nc + len(add_idx)]]
        out_refs = refs[nr + npar + nc + len(add_idx):]

        def g(*dv):
            full = list(vals)
            for i, v in zip(diff_idx, dv):
                full[i] = v
            return tuple(f(*full))

        res, vjp = jax.vjp(g, *[vals[i] for i in diff_idx])
        grads = vjp(tuple(c.astype(r.dtype) for c, r in zip(cvals, res)))
        for n, i in enumerate(rg_idx):
            gval = grads[n].astype(F32)
            if i in adds:
                gval = gval + avals[add_idx.index(i)].astype(F32)
            out_refs[n][...] = gval.astype(out_refs[n].dtype)
        first = pl.program_id(0) == 0
        for n, i in enumerate(pg_idx):
            o_ref = out_refs[len(rg_idx) + n]
            gval = grads[len(rg_idx) + n].astype(F32)

            @pl.when(first)
            def _(o_ref=o_ref, gval=gval):
                o_ref[...] = gval

            @pl.when(jnp.logical_not(first))
            def _(o_ref=o_ref, gval=gval):
                o_ref[...] += gval

    row_spec = lambda a: pl.BlockSpec((tm, a.shape[1]), lambda i: (i, 0))
    est = 2 * sum(_nbytes((tm, a.shape[1]), a.dtype) for a in list(rows) + list(cots))
    est += 10 * sum(_nbytes((tm, a.shape[1]), F32) for a in rows)
    res = pl.pallas_call(
        body, name=name, grid=(L // tm,),
        in_specs=[row_spec(a) for a in rows] + [pl.BlockSpec(p.shape, lambda i: (0, 0)) for p in pars]
        + [row_spec(c) for c in cots] + [row_spec(adds[i]) for i in add_idx],
        out_specs=[row_spec(rows[i]) for i in rg_idx] + [pl.BlockSpec(pars[i].shape, lambda i_: (0, 0)) for i in pg_idx],
        out_shape=[jax.ShapeDtypeStruct(rows[i].shape, row_grad_dtypes[i]) for i in rg_idx]
        + [jax.ShapeDtypeStruct(pars[i].shape, F32) for i in pg_idx],
        compiler_params=_params(est, ("arbitrary",)),
    )(*rows, *pars, *cots, *[adds[i] for i in add_idx])
    return list(res[:len(rg_idx)]), list(res[len(rg_idx):])


def _halo_specs(rows, T, nt, rev):
    specs = []
    for (_, w, ci, hs) in rows:
        if rev:
            specs.append(pl.BlockSpec((T, w), lambda j, ci=ci: (nt - 1 - j, ci)))
        else:
            specs.append(pl.BlockSpec((T, w), lambda i, ci=ci: (i, ci)))
        if hs:
            r = T // hs
            if rev:
                specs.append(pl.BlockSpec((hs, w), lambda j, ci=ci, r=r: (jnp.maximum((nt - 1 - j) * r - 1, 0), ci)))
            else:
                specs.append(pl.BlockSpec((hs, w), lambda i, ci=ci, r=r: (jnp.maximum(i * r - 1, 0), ci)))
    return specs


def _halo_args(rows):
    args = []
    for (a, _, _, hs) in rows:
        args.append(a)
        if hs:
            args.append(a)
    return args


def _halo_vals(rows, refs):
    vals, n = [], 0
    for (_, _, _, hs) in rows:
        if hs:
            vals.append((refs[n + 1][...], refs[n][...]))
            n += 2
        else:
            vals.append(refs[n][...])
            n += 1
    return vals, n


def halo_call(name, f, rows, pars, outs, T):
    L = rows[0][0].shape[0]
    T = min(T, L)
    nt = L // T
    npar = len(pars)

    def body(*refs):
        i = pl.program_id(0)
        vals, n = _halo_vals(rows, refs)
        pv = [r[...] for r in refs[n:n + npar]]
        res = f(i == 0, i * T, *vals, *pv)
        for o_ref, v in zip(refs[n + npar:], res):
            o_ref[...] = v.astype(o_ref.dtype)

    est = 2 * sum(_nbytes((T, w), a.dtype) for a, w, _, _ in rows) + 2 * sum(_nbytes((T, w), d) for w, d in outs)
    est += 8 * sum(_nbytes((T, w), F32) for _, w, _, _ in rows)
    return pl.pallas_call(
        body, name=name, grid=(nt,),
        in_specs=_halo_specs(rows, T, nt, False) + [pl.BlockSpec(p.shape, lambda i, nd=p.ndim: (0,) * nd) for p in pars],
        out_specs=[pl.BlockSpec((T, w), lambda i: (i, 0)) for w, _ in outs],
        out_shape=[jax.ShapeDtypeStruct((L, w), d) for w, d in outs],
        compiler_params=_params(est, ("parallel",)),
    )(*_halo_args(rows), *pars)


def halo_vjp_call(name, f, rows, pars, cots, row_grad_dtypes, par_grads, T):
    L = rows[0][0].shape[0]
    T = min(T, L)
    nt = L // T
    nr, npar, nc = len(rows), len(pars), len(cots)
    pg_idx = [i for i, w in enumerate(par_grads) if w]
    halo_idx = [i for i, r in enumerate(rows) if r[3]]

    def body(*refs):
        j = pl.program_id(0)
        i = nt - 1 - j
        vals, n = _halo_vals(rows, refs)
        pv = [r[...] for r in refs[n:n + npar]]
        cv = [r[...] for r in refs[n + npar:n + npar + nc]]
        out_refs = refs[n + npar + nc:n + npar + nc + nr + len(pg_idx)]
        carries = refs[n + npar + nc + nr + len(pg_idx):]
        first = i == 0

        def g(vals_, pv_):
            return tuple(f(first, i * T, *vals_, *pv_))

        res, vjp = jax.vjp(g, vals, pv)
        dvals, dpv = vjp(tuple(c.astype(r.dtype) for c, r in zip(cv, res)))

        @pl.when(j == 0)
        def _():
            for c_ref in carries:
                c_ref[...] = jnp.zeros_like(c_ref)

        for k in range(nr):
            hs = rows[k][3]
            o_ref = out_refs[k]
            if hs:
                dh, dc = dvals[k]
                c_ref = carries[halo_idx.index(k)]
                dc = dc.astype(F32)
                if hs == T:
                    o_ref[...] = (dc + c_ref[...]).astype(o_ref.dtype)
                else:
                    o_ref[0:T - hs, :] = dc[0:T - hs].astype(o_ref.dtype)
                    o_ref[T - hs:T, :] = (dc[T - hs:T] + c_ref[...]).astype(o_ref.dtype)
                c_ref[...] = dh.astype(F32)
            else:
                o_ref[...] = dvals[k].astype(o_ref.dtype)
        for m, k in enumerate(pg_idx):
            o_ref = out_refs[nr + m]
            gval = dpv[k].astype(F32)

            @pl.when(j == 0)
            def _(o_ref=o_ref, gval=gval):
                o_ref[...] = gval

            @pl.when(j > 0)
            def _(o_ref=o_ref, gval=gval):
                o_ref[...] += gval

    est = 2 * sum(_nbytes((T, w), a.dtype) for a, w, _, _ in rows) + 2 * sum(_nbytes((T, c.shape[1]), c.dtype) for c in cots)
    est += 12 * sum(_nbytes((T, w), F32) for _, w, _, _ in rows)
    res = pl.pallas_call(
        body, name=name, grid=(nt,),
        in_specs=_halo_specs(rows, T, nt, True) + [pl.BlockSpec(p.shape, lambda j, nd=p.ndim: (0,) * nd) for p in pars]
        + [pl.BlockSpec((T, c.shape[1]), lambda j: (nt - 1 - j, 0)) for c in cots],
        out_specs=[pl.BlockSpec((T, w), lambda j: (nt - 1 - j, 0)) for _, w, _, _ in rows]
        + [pl.BlockSpec(pars[k].shape, lambda j, nd=pars[k].ndim: (0,) * nd) for k in pg_idx],
        out_shape=[jax.ShapeDtypeStruct((L, w), row_grad_dtypes[k]) for k, (_, w, _, _) in enumerate(rows)]
        + [jax.ShapeDtypeStruct(pars[k].shape, F32) for k in pg_idx],
        scratch_shapes=[pltpu.VMEM((rows[k][3], rows[k][1]), F32) for k in halo_idx],
        compiler_params=_params(est, ("arbitrary",)),
    )(*_halo_args(rows), *pars, *cots)
    return list(res[:nr]), list(res[nr:])


def f_rms(x, w):
    x = x.astype(F32)
    return (x * lax.rsqrt(jnp.mean(x * x, axis=-1, keepdims=True) + EPS) * w,)


def f_swiglu(gu):
    dff = gu.shape[1] // 2
    gu = gu.astype(F32)
    return (_silu(gu[:, :dff]) * gu[:, dff:],)


def f_merge(gl, a, p, s):
    d = a.shape[1]
    g = jax.nn.sigmoid(gl.astype(F32))
    return (g[:, :d] * a.astype(F32) + g[:, d:2 * d] * p.astype(F32) + g[:, 2 * d:] * s.astype(F32),)


def f_gnorm(y, z, nw):
    dm = Dims()
    g = y * _silu(z.astype(F32))
    outs = []
    for gi in range(dm.G):
        gg = g[:, gi * dm.GW:(gi + 1) * dm.GW]
        outs.append(gg * lax.rsqrt(jnp.mean(gg * gg, axis=-1, keepdims=True) + EPS))
    return (jnp.concatenate(outs, axis=1) * nw,)


def f_conv(first, row0, xs, w, b):
    halo, cur = xs
    halo = jnp.where(first, 0.0, halo)
    ext = jnp.concatenate([halo, cur], axis=0)
    T = cur.shape[0]
    base = CONV_HALO - (CONV_K - 1)
    pre = b
    for k in range(CONV_K):
        pre = pre + w[k:k + 1, :] * ext[base + k:base + k + T]
    return (_silu(pre),)


def f_pool(first, row0, us, pw, scale):
    dm = Dims()
    halo, cur = us
    halo = jnp.where(first, 0.0, halo)
    ext = jnp.concatenate([halo, cur], axis=0)
    T = cur.shape[0]
    t = row0 + lax.broadcasted_iota(jnp.int32, (T, 1), 0)
    outs = []
    for gi, w in enumerate(POOL_WINDOWS):
        assert w & (w - 1) == 0 and w <= POOL_HALO
        s = ext[:, gi * dm.PC:(gi + 1) * dm.PC]
        sh = 1
        while sh < w:
            s = s + jnp.concatenate([jnp.zeros((sh, dm.PC), F32), s[:-sh]], axis=0)
            sh *= 2
        cnt = jnp.minimum(t + 1, w).astype(F32)
        mixed = s[POOL_HALO:] / cnt - cur[:, gi * dm.PC:(gi + 1) * dm.PC]
        outs.append(bdot(mixed, pw[gi], 1, 0))
    return (jnp.concatenate(outs, axis=1) * scale,)


def f_attn(first, row0, q, ks, vs, sink):
    dm = Dims()
    kp, kc = ks
    vp, vc = vs
    T = q.shape[0]
    hd = ATT_HEAD_DIM
    qi = lax.broadcasted_iota(jnp.int32, (T, 2 * T), 0)
    si = lax.broadcasted_iota(jnp.int32, (T, 2 * T), 1)
    diff = qi + T - si
    ok = (diff >= 0) & (diff < WINDOW) & ((si >= T) | jnp.logical_not(first))
    mask = jnp.concatenate([ok] * dm.GQ, axis=0)
    outs = []
    for k in range(ATT_KV_HEADS):
        kk = jnp.concatenate([kp[:, k * hd:(k + 1) * hd], kc[:, k * hd:(k + 1) * hd]], axis=0)
        vv = jnp.concatenate([vp[:, k * hd:(k + 1) * hd], vc[:, k * hd:(k + 1) * hd]], axis=0)
        heads = [k * dm.GQ + g for g in range(dm.GQ)]
        qs = jnp.concatenate([q[:, h * hd:(h + 1) * hd] for h in heads], axis=0)
        s = bdot(qs, kk, 1, 1) * (hd ** -0.5)
        s = jnp.where(mask, s, NEG)
        sk = jnp.concatenate([jnp.broadcast_to(sink[:, h:h + 1], (T, 1)) for h in heads], axis=0)
        m = jnp.maximum(jnp.max(s, axis=-1, keepdims=True), sk)
        p = jnp.exp(s - m)
        den = jnp.sum(p, axis=-1, keepdims=True) + jnp.exp(sk - m)
        o = bdot(p / den, vv, 1, 0)
        outs += [o[g * T:(g + 1) * T] for g in range(dm.GQ)]
    return (jnp.concatenate(outs, axis=1),)


def f_ssd(xc, dtr, ht, dt_bias, a_log, d_skip):
    dm = Dims()
    Q = xc.shape[0]
    xs = xc[:, :dm.DI]
    bm = xc[:, dm.DI:dm.DI + dm.GN]
    cm = xc[:, dm.DI + dm.GN:]
    expand = (lax.broadcasted_iota(jnp.int32, (dm.H, dm.DI), 1) // dm.P
              == lax.broadcasted_iota(jnp.int32, (dm.H, dm.DI), 0)).astype(F32)
    ri = lax.broadcasted_iota(jnp.int32, (Q, Q), 0)
    ci = lax.broadcasted_iota(jnp.int32, (Q, Q), 1)
    causal = ri >= ci
    tinc = causal.astype(F32)
    dt = _softplus(dtr[:, :dm.H] + dt_bias)
    da = dt * (-jnp.exp(a_log))
    acs = hdot(tinc, da)
    acs_t = hdot(da, tinc, 0, 1)
    eacs = jnp.exp(acs)
    dend = jnp.exp(acs[Q - 1:Q, :] - acs)
    ex = hdot(jnp.concatenate([dt, eacs, dend, jnp.broadcast_to(d_skip, (8, dm.H))], axis=0), expand)
    dt_x, eacs_x, dend_x, dsk_x = ex[:Q], ex[Q:2 * Q], ex[2 * Q:3 * Q], ex[3 * Q:3 * Q + 1]
    xdt = xs * dt_x
    ys, hts = [], []
    for g in range(dm.G):
        gs = slice(g * dm.GW, (g + 1) * dm.GW)
        bg = bm[:, g * dm.N:(g + 1) * dm.N]
        cg = cm[:, g * dm.N:(g + 1) * dm.N]
        cb = bdot(cg, bg, 1, 1)
        y_off = bdot(cg, ht[:, gs], 1, 0) * eacs_x[:, gs]
        xg = xdt[:, gs]
        st = bdot(bg, xg * dend_x[:, gs], 0, 0)
        hts.append(ht[:, gs] * eacs_x[Q - 1:Q, gs] + st)
        yd = []
        for e in range(dm.HPG):
            h = g * dm.HPG + e
            seg = acs[:, h:h + 1] - acs_t[h:h + 1, :]
            lm = jnp.exp(jnp.where(causal, seg, NEG))
            yd.append(bdot(cb * lm, xg[:, e * dm.P:(e + 1) * dm.P], 1, 0))
        ys.append(jnp.concatenate(yd, axis=1) + y_off)
    y = jnp.concatenate(ys, axis=1) + dsk_x * xs
    return y, jnp.concatenate(hts, axis=1)


def ssd_fwd(name, xc, dtr, dt_bias, a_log, d_skip):
    dm = Dims()
    L = xc.shape[0]
    Q = CHUNK
    nc = L // Q

    def body(xc_ref, dtr_ref, b_ref, a_ref, s_ref, y_ref, hts_ref, ht):
        @pl.when(pl.program_id(0) == 0)
        def _():
            ht[...] = jnp.zeros_like(ht)

        h0 = ht[...]
        hts_ref[0] = h0
        y, h1 = f_ssd(xc_ref[...], dtr_ref[...], h0, b_ref[...], a_ref[...], s_ref[...])
        y_ref[...] = y
        ht[...] = h1

    par = pl.BlockSpec((1, dm.H), lambda c: (0, 0))
    est = 40 * _nbytes((Q, dm.CC), F32) + 4 * _nbytes((dm.N, dm.DI), F32)
    return pl.pallas_call(
        body, name=name, grid=(nc,),
        in_specs=[pl.BlockSpec((Q, dm.CC), lambda c: (c, 0)), pl.BlockSpec((Q, dm.DTP), lambda c: (c, 0)), par, par, par],
        out_specs=[pl.BlockSpec((Q, dm.DI), lambda c: (c, 0)), pl.BlockSpec((1, dm.N, dm.DI), lambda c: (c, 0, 0))],
        out_shape=[jax.ShapeDtypeStruct((L, dm.DI), F32), jax.ShapeDtypeStruct((nc, dm.N, dm.DI), F32)],
        scratch_shapes=[pltpu.VMEM((dm.N, dm.DI), F32)],
        compiler_params=_params(est, ("arbitrary",)),
    )(xc, dtr, dt_bias, a_log, d_skip)


def ssd_bwd(name, xc, dtr, hts, dt_bias, a_log, d_skip, dy):
    dm = Dims()
    L = xc.shape[0]
    Q = CHUNK
    nc = L // Q

    def body(xc_ref, dtr_ref, hts_ref, b_ref, a_ref, s_ref, dy_ref, dxc_ref, ddtr_ref, db_ref, da_ref, ds_ref, dht):
        j = pl.program_id(0)

        @pl.when(j == 0)
        def _():
            dht[...] = jnp.zeros_like(dht)

        _, vjp = jax.vjp(f_ssd, xc_ref[...], dtr_ref[...], hts_ref[0], b_ref[...], a_ref[...], s_ref[...])
        dxc, ddtr, dh0, db, da, ds = vjp((dy_ref[...], dht[...]))
        dxc_ref[...] = dxc.astype(dxc_ref.dtype)
        ddtr_ref[...] = ddtr.astype(ddtr_ref.dtype)
        dht[...] = dh0
        for o_ref, gval in ((db_ref, db), (da_ref, da), (ds_ref, ds)):
            @pl.when(j == 0)
            def _(o_ref=o_ref, gval=gval):
                o_ref[...] = gval

            @pl.when(j > 0)
            def _(o_ref=o_ref, gval=gval):
                o_ref[...] += gval

    par = pl.BlockSpec((1, dm.H), lambda j: (0, 0))
    rev = lambda w: pl.BlockSpec((Q, w), lambda j: (nc - 1 - j, 0))
    est = 80 * _nbytes((Q, dm.CC), F32) + 6 * _nbytes((dm.N, dm.DI), F32)
    return pl.pallas_call(
        body, name=name, grid=(nc,),
        in_specs=[rev(dm.CC), rev(dm.DTP), pl.BlockSpec((1, dm.N, dm.DI), lambda j: (nc - 1 - j, 0, 0)), par, par, par, rev(dm.DI)],
        out_specs=[rev(dm.CC), rev(dm.DTP), par, par, par],
        out_shape=[jax.ShapeDtypeStruct((L, dm.CC), F32), jax.ShapeDtypeStruct((L, dm.DTP), BF16)]
        + [jax.ShapeDtypeStruct((1, dm.H), F32)] * 3,
        scratch_shapes=[pltpu.VMEM((dm.N, dm.DI), F32)],
        compiler_params=_params(est, ("arbitrary",)),
    )(xc, dtr, hts, dt_bias, a_log, d_skip, dy)


def loss_head(x, w, target, tm):
    L, D = x.shape
    tm = min(tm, L)

    def tile_loss(xv, wv, tv):
        (y,) = f_rms(xv, wv)
        return 0.5 * jnp.sum(jnp.mean(jnp.square(y - tv), axis=-1))

    def body(x_ref, w_ref, t_ref, dx_ref, dw_ref, loss_ref):
        val, (dx, dw) = jax.value_and_grad(tile_loss, argnums=(0, 1))(x_ref[...], w_ref[...], t_ref[...])
        dx_ref[...] = dx
        first = pl.program_id(0) == 0
        lv = jnp.full((8, 128), val, F32)

        @pl.when(first)
        def _():
            dw_ref[...] = dw
            loss_ref[...] = lv

        @pl.when(jnp.logical_not(first))
        def _():
            dw_ref[...] += dw
            loss_ref[...] += lv

    row = pl.BlockSpec((tm, D), lambda i: (i, 0))
    est = 16 * _nbytes((tm, D), F32)
    return pl.pallas_call(
        body, name="loss_head", grid=(L // tm,),
        in_specs=[row, pl.BlockSpec((1, D), lambda i: (0, 0)), row],
        out_specs=[row, pl.BlockSpec((1, D), lambda i: (0, 0)), pl.BlockSpec((8, 128), lambda i: (0, 0))],
        out_shape=[jax.ShapeDtypeStruct((L, D), F32), jax.ShapeDtypeStruct((1, D), F32), jax.ShapeDtypeStruct((8, 128), F32)],
        compiler_params=_params(est, ("arbitrary",)),
    )(x, w, target)


def _adamw(w, g, m, v):
    m = ADAM_B1 * m + (1.0 - ADAM_B1) * g
    v = ADAM_B2 * v + (1.0 - ADAM_B2) * jnp.square(g)
    m_hat = m / (1.0 - ADAM_B1 ** ADAM_STEP)
    v_hat = v / (1.0 - ADAM_B2 ** ADAM_STEP)
    delta = -ADAM_LR * (m_hat / (jnp.sqrt(v_hat) + ADAM_EPS) + ADAM_WD * w)
    return delta, m, v


def adamw_sharded(name, parts, w, m, v, li, carried):
    depth, R, C = w.shape
    tr = _tile(R, 128, 16)
    n_in = 4 + (4 if carried else 0)

    def body(*refs):
        p_ref, w_ref, m_ref, v_ref = refs[:4]
        g_ref, d_ref, nm_ref, nv_ref = refs[n_in:n_in + 4]
        g = p_ref[0].astype(F32)
        for s in range(1, N_DEV):
            g = g + p_ref[s].astype(F32)
        d, nm, nv = _adamw(w_ref[...], g, m_ref[...], v_ref[...])
        g_ref[...] = g
        d_ref[...] = d
        nm_ref[...] = nm
        nv_ref[...] = nv

    row = pl.BlockSpec((None, tr, C), lambda i: (li, i, 0))
    est = 2 * _nbytes((N_DEV, tr, C), parts.dtype) + 20 * _nbytes((tr, C), F32)
    return pl.pallas_call(
        body, name=name, grid=(R // tr,),
        in_specs=[pl.BlockSpec((N_DEV, tr, C), lambda i: (0, i, 0)), row, row, row]
        + ([pl.BlockSpec(memory_space=pl.ANY)] * 4 if carried else []),
        out_specs=[row] * 4,
        out_shape=[jax.ShapeDtypeStruct((depth, R, C), F32)] * 4,
        input_output_aliases={4 + k: k for k in range(4)} if carried else {},
        compiler_params=_params(est, ("parallel",)),
    )(parts, w, m, v, *(carried or ()))


def _my_place():
    return lax.axis_index("x"), lax.axis_index("y"), lax.axis_index("c")


def all_gather_hbm(name, shard):
    R, W = shard.shape

    def body(x_ref, out_ref, send_sems, recv_sems, local_sem):
        x, y, c = _my_place()
        me, sibling = (x, y, c), (x, y, 1 - c)
        chips = [(1 - x, y), (x, 1 - y), (1 - x, 1 - y)]

        def slot(px, py, pc):
            return out_ref.at[4 * px + 2 * py + pc]

        def copy(k, block, to, src=None):
            return pltpu.make_async_remote_copy(
                src_ref=slot(*block) if src is None else src, dst_ref=slot(*block),
                send_sem=send_sems.at[k], recv_sem=recv_sems.at[k], device_id=to, device_id_type=MESH_T)

        mine = pltpu.make_async_copy(x_ref, slot(*me), local_sem)
        mine.start()
        first = [copy(0, me, sibling, src=x_ref)]
        first += [copy(1 + j, me, (*chip, c), src=x_ref) for j, chip in enumerate(chips)]
        for cp in first:
            cp.start()
        passed = [copy(4 + j, (*chip, c), sibling) for j, chip in enumerate(chips)]
        for j, chip in enumerate(chips):
            copy(1 + j, (*chip, c), me).wait_recv()
            passed[j].start()
        copy(0, sibling, me).wait_recv()
        for j, chip in enumerate(chips):
            copy(4 + j, (*chip, 1 - c), me).wait_recv()
        for cp in first + passed:
            cp.wait_send()
        mine.wait()

    return pl.pallas_call(
        body, name=name,
        out_shape=jax.ShapeDtypeStruct((N_DEV, R, W), shard.dtype),
        in_specs=[pl.BlockSpec(memory_space=pl.ANY)],
        out_specs=pl.BlockSpec(memory_space=pl.ANY),
        scratch_shapes=[pltpu.SemaphoreType.DMA((7,)), pltpu.SemaphoreType.DMA((7,)), pltpu.SemaphoreType.DMA],
    )(shard)


HBM_SPEC = pl.BlockSpec(memory_space=pltpu.HBM)
SEM_SPEC = pl.BlockSpec(memory_space=pltpu.SEMAPHORE)
DATAFLOW = pltpu.SideEffectType.DATAFLOW_SIDE_EFFECTING


def _me():
    x, y, c = _my_place()
    return 4 * x + 2 * y + c


def copies_start(name, src, gather, after=()):
    blk = src.shape if gather else src.shape[1:]
    mine = src[None] if gather else lax.dynamic_slice_in_dim(src, _me(), 1, axis=0)
    land = lax.dynamic_update_slice(lax.empty((N_DEV,) + tuple(blk), src.dtype), mine, (_me(), 0, 0))
    n_after = len(after)

    def body(*refs):
        src_ref, land_ref = refs[0], refs[1]
        send_sem, recv_sem = refs[2 + n_after], refs[3 + n_after]
        token = refs[-1]
        x, y, c = _my_place()
        me = 4 * x + 2 * y + c
        for k in range(1, N_DEV):
            px, py, pc = (x + (k >> 2)) % 2, (y + ((k >> 1) & 1)) % 2, (c + (k & 1)) % 2
            pltpu.make_async_remote_copy(
                src_ref=src_ref if gather else src_ref.at[4 * px + 2 * py + pc], dst_ref=land_ref.at[me],
                send_sem=send_sem, recv_sem=recv_sem, device_id=(px, py, pc), device_id_type=MESH_T).start()
        token[...] = jnp.zeros_like(token)

    return pl.pallas_call(
        body, name=name,
        out_shape=(pltpu.SemaphoreType.DMA(()), pltpu.SemaphoreType.DMA(()), pltpu.HBM(src.shape, src.dtype),
                   pltpu.HBM(land.shape, land.dtype), jax.ShapeDtypeStruct((8, 128), F32)),
        in_specs=(HBM_SPEC, HBM_SPEC) + (pl.BlockSpec(memory_space=pl.ANY),) * n_after,
        out_specs=(SEM_SPEC, SEM_SPEC, HBM_SPEC, HBM_SPEC, pl.BlockSpec(memory_space=pltpu.VMEM)),
        input_output_aliases={0: 2, 1: 3},
        compiler_params=pltpu.CompilerParams(has_side_effects=DATAFLOW),
    )(pltpu.with_memory_space_constraint(src, pltpu.HBM), pltpu.with_memory_space_constraint(land, pltpu.HBM), *after)


def copies_wait(name, started, after):
    send_sem, recv_sem, src_thru, land_thru, _ = started

    def body(src_ref, land_ref, send_sem, recv_sem, after_ref, src_dead, got_ref):
        seven = land_ref.at[pl.ds(0, N_DEV - 1)]
        all_seven = pltpu.make_async_remote_copy(src_ref=seven, dst_ref=seven, send_sem=send_sem, recv_sem=recv_sem,
                                                 device_id=_my_place(), device_id_type=MESH_T)
        all_seven.wait_send()
        all_seven.wait_recv()

    return pl.pallas_call(
        body, name=name,
        out_shape=(pltpu.HBM(src_thru.shape, src_thru.dtype), pltpu.HBM(land_thru.shape, land_thru.dtype)),
        in_specs=(HBM_SPEC, HBM_SPEC, SEM_SPEC, SEM_SPEC, pl.BlockSpec(memory_space=pl.ANY)),
        out_specs=(HBM_SPEC, HBM_SPEC), input_output_aliases={0: 0, 1: 1},
        compiler_params=pltpu.CompilerParams(has_side_effects=DATAFLOW),
    )(src_thru, land_thru, send_sem, recv_sem, after)[1]


def small_allreduce_adamw(part, w, m, v):
    R, W = part.shape

    def body(x_ref, w_ref, m_ref, v_ref, g_ref, d_ref, nm_ref, nv_ref, all_ref, send_sems, recv_sems, local_sem):
        x, y, c = _my_place()
        me, sibling = (x, y, c), (x, y, 1 - c)
        chips = [(1 - x, y), (x, 1 - y), (1 - x, 1 - y)]

        def slot(px, py, pc):
            return all_ref.at[4 * px + 2 * py + pc]

        def copy(k, block, to, src=None):
            return pltpu.make_async_remote_copy(
                src_ref=slot(*block) if src is None else src, dst_ref=slot(*block),
                send_sem=send_sems.at[k], recv_sem=recv_sems.at[k], device_id=to, device_id_type=MESH_T)

        mine = pltpu.make_async_copy(x_ref, slot(*me), local_sem)
        mine.start()
        first = [copy(0, me, sibling, src=x_ref)]
        first += [copy(1 + j, me, (*chip, c), src=x_ref) for j, chip in enumerate(chips)]
        for cp in first:
            cp.start()
        passed = [copy(4 + j, (*chip, c), sibling) for j, chip in enumerate(chips)]
        for j, chip in enumerate(chips):
            copy(1 + j, (*chip, c), me).wait_recv()
            passed[j].start()
        copy(0, sibling, me).wait_recv()
        for j, chip in enumerate(chips):
            copy(4 + j, (*chip, 1 - c), me).wait_recv()
        for cp in first + passed:
            cp.wait_send()
        mine.wait()
        g = all_ref[0]
        for s in range(1, N_DEV):
            g = g + all_ref[s]
        d, nm, nv = _adamw(w_ref[...], g, m_ref[...], v_ref[...])
        g_ref[...] = g
        d_ref[...] = d
        nm_ref[...] = nm
        nv_ref[...] = nv

    vm = pl.BlockSpec(memory_space=pltpu.VMEM)
    return pl.pallas_call(
        body, name="small_allreduce_adamw",
        out_shape=[jax.ShapeDtypeStruct((R, W), F32)] * 4,
        in_specs=[vm] * 4, out_specs=[vm] * 4,
        scratch_shapes=[pltpu.VMEM((N_DEV, R, W), F32), pltpu.SemaphoreType.DMA((7,)), pltpu.SemaphoreType.DMA((7,)),
                        pltpu.SemaphoreType.DMA],
    )(part, w, m, v)


SHARDED = ("w_in", "conv_w", "pool_w", "w_attn_br", "w_pool_br", "w_ssm_br", "w_out", "w_gate_up", "w_down")
REPLICATED = ("ln1_w", "attn_sink", "conv_b", "dt_bias", "a_log", "d_skip", "ssm_norm_w", "pool_scale", "ln2_w")
CONV_ROWS = 8


class Pending:
    def __init__(self, name, started, layout):
        self.name, self.started, self.layout = name, started, layout


def need(fw, n, after):
    if n not in fw:
        n_src = "w_in"
        fw.update(fw[n_src].layout(copies_wait(fw[n_src].name, fw[n_src].started, after)))
        del fw[n_src]
    elif isinstance(fw[n], Pending):
        fw[n] = fw[n].layout(copies_wait(fw[n].name, fw[n].started, after))
    return fw[n]


def _w_in_layout(dm, g_in):
    win = jnp.concatenate([g_in[d] for d in range(N_DEV)], axis=1)
    pts, acc = [], 0
    for wd in dm.in_widths:
        pts.append((acc, acc + wd))
        acc += wd
    cols = lambda k: win[:, pts[k][0]:pts[k][1]]
    fw = {"w_qkv": win[:, :pts[2][1]], "w_u": cols(3), "w_z": cols(4), "w_xbc": cols(5), "w_gl": cols(7)}
    fw["w_dt"] = jnp.pad(cols(6), ((0, 0), (0, dm.DTP - dm.H)))
    fw["w_in_int"] = jnp.concatenate([win[:, :pts[5][1]], fw["w_dt"], fw["w_gl"]], axis=1)
    return fw


def _gather_weights(dm, W, li, w_in_now, tokens):
    tag = f"ag_l{li}_"
    fw = {}

    def start(n, shard, layout):
        st = copies_start(tag + n + "_start", shard, True, after=tokens[-1:])
        tokens.append(st[4])
        fw[n] = Pending(tag + n + "_wait", st, layout)

    if w_in_now:
        g_in = all_gather_hbm(tag + "w_in", W["w_in"][li].astype(BF16))
        tokens.append(g_in)
        fw.update(_w_in_layout(dm, g_in))
    else:
        start("w_in", W["w_in"][li].astype(BF16), functools.partial(_w_in_layout, dm))
    start("pool_w", W["pool_w"][li].astype(BF16).reshape(dm.PG * dm.PC // N_DEV, dm.PC),
          lambda g: g.reshape(N_DEV, dm.PG, dm.PC // N_DEV, dm.PC).transpose(1, 0, 2, 3)
          .reshape(dm.PG, dm.PC, dm.PC).astype(F32))
    start("conv_w", jnp.pad(W["conv_w"][li], ((0, CONV_ROWS - CONV_K), (0, 0))),
          lambda g: g[:, :CONV_K].transpose(1, 0, 2).reshape(CONV_K, dm.CC))
    for n in ("w_attn_br", "w_pool_br", "w_ssm_br", "w_out", "w_gate_up", "w_down"):
        if n in ("w_ssm_br", "w_out", "w_down"):
            layout = lambda g: g.reshape(g.shape[0] * g.shape[1], g.shape[2])
        else:
            layout = lambda g: g
        start(n, W[n][li].astype(BF16), layout)
    return fw


def _grad_blocks(dm, n, g):
    if n == "w_in":
        o = [0]
        for wd in dm.seg:
            o.append(o[-1] + wd)
        ref_cols = jnp.concatenate([g[:, :o[4]], g[:, o[4]:o[4] + dm.H], g[:, o[5]:]], axis=1)
        per = dm.IN_COLS // N_DEV
        return jnp.stack([ref_cols[:, d * per:(d + 1) * per] for d in range(N_DEV)])
    if n in ("w_attn_br", "w_pool_br", "w_gate_up"):
        return g
    if n in ("w_ssm_br", "w_out", "w_down"):
        return g.reshape(N_DEV, g.shape[0] // N_DEV, g.shape[1])
    if n == "pool_w":
        g = g.reshape(dm.PG, N_DEV, dm.PC // N_DEV, dm.PC).transpose(1, 0, 2, 3)
        return g.reshape(N_DEV, dm.PG * dm.PC // N_DEV, dm.PC).astype(BF16)
    assert n == "conv_w"
    g = g.reshape(CONV_K, N_DEV, dm.CC // N_DEV).transpose(1, 0, 2)
    return jnp.pad(g, ((0, 0), (0, CONV_ROWS - CONV_K), (0, 0)))


def _as_rows(n, a):
    if n == "pool_w":
        return a.reshape(a.shape[0], a.shape[1] * a.shape[2], a.shape[3])
    if n == "conv_w":
        return jnp.pad(a, ((0, 0), (0, CONV_ROWS - CONV_K), (0, 0)))
    return a


def _from_rows(n, a, like):
    if n == "pool_w":
        return a.reshape(like.shape)
    if n == "conv_w":
        return a[:, :CONV_K]
    return a


def _size(shape):
    n = 1
    for s in shape:
        n *= s
    return n


def _layer_forward(dm, x, wts, rep, li):
    tag = f"l{li}_"
    sv = {"x": x}
    (h,) = row_call(tag + "rms1", f_rms, [x], [rep["ln1_w"]], [(dm.D, BF16)], 256)
    sv["h"] = h
    qkv = matmul(tag + "p_qkv", h, need(wts, "w_qkv", x), "nn", BF16)
    u = matmul(tag + "p_u", h, wts["w_u"], "nn", F32)
    z = matmul(tag + "p_z", h, wts["w_z"], "nn", F32)
    xbc = matmul(tag + "p_xbc", h, wts["w_xbc"], "nn", F32)
    dtr = matmul(tag + "p_dt", h, wts["w_dt"], "nn", F32)
    gl = matmul(tag + "p_gl", h, wts["w_gl"], "nn", BF16)
    sv.update(qkv=qkv, u=u, z=z, xbc=xbc, dtr=dtr, gl=gl)
    kvi = dm.AW // dm.KVW
    (att,) = halo_call(tag + "attn", f_attn,
                       [(qkv, dm.AW, 0, 0), (qkv, dm.KVW, kvi, WINDOW), (qkv, dm.KVW, kvi + 1, WINDOW)],
                       [rep["attn_sink"]], [(dm.AW, BF16)], WINDOW)
    (pool,) = halo_call(tag + "pool", f_pool, [(u, dm.PW, 0, POOL_HALO)],
                        [need(wts, "pool_w", att), rep["pool_scale"]], [(dm.PW, BF16)], 256)
    (xc,) = halo_call(tag + "conv", f_conv, [(xbc, dm.CC, 0, CONV_HALO)],
                      [need(wts, "conv_w", pool), rep["conv_b"]], [(dm.CC, F32)], 256)
    y, hts = ssd_fwd(tag + "ssd", xc, dtr, rep["dt_bias"], rep["a_log"], rep["d_skip"])
    (ssm,) = row_call(tag + "gnorm", f_gnorm, [y, z], [rep["ssm_norm_w"]], [(dm.DI, BF16)], 256)
    sv.update(att=att, pool=pool, xc=xc, y=y, hts=hts, ssm=ssm)
    ba = matmul(tag + "br_a", att, need(wts, "w_attn_br", ssm), "nn", BF16, b_blocked=True)
    bp = matmul(tag + "br_p", pool, need(wts, "w_pool_br", ba), "nn", BF16, b_blocked=True)
    bs = matmul(tag + "br_s", ssm, need(wts, "w_ssm_br", bp), "nn", BF16)
    (merged,) = row_call(tag + "merge", f_merge, [gl, ba, bp, bs], [], [(dm.D, BF16)], 256)
    x1 = matmul(tag + "out", merged, need(wts, "w_out", merged), "nn", F32, add=x)
    (h2,) = row_call(tag + "rms2", f_rms, [x1], [rep["ln2_w"]], [(dm.D, BF16)], 256)
    gu = matmul(tag + "gu", h2, need(wts, "w_gate_up", h2), "nn", BF16, b_blocked=True)
    (act,) = row_call(tag + "swiglu", f_swiglu, [gu], [], [(dm.DFF, BF16)], 256)
    x2 = matmul(tag + "down", act, need(wts, "w_down", act), "nn", F32, add=x1)
    sv.update(ba=ba, bp=bp, bs=bs, merged=merged, x1=x1, h2=h2, gu=gu, act=act)
    return x2, sv


class GradSink:
    def __init__(self, dm, li):
        self.dm, self.li, self.started, self.tokens = dm, li, {}, []

    def __setitem__(self, n, g):
        self.started[n] = copies_start(f"rs_l{self.li}_{n}_start", _grad_blocks(self.dm, n, g), False)
        self.tokens.append(self.started[n][4])

    def take(self):
        t, self.tokens = tuple(self.tokens), []
        return t


def _layer_backward(dm, dx2, sv, wts, rep, li, prev_tokens):
    tag = f"l{li}_b_"
    gw, gr = GradSink(dm, li), {}
    dact = matmul(tag + "d_act", dx2, wts["w_down"], "nt", BF16, after=prev_tokens)
    gw["w_down"] = matmul(tag + "g_down", sv["act"], dx2, "tn", BF16)
    (dgu,), _ = row_vjp_call(tag + "swiglu", f_swiglu, [sv["gu"]], [], [dact], [BF16], [], 256)
    dh2 = matmul(tag + "d_h2", dgu, wts["w_gate_up"], "nt", F32, b_blocked=True, after=gw.take())
    gw["w_gate_up"] = matmul(tag + "g_gu", sv["h2"], dgu, "tn", BF16, out_blocks=N_DEV)
    (dx1,), (gr["ln2_w"],) = row_vjp_call(tag + "rms2", f_rms, [sv["x1"]], [rep["ln2_w"]], [dh2], [F32], [True], 256,
                                          adds={0: dx2})
    dmerged = matmul(tag + "d_merged", dx1, wts["w_out"], "nt", BF16, after=gw.take())
    gw["w_out"] = matmul(tag + "g_out", sv["merged"], dx1, "tn", BF16)
    (dgl, dba, dbp, dbs), _ = row_vjp_call(tag + "merge", f_merge, [sv["gl"], sv["ba"], sv["bp"], sv["bs"]], [],
                                           [dmerged], [BF16, BF16, BF16, BF16], [], 256)
    datt = matmul(tag + "d_att", dba, wts["w_attn_br"], "nt", F32, b_blocked=True, after=gw.take())
    gw["w_attn_br"] = matmul(tag + "g_br_a", sv["att"], dba, "tn", BF16, out_blocks=N_DEV)
    dpool = matmul(tag + "d_pool", dbp, wts["w_pool_br"], "nt", F32, b_blocked=True, after=gw.take())
    gw["w_pool_br"] = matmul(tag + "g_br_p", sv["pool"], dbp, "tn", BF16, out_blocks=N_DEV)
    dssm = matmul(tag + "d_ssm", dbs, wts["w_ssm_br"], "nt", F32, after=gw.take())
    gw["w_ssm_br"] = matmul(tag + "g_br_s", sv["ssm"], dbs, "tn", BF16)
    (dy, dz), (gr["ssm_norm_w"],) = row_vjp_call(tag + "gnorm", f_gnorm, [sv["y"], sv["z"]], [rep["ssm_norm_w"]],
                                                 [dssm], [F32, BF16], [True], 256)
    dxc, ddtr, gr["dt_bias"], gr["a_log"], gr["d_skip"] = ssd_bwd(
        tag + "ssd", sv["xc"], sv["dtr"], sv["hts"], rep["dt_bias"], rep["a_log"], rep["d_skip"], dy)
    (dxbc,), (gw["conv_w"], gr["conv_b"]) = halo_vjp_call(
        tag + "conv", f_conv, [(sv["xbc"], dm.CC, 0, CONV_HALO)], [wts["conv_w"], rep["conv_b"]], [dxc],
        [BF16], [True, True], 256)
    (du,), (gw["pool_w"], gr["pool_scale"]) = halo_vjp_call(
        tag + "pool", f_pool, [(sv["u"], dm.PW, 0, POOL_HALO)], [wts["pool_w"], rep["pool_scale"]], [dpool],
        [BF16], [True, True], 256)
    kvi = dm.AW // dm.KVW
    qkv = sv["qkv"]
    (dq, dk, dv), (gr["attn_sink"],) = halo_vjp_call(
        tag + "attn", f_attn, [(qkv, dm.AW, 0, 0), (qkv, dm.KVW, kvi, WINDOW), (qkv, dm.KVW, kvi + 1, WINDOW)],
        [rep["attn_sink"]], [datt], [BF16, BF16, BF16], [True], WINDOW)
    dproj = jnp.concatenate([dq, dk, dv, du, dz, dxbc, ddtr, dgl], axis=1)
    dh = matmul(tag + "d_h", dproj, wts["w_in_int"], "nt", F32, after=gw.take())
    gw["w_in"] = matmul(tag + "g_in", sv["h"], dproj, "tn", BF16)
    (dx,), (gr["ln1_w"],) = row_vjp_call(tag + "rms1", f_rms, [sv["x"]], [rep["ln1_w"]], [dh], [F32], [True], 256,
                                         adds={0: dx1})
    return dx, gw.started, gr, gw.take()


def kernel(x, ln1_w, w_in, attn_sink, conv_w, conv_b, dt_bias, a_log, d_skip, ssm_norm_w, pool_w, pool_scale, w_attn_br, w_pool_br, w_ssm_br, w_out, ln2_w, w_gate_up, w_down, final_w, loss_target, m_ln1_w, m_w_in, m_attn_sink, m_conv_w, m_conv_b, m_dt_bias, m_a_log, m_d_skip, m_ssm_norm_w, m_pool_w, m_pool_scale, m_w_attn_br, m_w_pool_br, m_w_ssm_br, m_w_out, m_ln2_w, m_w_gate_up, m_w_down, m_final_w, v_ln1_w, v_w_in, v_attn_sink, v_conv_w, v_conv_b, v_dt_bias, v_a_log, v_d_skip, v_ssm_norm_w, v_pool_w, v_pool_scale, v_w_attn_br, v_w_pool_br, v_w_ssm_br, v_w_out, v_ln2_w, v_w_gate_up, v_w_down, v_final_w):
    dm = Dims()
    W = dict(ln1_w=ln1_w, w_in=w_in, attn_sink=attn_sink, conv_w=conv_w, conv_b=conv_b, dt_bias=dt_bias, a_log=a_log,
             d_skip=d_skip, ssm_norm_w=ssm_norm_w, pool_w=pool_w, pool_scale=pool_scale, w_attn_br=w_attn_br,
             w_pool_br=w_pool_br, w_ssm_br=w_ssm_br, w_out=w_out, ln2_w=ln2_w, w_gate_up=w_gate_up, w_down=w_down,
             final_w=final_w)
    M = dict(ln1_w=m_ln1_w, w_in=m_w_in, attn_sink=m_attn_sink, conv_w=m_conv_w, conv_b=m_conv_b, dt_bias=m_dt_bias,
             a_log=m_a_log, d_skip=m_d_skip, ssm_norm_w=m_ssm_norm_w, pool_w=m_pool_w, pool_scale=m_pool_scale,
             w_attn_br=m_w_attn_br, w_pool_br=m_w_pool_br, w_ssm_br=m_w_ssm_br, w_out=m_w_out, ln2_w=m_ln2_w,
             w_gate_up=m_w_gate_up, w_down=m_w_down, final_w=m_final_w)
    V = dict(ln1_w=v_ln1_w, w_in=v_w_in, attn_sink=v_attn_sink, conv_w=v_conv_w, conv_b=v_conv_b, dt_bias=v_dt_bias,
             a_log=v_a_log, d_skip=v_d_skip, ssm_norm_w=v_ssm_norm_w, pool_w=v_pool_w, pool_scale=v_pool_scale,
             w_attn_br=v_w_attn_br, w_pool_br=v_w_pool_br, w_ssm_br=v_w_ssm_br, w_out=v_w_out, ln2_w=v_ln2_w,
             w_gate_up=v_w_gate_up, w_down=v_w_down, final_w=v_final_w)
    xl = x[0]
    target = loss_target[0]

    tokens = []
    full = [_gather_weights(dm, W, li, li == 0, tokens) for li in range(DEPTH)]
    rep = [{n: W[n][li].reshape(1, -1) for n in REPLICATED} for li in range(DEPTH)]
    rep[0]["ln1_w"] = rep[0]["ln1_w"] + tokens[-1][0, 0]

    saved = []
    xa = xl
    for li in range(DEPTH):
        xa, sv = _layer_forward(dm, xa, full[li], rep[li], li)
        saved.append(sv)
    dxa, g_final, loss_blk = loss_head(xa, W["final_w"].reshape(1, -1), target, 256)

    gws, grs, left = [None] * DEPTH, [None] * DEPTH, ()
    for li in reversed(range(DEPTH)):
        dxa, gws[li], grs[li], left = _layer_backward(dm, dxa, saved[li], full[li], rep[li], li, left)
    grad_x = dxa[None]

    sharded_out = {n: None for n in SHARDED}
    for li in reversed(range(DEPTH)):
        for n in SHARDED:
            parts = copies_wait(f"rs_l{li}_{n}_wait", gws[li][n], dxa)
            sharded_out[n] = adamw_sharded(f"adamw_l{li}_{n}", parts, _as_rows(n, W[n]), _as_rows(n, M[n]),
                                           _as_rows(n, V[n]), li, sharded_out[n])
    g_sh, d_sh, m_sh, v_sh = [{n: _from_rows(n, sharded_out[n][k], W[n]) for n in SHARDED} for k in range(4)]

    small_names = [(n, li) for li in range(DEPTH) for n in REPLICATED] + [("final_w", None)]
    small_shape = lambda n, li: W[n].shape if li is None else W[n].shape[1:]
    small_rows = [-(-_size(small_shape(n, li)) // (8 * 128)) * 8 for n, li in small_names]
    loss_row = sum(small_rows)

    def small_pack(get, last=None):
        rows = []
        for (n, li), r in zip(small_names, small_rows):
            a = get(n, li).reshape(-1).astype(F32)
            rows.append(jnp.pad(a, (0, r * 128 - a.size)).reshape(r, 128))
        rows.append(jnp.zeros((8, 128), F32) if last is None else last)
        return jnp.concatenate(rows, axis=0)

    part = small_pack(lambda n, li: g_final if li is None else grs[li][n], loss_blk)
    wsm = small_pack(lambda n, li: W[n] if li is None else W[n][li])
    msm = small_pack(lambda n, li: M[n] if li is None else M[n][li])
    vsm = small_pack(lambda n, li: V[n] if li is None else V[n][li])
    sm = small_allreduce_adamw(part, wsm, msm, vsm)
    loss = sm[0][loss_row, 0]

    def small_unpack(buf):
        out, r0 = {}, 0
        for (n, li), r in zip(small_names, small_rows):
            shp = small_shape(n, li)
            out[(n, li)] = buf[r0:r0 + r].reshape(-1)[:_size(shp)].reshape(shp)
            r0 += r
        return out

    sm_g, sm_d, sm_m, sm_v = [small_unpack(b) for b in sm]

    def assemble(sharded_list, small):
        outs = []
        for n in ("ln1_w", "w_in", "attn_sink", "conv_w", "conv_b", "dt_bias", "a_log", "d_skip", "ssm_norm_w", "pool_w",
                  "pool_scale", "w_attn_br", "w_pool_br", "w_ssm_br", "w_out", "ln2_w", "w_gate_up", "w_down"):
            if n in SHARDED:
                outs.append(sharded_list[n])
            else:
                outs.append(jnp.stack([small[(n, li)] for li in range(DEPTH)]))
        outs.append(small[("final_w", None)])
        return outs

    return (loss, grad_x, *assemble(g_sh, sm_g), *assemble(d_sh, sm_d), *assemble(m_sh, sm_m), *assemble(v_sh, sm_v))
```

```python
import functools

import jax
import jax.numpy as jnp
from jax import lax
from jax.experimental import pallas as pl
from jax.experimental.pallas import tpu as pltpu

D_MODEL = 2048
DEPTH = 2
ATT_HEAD_DIM = 64
ATT_Q_HEADS = 16
ATT_KV_HEADS = 4
WINDOW = 128
POOL_WINDOWS = (2, 4, 8, 16)
POOL_WIDTH = D_MODEL // 2
D_INNER = D_MODEL
SSM_HEAD_DIM = 64
SSM_GROUPS = 4
D_STATE = 128
CONV_K = 4
CHUNK = 128
N_BRANCH = 3
D_FF = 5632
EPS = 1e-6

ADAM_LR = 0.001
ADAM_B1 = 0.9
ADAM_B2 = 0.999
ADAM_EPS = 1e-08
ADAM_WD = 0.01
ADAM_STEP = 10

N_DEV = 8
F32 = jnp.float32
BF16 = jnp.bfloat16
MXU_DTYPE = jnp.bfloat16
NEG = -1e30
VMEM_CAP = 60 * 2**20
CONV_HALO = 8
POOL_HALO = 16
MESH_T = pl.DeviceIdType.MESH


class Dims:
    def __init__(self):
        self.D = D_MODEL
        self.AW = ATT_Q_HEADS * ATT_HEAD_DIM
        self.KVW = ATT_KV_HEADS * ATT_HEAD_DIM
        self.GQ = ATT_Q_HEADS // ATT_KV_HEADS
        self.PW = POOL_WIDTH
        self.PG = len(POOL_WINDOWS)
        self.PC = POOL_WIDTH // len(POOL_WINDOWS)
        self.DI = D_INNER
        self.H = D_INNER // SSM_HEAD_DIM
        self.P = SSM_HEAD_DIM
        self.G = SSM_GROUPS
        self.HPG = self.H // SSM_GROUPS
        self.GW = D_INNER // SSM_GROUPS
        self.N = D_STATE
        self.GN = SSM_GROUPS * D_STATE
        self.CC = D_INNER + 2 * SSM_GROUPS * D_STATE
        self.DTP = -(-self.H // 256) * 256
        self.DFF = D_FF
        self.in_widths = (self.AW, self.KVW, self.KVW, self.PW, self.DI, self.CC, self.H, N_BRANCH * self.D)
        self.IN_COLS = sum(self.in_widths)
        self.QKV = self.AW + 2 * self.KVW
        self.seg = (self.QKV, self.PW, self.DI, self.CC, self.DTP, N_BRANCH * self.D)
        self.IN_INT = sum(self.seg)


def _tile(n, cap, mult=128):
    if n <= cap:
        return n
    best = 0
    for t in range(mult, cap + 1, mult):
        if n % t == 0:
            best = t
    assert best, (n, cap, mult)
    return best


def _nbytes(shape, dtype):
    n = 1
    for s in shape:
        n *= s
    return n * jnp.dtype(dtype).itemsize


def _params(est_bytes, sem=None):
    limit = int(min(VMEM_CAP, max(32 * 2**20, est_bytes * 3 // 2 + 8 * 2**20)))
    kw = dict(vmem_limit_bytes=limit)
    if sem is not None:
        kw["dimension_semantics"] = sem
    return pltpu.CompilerParams(**kw)


def _raw_dot(a, b, ca, cb):
    return lax.dot_general(a.astype(MXU_DTYPE), b.astype(MXU_DTYPE), (((ca,), (cb,)), ((), ())),
                           preferred_element_type=F32)


@functools.partial(jax.custom_vjp, nondiff_argnums=(2, 3))
def bdot(a, b, ca, cb):
    return _raw_dot(a, b, ca, cb)


def _bdot_fwd(a, b, ca, cb):
    return _raw_dot(a, b, ca, cb), (a, b)


def _bdot_bwd(ca, cb, res, g):
    a, b = res
    if (ca, cb) == (1, 0):
        da, db = bdot(g, b, 1, 1), bdot(a, g, 0, 0)
    elif (ca, cb) == (1, 1):
        da, db = bdot(g, b, 1, 0), bdot(g, a, 0, 0)
    else:
        assert (ca, cb) == (0, 0)
        da, db = bdot(b, g, 1, 1), bdot(a, g, 1, 0)
    return da.astype(a.dtype), db.astype(b.dtype)


bdot.defvjp(_bdot_fwd, _bdot_bwd)


def hdot(a, b, ca=1, cb=0):
    return lax.dot_general(a, b, (((ca,), (cb,)), ((), ())), precision=lax.Precision.HIGHEST,
                           preferred_element_type=F32)


def _split_dot(x, e, ca, cb):
    hi = x.astype(BF16)
    lo = (x - hi.astype(F32)).astype(BF16)
    dn = (((ca,), (cb,)), ((), ()))
    eb = e.astype(BF16)
    return (lax.dot_general(hi, eb, dn, preferred_element_type=F32)
            + lax.dot_general(lo, eb, dn, preferred_element_type=F32))


@jax.custom_vjp
def edot(x, e):
    return _split_dot(x, e, 1, 0)


def _edot_fwd(x, e):
    return _split_dot(x, e, 1, 0), e


def _edot_bwd(e, g):
    return _split_dot(g, e, 1, 1), jnp.zeros_like(e)


edot.defvjp(_edot_fwd, _edot_bwd)


def _silu(x):
    return x * jax.nn.sigmoid(x)


def _softplus(x):
    return jnp.maximum(x, 0.0) + jnp.log1p(jnp.exp(-jnp.abs(x)))


def matmul(name, a, b, mode, out_dtype, add=None, b_blocked=False, out_blocks=0, after=()):
    if b_blocked:
        nb, rows, n = b.shape
        b_rows, b_cols = rows, nb * n
    else:
        b_rows, b_cols = b.shape
    if mode == "nn":
        (M, K), (K2, N) = a.shape, (b_rows, b_cols)
    elif mode == "nt":
        (M, K), (N, K2) = a.shape, (b_rows, b_cols)
    else:
        (K, M), (K2, N) = a.shape, (b_rows, b_cols)
    assert K == K2, (name, a.shape, b.shape, mode)
    n_blk = n if b_blocked else (N // out_blocks if out_blocks else 0)
    tm = _tile(M, 1024)
    tn = _tile(n_blk if (n_blk and mode != "nt") else N, 1536)
    tk = _tile(n_blk if (n_blk and mode == "nt") else K, 2048)
    nk = K // tk
    dims = {"nn": (1, 0), "nt": (1, 1), "tn": (0, 0)}[mode]
    a_spec = (pl.BlockSpec((tk, tm), lambda i, j, k: (k, i)) if mode == "tn"
              else pl.BlockSpec((tm, tk), lambda i, j, k: (i, k)))
    if b_blocked and mode == "nt":
        per = n // tk
        b_spec = pl.BlockSpec((None, tn, tk), lambda i, j, k: (k // per, j, k % per))
    elif b_blocked:
        per = n // tn
        b_spec = pl.BlockSpec((None, tk, tn), lambda i, j, k: (j // per, k, j % per))
    elif mode == "nt":
        b_spec = pl.BlockSpec((tn, tk), lambda i, j, k: (j, k))
    else:
        b_spec = pl.BlockSpec((tk, tn), lambda i, j, k: (k, j))
    o_spec = pl.BlockSpec((tm, tn), lambda i, j, k: (i, j))
    out_shape = jax.ShapeDtypeStruct((M, N), out_dtype)
    if out_blocks:
        assert mode == "tn" and add is None
        per_o = n_blk // tn
        o_spec = pl.BlockSpec((None, tm, tn), lambda i, j, k: (j // per_o, i, j % per_o))
        out_shape = jax.ShapeDtypeStruct((out_blocks, M, n_blk), out_dtype)
    has_add = add is not None

    def body(*refs):
        a_ref, b_ref = refs[0], refs[1]
        c_ref = refs[2] if has_add else None
        o_ref = refs[2 + has_add + len(after)]
        p = _raw_dot(a_ref[...], b_ref[...], *dims)
        if nk == 1:
            if has_add:
                p = p + c_ref[...].astype(F32)
            o_ref[...] = p.astype(o_ref.dtype)
            return
        acc = refs[-1]
        k = pl.program_id(2)

        @pl.when(k == 0)
        def _():
            acc[...] = p + c_ref[...].astype(F32) if has_add else p

        @pl.when(k > 0)
        def _():
            acc[...] += p

        @pl.when(k == nk - 1)
        def _():
            o_ref[...] = acc[...].astype(o_ref.dtype)

    est = 2 * (_nbytes((tm, tk), a.dtype) + _nbytes((tk, tn), b.dtype) + _nbytes((tm, tn), out_dtype))
    est += 3 * _nbytes((tm, tn), F32) + _nbytes((tm, tk), MXU_DTYPE) + _nbytes((tk, tn), MXU_DTYPE)
    if has_add:
        est += 2 * _nbytes((tm, tn), add.dtype)
    args = (a, b) + ((add,) if has_add else ()) + tuple(after)
    in_specs = [a_spec, b_spec] + ([o_spec] if has_add else []) + [pl.BlockSpec(memory_space=pl.ANY)] * len(after)
    return pl.pallas_call(
        body, name=name, grid=(M // tm, N // tn, nk),
        in_specs=in_specs, out_specs=o_spec, out_shape=out_shape,
        scratch_shapes=[pltpu.VMEM((tm, tn), F32)] if nk > 1 else [],
        compiler_params=_params(est, ("parallel", "parallel", "arbitrary")),
    )(*args)


def row_call(name, f, rows, pars, outs, tm):
    L = rows[0].shape[0]
    tm = min(tm, L)
    nr, npar = len(rows), len(pars)

    def body(*refs):
        vals = [r[...] for r in refs[:nr + npar]]
        res = f(*vals)
        for o_ref, v in zip(refs[nr + npar:], res):
            o_ref[...] = v.astype(o_ref.dtype)

    est = 2 * sum(_nbytes((tm, a.shape[1]), a.dtype) for a in rows) + 2 * sum(_nbytes((tm, w), d) for w, d in outs)
    est += 4 * sum(_nbytes((tm, a.shape[1]), F32) for a in rows)
    res = pl.pallas_call(
        body, name=name, grid=(L // tm,),
        in_specs=[pl.BlockSpec((tm, a.shape[1]), lambda i: (i, 0)) for a in rows]
        + [pl.BlockSpec(p.shape, lambda i: (0, 0)) for p in pars],
        out_specs=[pl.BlockSpec((tm, w), lambda i: (i, 0)) for w, _ in outs],
        out_shape=[jax.ShapeDtypeStruct((L, w), d) for w, d in outs],
        compiler_params=_params(est, ("parallel",)),
    )(*rows, *pars)
    return res


def row_vjp_call(name, f, rows, pars, cots, row_grad_dtypes, par_grads, tm, adds=None):
    L = rows[0].shape[0]
    tm = min(tm, L)
    nr, npar, nc = len(rows), len(pars), len(cots)
    adds = adds or {}
    add_idx = sorted(adds)
    rg_idx = [i for i, d in enumerate(row_grad_dtypes) if d is not None]
    pg_idx = [i for i, w in enumerate(par_grads) if w]
    diff_idx = rg_idx + [nr + i for i in pg_idx]

    def body(*refs):
        vals = [r[...] for r in refs[:nr + npar]]
        cvals = [r[...] for r in refs[nr + npar:nr + npar + nc]]
        avals = [r[...] for r in refs[nr + npar + nc:nr + npar + nc + len(add_idx)]]
        out_refs = refs[nr + npar + nc + len(add_idx):]

        def g(*dv):
            full = list(vals)
            for i, v in zip(diff_idx, dv):
                full[i] = v
            return tuple(f(*full))

        res, vjp = jax.vjp(g, *[vals[i] for i in diff_idx])
        grads = vjp(tuple(c.astype(r.dtype) for c, r in zip(cvals, res)))
        for n, i in enumerate(rg_idx):
            gval = grads[n].astype(F32)
            if i in adds:
                gval = gval + avals[add_idx.index(i)].astype(F32)
            out_refs[n][...] = gval.astype(out_refs[n].dtype)
        first = pl.program_id(0) == 0
        for n, i in enumerate(pg_idx):
            o_ref = out_refs[len(rg_idx) + n]
            gval = grads[len(rg_idx) + n].astype(F32)

            @pl.when(first)
            def _(o_ref=o_ref, gval=gval):
                o_ref[...] = gval

            @pl.when(jnp.logical_not(first))
            def _(o_ref=o_ref, gval=gval):
                o_ref[...] += gval

    row_spec = lambda a: pl.BlockSpec((tm, a.shape[1]), lambda i: (i, 0))
    est = 2 * sum(_nbytes((tm, a.shape[1]), a.dtype) for a in list(rows) + list(cots))
    est += 10 * sum(_nbytes((tm, a.shape[1]), F32) for a in rows)
    res = pl.pallas_call(
        body, name=name, grid=(L // tm,),
        in_specs=[row_spec(a) for a in rows] + [pl.BlockSpec(p.shape, lambda i: (0, 0)) for p in pars]
        + [row_spec(c) for c in cots] + [row_spec(adds[i]) for i in add_idx],
        out_specs=[row_spec(rows[i]) for i in rg_idx] + [pl.BlockSpec(pars[i].shape, lambda i_: (0, 0)) for i in pg_idx],
        out_shape=[jax.ShapeDtypeStruct(rows[i].shape, row_grad_dtypes[i]) for i in rg_idx]
        + [jax.ShapeDtypeStruct(pars[i].shape, F32) for i in pg_idx],
        compiler_params=_params(est, ("arbitrary",)),
    )(*rows, *pars, *cots, *[adds[i] for i in add_idx])
    return list(res[:len(rg_idx)]), list(res[len(rg_idx):])


def _halo_specs(rows, T, nt, rev):
    specs = []
    for (_, w, ci, hs) in rows:
        if rev:
            specs.append(pl.BlockSpec((T, w), lambda j, ci=ci: (nt - 1 - j, ci)))
        else:
            specs.append(pl.BlockSpec((T, w), lambda i, ci=ci: (i, ci)))
        if hs:
            r = T // hs
            if rev:
                specs.append(pl.BlockSpec((hs, w), lambda j, ci=ci, r=r: (jnp.maximum((nt - 1 - j) * r - 1, 0), ci)))
            else:
                specs.append(pl.BlockSpec((hs, w), lambda i, ci=ci, r=r: (jnp.maximum(i * r - 1, 0), ci)))
    return specs


def _halo_args(rows):
    args = []
    for (a, _, _, hs) in rows:
        args.append(a)
        if hs:
            args.append(a)
    return args


def _halo_vals(rows, refs):
    vals, n = [], 0
    for (_, _, _, hs) in rows:
        if hs:
            vals.append((refs[n + 1][...], refs[n][...]))
            n += 2
        else:
            vals.append(refs[n][...])
            n += 1
    return vals, n


def halo_call(name, f, rows, pars, outs, T):
    L = rows[0][0].shape[0]
    T = min(T, L)
    nt = L // T
    npar = len(pars)

    def body(*refs):
        i = pl.program_id(0)
        vals, n = _halo_vals(rows, refs)
        pv = [r[...] for r in refs[n:n + npar]]
        res = f(i == 0, i * T, *vals, *pv)
        for o_ref, v in zip(refs[n + npar:], res):
            o_ref[...] = v.astype(o_ref.dtype)

    est = 2 * sum(_nbytes((T, w), a.dtype) for a, w, _, _ in rows) + 2 * sum(_nbytes((T, w), d) for w, d in outs)
    est += 8 * sum(_nbytes((T, w), F32) for _, w, _, _ in rows)
    return pl.pallas_call(
        body, name=name, grid=(nt,),
        in_specs=_halo_specs(rows, T, nt, False) + [pl.BlockSpec(p.shape, lambda i, nd=p.ndim: (0,) * nd) for p in pars],
        out_specs=[pl.BlockSpec((T, w), lambda i: (i, 0)) for w, _ in outs],
        out_shape=[jax.ShapeDtypeStruct((L, w), d) for w, d in outs],
        compiler_params=_params(est, ("parallel",)),
    )(*_halo_args(rows), *pars)


def halo_vjp_call(name, f, rows, pars, cots, row_grad_dtypes, par_grads, T):
    L = rows[0][0].shape[0]
    T = min(T, L)
    nt = L // T
    nr, npar, nc = len(rows), len(pars), len(cots)
    pg_idx = [i for i, w in enumerate(par_grads) if w]
    halo_idx = [i for i, r in enumerate(rows) if r[3]]

    def body(*refs):
        j = pl.program_id(0)
        i = nt - 1 - j
        vals, n = _halo_vals(rows, refs)
        pv = [r[...] for r in refs[n:n + npar]]
        cv = [r[...] for r in refs[n + npar:n + npar + nc]]
        out_refs = refs[n + npar + nc:n + npar + nc + nr + len(pg_idx)]
        carries = refs[n + npar + nc + nr + len(pg_idx):]
        first = i == 0

        def g(vals_, pv_):
            return tuple(f(first, i * T, *vals_, *pv_))

        res, vjp = jax.vjp(g, vals, pv)
        dvals, dpv = vjp(tuple(c.astype(r.dtype) for c, r in zip(cv, res)))

        @pl.when(j == 0)
        def _():
            for c_ref in carries:
                c_ref[...] = jnp.zeros_like(c_ref)

        for k in range(nr):
            hs = rows[k][3]
            o_ref = out_refs[k]
            if hs:
                dh, dc = dvals[k]
                c_ref = carries[halo_idx.index(k)]
                dc = dc.astype(F32)
                if hs == T:
                    o_ref[...] = (dc + c_ref[...]).astype(o_ref.dtype)
                else:
                    o_ref[0:T - hs, :] = dc[0:T - hs].astype(o_ref.dtype)
                    o_ref[T - hs:T, :] = (dc[T - hs:T] + c_ref[...]).astype(o_ref.dtype)
                c_ref[...] = dh.astype(F32)
            else:
                o_ref[...] = dvals[k].astype(o_ref.dtype)
        for m, k in enumerate(pg_idx):
            o_ref = out_refs[nr + m]
            gval = dpv[k].astype(F32)

            @pl.when(j == 0)
            def _(o_ref=o_ref, gval=gval):
                o_ref[...] = gval

            @pl.when(j > 0)
            def _(o_ref=o_ref, gval=gval):
                o_ref[...] += gval

    est = 2 * sum(_nbytes((T, w), a.dtype) for a, w, _, _ in rows) + 2 * sum(_nbytes((T, c.shape[1]), c.dtype) for c in cots)
    est += 12 * sum(_nbytes((T, w), F32) for _, w, _, _ in rows)
    res = pl.pallas_call(
        body, name=name, grid=(nt,),
        in_specs=_halo_specs(rows, T, nt, True) + [pl.BlockSpec(p.shape, lambda j, nd=p.ndim: (0,) * nd) for p in pars]
        + [pl.BlockSpec((T, c.shape[1]), lambda j: (nt - 1 - j, 0)) for c in cots],
        out_specs=[pl.BlockSpec((T, w), lambda j: (nt - 1 - j, 0)) for _, w, _, _ in rows]
        + [pl.BlockSpec(pars[k].shape, lambda j, nd=pars[k].ndim: (0,) * nd) for k in pg_idx],
        out_shape=[jax.ShapeDtypeStruct((L, w), row_grad_dtypes[k]) for k, (_, w, _, _) in enumerate(rows)]
        + [jax.ShapeDtypeStruct(pars[k].shape, F32) for k in pg_idx],
        scratch_shapes=[pltpu.VMEM((rows[k][3], rows[k][1]), F32) for k in halo_idx],
        compiler_params=_params(est, ("arbitrary",)),
    )(*_halo_args(rows), *pars, *cots)
    return list(res[:nr]), list(res[nr:])


def f_rms(x, w):
    x = x.astype(F32)
    return (x * lax.rsqrt(jnp.mean(x * x, axis=-1, keepdims=True) + EPS) * w,)


def f_swiglu(gu):
    dff = gu.shape[1] // 2
    gu = gu.astype(F32)
    return (_silu(gu[:, :dff]) * gu[:, dff:],)


def f_merge(gl, a, p, s):
    d = a.shape[1]
    g = jax.nn.sigmoid(gl.astype(F32))
    return (g[:, :d] * a.astype(F32) + g[:, d:2 * d] * p.astype(F32) + g[:, 2 * d:] * s.astype(F32),)


def f_gnorm(y, z, nw):
    dm = Dims()
    g = y * _silu(z.astype(F32))
    outs = []
    for gi in range(dm.G):
        gg = g[:, gi * dm.GW:(gi + 1) * dm.GW]
        outs.append(gg * lax.rsqrt(jnp.mean(gg * gg, axis=-1, keepdims=True) + EPS))
    return (jnp.concatenate(outs, axis=1) * nw,)


def f_conv(first, row0, xs, w, b):
    halo, cur = xs
    halo = jnp.where(first, 0.0, halo)
    ext = jnp.concatenate([halo, cur], axis=0)
    T = cur.shape[0]
    base = CONV_HALO - (CONV_K - 1)
    pre = b
    for k in range(CONV_K):
        pre = pre + w[k:k + 1, :] * ext[base + k:base + k + T]
    return (_silu(pre),)


def f_pool(first, row0, us, pw, scale):
    dm = Dims()
    halo, cur = us
    halo = jnp.where(first, 0.0, halo)
    ext = jnp.concatenate([halo, cur], axis=0)
    T = cur.shape[0]
    t = row0 + lax.broadcasted_iota(jnp.int32, (T, 1), 0)
    outs = []
    for gi, w in enumerate(POOL_WINDOWS):
        assert w & (w - 1) == 0 and w <= POOL_HALO
        s = ext[:, gi * dm.PC:(gi + 1) * dm.PC]
        sh = 1
        while sh < w:
            s = s + jnp.concatenate([jnp.zeros((sh, dm.PC), F32), s[:-sh]], axis=0)
            sh *= 2
        cnt = jnp.minimum(t + 1, w).astype(F32)
        mixed = s[POOL_HALO:] / cnt - cur[:, gi * dm.PC:(gi + 1) * dm.PC]
        outs.append(bdot(mixed, pw[gi], 1, 0))
    return (jnp.concatenate(outs, axis=1) * scale,)


def f_attn(first, row0, q, ks, vs, sink):
    dm = Dims()
    kp, kc = ks
    vp, vc = vs
    T = q.shape[0]
    hd = ATT_HEAD_DIM
    qi = lax.broadcasted_iota(jnp.int32, (T, 2 * T), 0)
    si = lax.broadcasted_iota(jnp.int32, (T, 2 * T), 1)
    diff = qi + T - si
    ok = (diff >= 0) & (diff < WINDOW) & ((si >= T) | jnp.logical_not(first))
    mask = jnp.concatenate([ok] * dm.GQ, axis=0)
    outs = []
    for k in range(ATT_KV_HEADS):
        kk = jnp.concatenate([kp[:, k * hd:(k + 1) * hd], kc[:, k * hd:(k + 1) * hd]], axis=0)
        vv = jnp.concatenate([vp[:, k * hd:(k + 1) * hd], vc[:, k * hd:(k + 1) * hd]], axis=0)
        heads = [k * dm.GQ + g for g in range(dm.GQ)]
        qs = jnp.concatenate([q[:, h * hd:(h + 1) * hd] for h in heads], axis=0)
        s = bdot(qs, kk, 1, 1) * (hd ** -0.5)
        s = jnp.where(mask, s, NEG)
        sk = jnp.concatenate([jnp.broadcast_to(sink[:, h:h + 1], (T, 1)) for h in heads], axis=0)
        m = jnp.maximum(jnp.max(s, axis=-1, keepdims=True), sk)
        p = jnp.exp(s - m)
        den = jnp.sum(p, axis=-1, keepdims=True) + jnp.exp(sk - m)
        o = bdot(p / den, vv, 1, 0)
        outs += [o[g * T:(g + 1) * T] for g in range(dm.GQ)]
    return (jnp.concatenate(outs, axis=1),)


def f_ssd(xc, dtr, ht, dt_bias, a_log, d_skip):
    dm = Dims()
    Q = xc.shape[0]
    xs = xc[:, :dm.DI]
    bm = xc[:, dm.DI:dm.DI + dm.GN]
    cm = xc[:, dm.DI + dm.GN:]
    expand = (lax.broadcasted_iota(jnp.int32, (dm.H, dm.DI), 1) // dm.P
              == lax.broadcasted_iota(jnp.int32, (dm.H, dm.DI), 0)).astype(F32)
    ri = lax.broadcasted_iota(jnp.int32, (Q, Q), 0)
    ci = lax.broadcasted_iota(jnp.int32, (Q, Q), 1)
    causal = ri >= ci
    tinc = causal.astype(F32)
    dt = _softplus(dtr[:, :dm.H] + dt_bias)
    da = dt * (-jnp.exp(a_log))
    acs = hdot(tinc, da)
    acs_t = hdot(da, tinc, 0, 1)
    eacs = jnp.exp(acs)
    dend = jnp.exp(acs[Q - 1:Q, :] - acs)
    ex = edot(jnp.concatenate([dt, eacs, dend, jnp.broadcast_to(d_skip, (8, dm.H))], axis=0), expand)
    dt_x, eacs_x, dend_x, dsk_x = ex[:Q], ex[Q:2 * Q], ex[2 * Q:3 * Q], ex[3 * Q:3 * Q + 1]
    xdt = xs * dt_x
    ys, hts = [], []
    for g in range(dm.G):
        gs = slice(g * dm.GW, (g + 1) * dm.GW)
        bg = bm[:, g * dm.N:(g + 1) * dm.N]
        cg = cm[:, g * dm.N:(g + 1) * dm.N]
        cb = bdot(cg, bg, 1, 1)
        y_off = bdot(cg, ht[:, gs], 1, 0) * eacs_x[:, gs]
        xg = xdt[:, gs]
        st = bdot(bg, xg * dend_x[:, gs], 0, 0)
        hts.append(ht[:, gs] * eacs_x[Q - 1:Q, gs] + st)
        yd = []
        for e in range(dm.HPG):
            h = g * dm.HPG + e
            seg = acs[:, h:h + 1] - acs_t[h:h + 1, :]
            lm = jnp.exp(jnp.where(causal, seg, NEG))
            yd.append(bdot(cb * lm, xg[:, e * dm.P:(e + 1) * dm.P], 1, 0))
        ys.append(jnp.concatenate(yd, axis=1) + y_off)
    y = jnp.concatenate(ys, axis=1) + dsk_x * xs
    return y, jnp.concatenate(hts, axis=1)


def ssd_fwd(name, xc, dtr, dt_bias, a_log, d_skip):
    dm = Dims()
    L = xc.shape[0]
    Q = CHUNK
    nc = L // Q

    def body(xc_ref, dtr_ref, b_ref, a_ref, s_ref, y_ref, hts_ref, ht):
        @pl.when(pl.program_id(0) == 0)
        def _():
            ht[...] = jnp.zeros_like(ht)

        h0 = ht[...]
        hts_ref[0] = h0
        y, h1 = f_ssd(xc_ref[...], dtr_ref[...], h0, b_ref[...], a_ref[...], s_ref[...])
        y_ref[...] = y
        ht[...] = h1

    par = pl.BlockSpec((1, dm.H), lambda c: (0, 0))
    est = 40 * _nbytes((Q, dm.CC), F32) + 4 * _nbytes((dm.N, dm.DI), F32)
    return pl.pallas_call(
        body, name=name, grid=(nc,),
        in_specs=[pl.BlockSpec((Q, dm.CC), lambda c: (c, 0)), pl.BlockSpec((Q, dm.DTP), lambda c: (c, 0)), par, par, par],
        out_specs=[pl.BlockSpec((Q, dm.DI), lambda c: (c, 0)), pl.BlockSpec((1, dm.N, dm.DI), lambda c: (c, 0, 0))],
        out_shape=[jax.ShapeDtypeStruct((L, dm.DI), F32), jax.ShapeDtypeStruct((nc, dm.N, dm.DI), F32)],
        scratch_shapes=[pltpu.VMEM((dm.N, dm.DI), F32)],
        compiler_params=_params(est, ("arbitrary",)),
    )(xc, dtr, dt_bias, a_log, d_skip)


def ssd_bwd(name, xc, dtr, hts, dt_bias, a_log, d_skip, dy):
    dm = Dims()
    L = xc.shape[0]
    Q = CHUNK
    nc = L // Q

    def body(xc_ref, dtr_ref, hts_ref, b_ref, a_ref, s_ref, dy_ref, dxc_ref, ddtr_ref, db_ref, da_ref, ds_ref, dht):
        j = pl.program_id(0)

        @pl.when(j == 0)
        def _():
            dht[...] = jnp.zeros_like(dht)

        _, vjp = jax.vjp(f_ssd, xc_ref[...], dtr_ref[...], hts_ref[0], b_ref[...], a_ref[...], s_ref[...])
        dxc, ddtr, dh0, db, da, ds = vjp((dy_ref[...], dht[...]))
        dxc_ref[...] = dxc.astype(dxc_ref.dtype)
        ddtr_ref[...] = ddtr.astype(ddtr_ref.dtype)
        dht[...] = dh0
        for o_ref, gval in ((db_ref, db), (da_ref, da), (ds_ref, ds)):
            @pl.when(j == 0)
            def _(o_ref=o_ref, gval=gval):
                o_ref[...] = gval

            @pl.when(j > 0)
            def _(o_ref=o_ref, gval=gval):
                o_ref[...] += gval

    par = pl.BlockSpec((1, dm.H), lambda j: (0, 0))
    rev = lambda w: pl.BlockSpec((Q, w), lambda j: (nc - 1 - j, 0))
    est = 80 * _nbytes((Q, dm.CC), F32) + 6 * _nbytes((dm.N, dm.DI), F32)
    return pl.pallas_call(
        body, name=name, grid=(nc,),
        in_specs=[rev(dm.CC), rev(dm.DTP), pl.BlockSpec((1, dm.N, dm.DI), lambda j: (nc - 1 - j, 0, 0)), par, par, par, rev(dm.DI)],
        out_specs=[rev(dm.CC), rev(dm.DTP), par, par, par],
        out_shape=[jax.ShapeDtypeStruct((L, dm.CC), F32), jax.ShapeDtypeStruct((L, dm.DTP), BF16)]
        + [jax.ShapeDtypeStruct((1, dm.H), F32)] * 3,
        scratch_shapes=[pltpu.VMEM((dm.N, dm.DI), F32)],
        compiler_params=_params(est, ("arbitrary",)),
    )(xc, dtr, hts, dt_bias, a_log, d_skip, dy)


def loss_head(x, w, target, tm):
    L, D = x.shape
    tm = min(tm, L)

    def tile_loss(xv, wv, tv):
        (y,) = f_rms(xv, wv)
        return 0.5 * jnp.sum(jnp.mean(jnp.square(y - tv), axis=-1))

    def body(x_ref, w_ref, t_ref, dx_ref, dw_ref, loss_ref):
        val, (dx, dw) = jax.value_and_grad(tile_loss, argnums=(0, 1))(x_ref[...], w_ref[...], t_ref[...])
        dx_ref[...] = dx
        first = pl.program_id(0) == 0
        lv = jnp.full((8, 128), val, F32)

        @pl.when(first)
        def _():
            dw_ref[...] = dw
            loss_ref[...] = lv

        @pl.when(jnp.logical_not(first))
        def _():
            dw_ref[...] += dw
            loss_ref[...] += lv

    row = pl.BlockSpec((tm, D), lambda i: (i, 0))
    est = 16 * _nbytes((tm, D), F32)
    return pl.pallas_call(
        body, name="loss_head", grid=(L // tm,),
        in_specs=[row, pl.BlockSpec((1, D), lambda i: (0, 0)), row],
        out_specs=[row, pl.BlockSpec((1, D), lambda i: (0, 0)), pl.BlockSpec((8, 128), lambda i: (0, 0))],
        out_shape=[jax.ShapeDtypeStruct((L, D), F32), jax.ShapeDtypeStruct((1, D), F32), jax.ShapeDtypeStruct((8, 128), F32)],
        compiler_params=_params(est, ("arbitrary",)),
    )(x, w, target)


def _adamw(w, g, m, v):
    m = ADAM_B1 * m + (1.0 - ADAM_B1) * g
    v = ADAM_B2 * v + (1.0 - ADAM_B2) * jnp.square(g)
    m_hat = m / (1.0 - ADAM_B1 ** ADAM_STEP)
    v_hat = v / (1.0 - ADAM_B2 ** ADAM_STEP)
    delta = -ADAM_LR * (m_hat / (jnp.sqrt(v_hat) + ADAM_EPS) + ADAM_WD * w)
    return delta, m, v


def adamw_sharded(name, parts, w, m, v, li, carried):
    depth, R, C = w.shape
    tr = _tile(R, 128, 16)
    n_in = 4 + (4 if carried else 0)

    def body(*refs):
        p_ref, w_ref, m_ref, v_ref = refs[:4]
        g_ref, d_ref, nm_ref, nv_ref = refs[n_in:n_in + 4]
        g = p_ref[0].astype(F32)
        for s in range(1, N_DEV):
            g = g + p_ref[s].astype(F32)
        d, nm, nv = _adamw(w_ref[...], g, m_ref[...], v_ref[...])
        g_ref[...] = g
        d_ref[...] = d
        nm_ref[...] = nm
        nv_ref[...] = nv

    row = pl.BlockSpec((None, tr, C), lambda i: (li, i, 0))
    est = 2 * _nbytes((N_DEV, tr, C), parts.dtype) + 20 * _nbytes((tr, C), F32)
    return pl.pallas_call(
        body, name=name, grid=(R // tr,),
        in_specs=[pl.BlockSpec((N_DEV, tr, C), lambda i: (0, i, 0)), row, row, row]
        + ([pl.BlockSpec(memory_space=pl.ANY)] * 4 if carried else []),
        out_specs=[row] * 4,
        out_shape=[jax.ShapeDtypeStruct((depth, R, C), F32)] * 4,
        input_output_aliases={4 + k: k for k in range(4)} if carried else {},
        compiler_params=_params(est, ("parallel",)),
    )(parts, w, m, v, *(carried or ()))


def _my_place():
    return lax.axis_index("x"), lax.axis_index("y"), lax.axis_index("c")


def all_gather_hbm(name, shard):
    R, W = shard.shape

    def body(x_ref, out_ref, send_sems, recv_sems, local_sem):
        x, y, c = _my_place()
        me, sibling = (x, y, c), (x, y, 1 - c)
        chips = [(1 - x, y), (x, 1 - y), (1 - x, 1 - y)]

        def slot(px, py, pc):
            return out_ref.at[4 * px + 2 * py + pc]

        def copy(k, block, to, src=None):
            return pltpu.make_async_remote_copy(
                src_ref=slot(*block) if src is None else src, dst_ref=slot(*block),
                send_sem=send_sems.at[k], recv_sem=recv_sems.at[k], device_id=to, device_id_type=MESH_T)

        mine = pltpu.make_async_copy(x_ref, slot(*me), local_sem)
        mine.start()
        first = [copy(0, me, sibling, src=x_ref)]
        first += [copy(1 + j, me, (*chip, c), src=x_ref) for j, chip in enumerate(chips)]
        for cp in first:
            cp.start()
        passed = [copy(4 + j, (*chip, c), sibling) for j, chip in enumerate(chips)]
        for j, chip in enumerate(chips):
            copy(1 + j, (*chip, c), me).wait_recv()
            passed[j].start()
        copy(0, sibling, me).wait_recv()
        for j, chip in enumerate(chips):
            copy(4 + j, (*chip, 1 - c), me).wait_recv()
        for cp in first + passed:
            cp.wait_send()
        mine.wait()

    return pl.pallas_call(
        body, name=name,
        out_shape=jax.ShapeDtypeStruct((N_DEV, R, W), shard.dtype),
        in_specs=[pl.BlockSpec(memory_space=pl.ANY)],
        out_specs=pl.BlockSpec(memory_space=pl.ANY),
        scratch_shapes=[pltpu.SemaphoreType.DMA((7,)), pltpu.SemaphoreType.DMA((7,)), pltpu.SemaphoreType.DMA],
    )(shard)


HBM_SPEC = pl.BlockSpec(memory_space=pltpu.HBM)
SEM_SPEC = pl.BlockSpec(memory_space=pltpu.SEMAPHORE)
DATAFLOW = pltpu.SideEffectType.DATAFLOW_SIDE_EFFECTING


def _me():
    x, y, c = _my_place()
    return 4 * x + 2 * y + c


def copies_start(name, src, gather, after=()):
    blk = src.shape if gather else src.shape[1:]
    mine = src[None] if gather else lax.dynamic_slice_in_dim(src, _me(), 1, axis=0)
    land = lax.dynamic_update_slice(lax.empty((N_DEV,) + tuple(blk), src.dtype), mine, (_me(), 0, 0))
    n_after = len(after)

    def body(*refs):
        src_ref, land_ref = refs[0], refs[1]
        send_sem, recv_sem = refs[2 + n_after], refs[3 + n_after]
        token = refs[-1]
        x, y, c = _my_place()
        me = 4 * x + 2 * y + c
        for k in range(1, N_DEV):
            px, py, pc = (x + (k >> 2)) % 2, (y + ((k >> 1) & 1)) % 2, (c + (k & 1)) % 2
            pltpu.make_async_remote_copy(
                src_ref=src_ref if gather else src_ref.at[4 * px + 2 * py + pc], dst_ref=land_ref.at[me],
                send_sem=send_sem, recv_sem=recv_sem, device_id=(px, py, pc), device_id_type=MESH_T).start()
        token[...] = jnp.zeros_like(token)

    return pl.pallas_call(
        body, name=name,
        out_shape=(pltpu.SemaphoreType.DMA(()), pltpu.SemaphoreType.DMA(()), pltpu.HBM(src.shape, src.dtype),
                   pltpu.HBM(land.shape, land.dtype), jax.ShapeDtypeStruct((8, 128), F32)),
        in_specs=(HBM_SPEC, HBM_SPEC) + (pl.BlockSpec(memory_space=pl.ANY),) * n_after,
        out_specs=(SEM_SPEC, SEM_SPEC, HBM_SPEC, HBM_SPEC, pl.BlockSpec(memory_space=pltpu.VMEM)),
        input_output_aliases={0: 2, 1: 3},
        compiler_params=pltpu.CompilerParams(has_side_effects=DATAFLOW),
    )(pltpu.with_memory_space_constraint(src, pltpu.HBM), pltpu.with_memory_space_constraint(land, pltpu.HBM), *after)


def copies_wait(name, started, after):
    send_sem, recv_sem, src_thru, land_thru, _ = started

    def body(src_ref, land_ref, send_sem, recv_sem, after_ref, src_dead, got_ref):
        seven = land_ref.at[pl.ds(0, N_DEV - 1)]
        all_seven = pltpu.make_async_remote_copy(src_ref=seven, dst_ref=seven, send_sem=send_sem, recv_sem=recv_sem,
                                                 device_id=_my_place(), device_id_type=MESH_T)
        all_seven.wait_send()
        all_seven.wait_recv()

    return pl.pallas_call(
        body, name=name,
        out_shape=(pltpu.HBM(src_thru.shape, src_thru.dtype), pltpu.HBM(land_thru.shape, land_thru.dtype)),
        in_specs=(HBM_SPEC, HBM_SPEC, SEM_SPEC, SEM_SPEC, pl.BlockSpec(memory_space=pl.ANY)),
        out_specs=(HBM_SPEC, HBM_SPEC), input_output_aliases={0: 0, 1: 1},
        compiler_params=pltpu.CompilerParams(has_side_effects=DATAFLOW),
    )(src_thru, land_thru, send_sem, recv_sem, after)[1]


def small_allreduce_adamw(part, w, m, v):
    R, W = part.shape

    def body(x_ref, w_ref, m_ref, v_ref, g_ref, d_ref, nm_ref, nv_ref, all_ref, send_sems, recv_sems, local_sem):
        x, y, c = _my_place()
        me, sibling = (x, y, c), (x, y, 1 - c)
        chips = [(1 - x, y), (x, 1 - y), (1 - x, 1 - y)]

        def slot(px, py, pc):
            return all_ref.at[4 * px + 2 * py + pc]

        def copy(k, block, to, src=None):
            return pltpu.make_async_remote_copy(
                src_ref=slot(*block) if src is None else src, dst_ref=slot(*block),
                send_sem=send_sems.at[k], recv_sem=recv_sems.at[k], device_id=to, device_id_type=MESH_T)

        mine = pltpu.make_async_copy(x_ref, slot(*me), local_sem)
        mine.start()
        first = [copy(0, me, sibling, src=x_ref)]
        first += [copy(1 + j, me, (*chip, c), src=x_ref) for j, chip in enumerate(chips)]
        for cp in first:
            cp.start()
        passed = [copy(4 + j, (*chip, c), sibling) for j, chip in enumerate(chips)]
        for j, chip in enumerate(chips):
            copy(1 + j, (*chip, c), me).wait_recv()
            passed[j].start()
        copy(0, sibling, me).wait_recv()
        for j, chip in enumerate(chips):
            copy(4 + j, (*chip, 1 - c), me).wait_recv()
        for cp in first + passed:
            cp.wait_send()
        mine.wait()
        g = all_ref[0]
        for s in range(1, N_DEV):
            g = g + all_ref[s]
        d, nm, nv = _adamw(w_ref[...], g, m_ref[...], v_ref[...])
        g_ref[...] = g
        d_ref[...] = d
        nm_ref[...] = nm
        nv_ref[...] = nv

    vm = pl.BlockSpec(memory_space=pltpu.VMEM)
    return pl.pallas_call(
        body, name="small_allreduce_adamw",
        out_shape=[jax.ShapeDtypeStruct((R, W), F32)] * 4,
        in_specs=[vm] * 4, out_specs=[vm] * 4,
        scratch_shapes=[pltpu.VMEM((N_DEV, R, W), F32), pltpu.SemaphoreType.DMA((7,)), pltpu.SemaphoreType.DMA((7,)),
                        pltpu.SemaphoreType.DMA],
    )(part, w, m, v)


SHARDED = ("w_in", "conv_w", "pool_w", "w_attn_br", "w_pool_br", "w_ssm_br", "w_out", "w_gate_up", "w_down")
REPLICATED = ("ln1_w", "attn_sink", "conv_b", "dt_bias", "a_log", "d_skip", "ssm_norm_w", "pool_scale", "ln2_w")
CONV_ROWS = 8


class Pending:
    def __init__(self, name, started, layout):
        self.name, self.started, self.layout = name, started, layout


def need(fw, n, after):
    if n not in fw:
        n_src = "w_in"
        fw.update(fw[n_src].layout(copies_wait(fw[n_src].name, fw[n_src].started, after)))
        del fw[n_src]
    elif isinstance(fw[n], Pending):
        fw[n] = fw[n].layout(copies_wait(fw[n].name, fw[n].started, after))
    return fw[n]


def _w_in_layout(dm, g_in):
    win = jnp.concatenate([g_in[d] for d in range(N_DEV)], axis=1)
    pts, acc = [], 0
    for wd in dm.in_widths:
        pts.append((acc, acc + wd))
        acc += wd
    cols = lambda k: win[:, pts[k][0]:pts[k][1]]
    fw = {"w_qkv": win[:, :pts[2][1]], "w_u": cols(3), "w_z": cols(4), "w_xbc": cols(5), "w_gl": cols(7)}
    fw["w_dt"] = jnp.pad(cols(6), ((0, 0), (0, dm.DTP - dm.H)))
    fw["w_in_int"] = jnp.concatenate([win[:, :pts[5][1]], fw["w_dt"], fw["w_gl"]], axis=1)
    return fw


def _gather_weights(dm, W, li, w_in_now, tokens):
    tag = f"ag_l{li}_"
    fw = {}

    def start(n, shard, layout):
        st = copies_start(tag + n + "_start", shard, True, after=tokens[-1:])
        tokens.append(st[4])
        fw[n] = Pending(tag + n + "_wait", st, layout)

    if w_in_now:
        g_in = all_gather_hbm(tag + "w_in", W["w_in"][li].astype(BF16))
        tokens.append(g_in)
        fw.update(_w_in_layout(dm, g_in))
    else:
        start("w_in", W["w_in"][li].astype(BF16), functools.partial(_w_in_layout, dm))
    start("pool_w", W["pool_w"][li].astype(BF16).reshape(dm.PG * dm.PC // N_DEV, dm.PC),
          lambda g: g.reshape(N_DEV, dm.PG, dm.PC // N_DEV, dm.PC).transpose(1, 0, 2, 3)
          .reshape(dm.PG, dm.PC, dm.PC).astype(F32))
    start("conv_w", jnp.pad(W["conv_w"][li], ((0, CONV_ROWS - CONV_K), (0, 0))),
          lambda g: g[:, :CONV_K].transpose(1, 0, 2).reshape(CONV_K, dm.CC))
    for n in ("w_attn_br", "w_pool_br", "w_ssm_br", "w_out", "w_gate_up", "w_down"):
        if n in ("w_ssm_br", "w_out", "w_down"):
            layout = lambda g: g.reshape(g.shape[0] * g.shape[1], g.shape[2])
        else:
            layout = lambda g: g
        start(n, W[n][li].astype(BF16), layout)
    return fw


def _grad_blocks(dm, n, g):
    if n == "w_in":
        o = [0]
        for wd in dm.seg:
            o.append(o[-1] + wd)
        ref_cols = jnp.concatenate([g[:, :o[4]], g[:, o[4]:o[4] + dm.H], g[:, o[5]:]], axis=1)
        per = dm.IN_COLS // N_DEV
        return jnp.stack([ref_cols[:, d * per:(d + 1) * per] for d in range(N_DEV)])
    if n in ("w_attn_br", "w_pool_br", "w_gate_up"):
        return g
    if n in ("w_ssm_br", "w_out", "w_down"):
        return g.reshape(N_DEV, g.shape[0] // N_DEV, g.shape[1])
    if n == "pool_w":
        g = g.reshape(dm.PG, N_DEV, dm.PC // N_DEV, dm.PC).transpose(1, 0, 2, 3)
        return g.reshape(N_DEV, dm.PG * dm.PC // N_DEV, dm.PC).astype(BF16)
    assert n == "conv_w"
    g = g.reshape(CONV_K, N_DEV, dm.CC // N_DEV).transpose(1, 0, 2)
    return jnp.pad(g, ((0, 0), (0, CONV_ROWS - CONV_K), (0, 0)))


def _as_rows(n, a):
    if n == "pool_w":
        return a.reshape(a.shape[0], a.shape[1] * a.shape[2], a.shape[3])
    if n == "conv_w":
        return jnp.pad(a, ((0, 0), (0, CONV_ROWS - CONV_K), (0, 0)))
    return a


def _from_rows(n, a, like):
    if n == "pool_w":
        return a.reshape(like.shape)
    if n == "conv_w":
        return a[:, :CONV_K]
    return a


def _size(shape):
    n = 1
    for s in shape:
        n *= s
    return n


def _layer_forward(dm, x, wts, rep, li):
    tag = f"l{li}_"
    sv = {"x": x}
    (h,) = row_call(tag + "rms1", f_rms, [x], [rep["ln1_w"]], [(dm.D, BF16)], 256)
    sv["h"] = h
    qkv = matmul(tag + "p_qkv", h, need(wts, "w_qkv", x), "nn", BF16)
    u = matmul(tag + "p_u", h, wts["w_u"], "nn", F32)
    z = matmul(tag + "p_z", h, wts["w_z"], "nn", F32)
    xbc = matmul(tag + "p_xbc", h, wts["w_xbc"], "nn", F32)
    dtr = matmul(tag + "p_dt", h, wts["w_dt"], "nn", F32)
    gl = matmul(tag + "p_gl", h, wts["w_gl"], "nn", BF16)
    sv.update(qkv=qkv, u=u, z=z, xbc=xbc, dtr=dtr, gl=gl)
    kvi = dm.AW // dm.KVW
    (att,) = halo_call(tag + "attn", f_attn,
                       [(qkv, dm.AW, 0, 0), (qkv, dm.KVW, kvi, WINDOW), (qkv, dm.KVW, kvi + 1, WINDOW)],
                       [rep["attn_sink"]], [(dm.AW, BF16)], WINDOW)
    (pool,) = halo_call(tag + "pool", f_pool, [(u, dm.PW, 0, POOL_HALO)],
                        [need(wts, "pool_w", att), rep["pool_scale"]], [(dm.PW, BF16)], 256)
    (xc,) = halo_call(tag + "conv", f_conv, [(xbc, dm.CC, 0, CONV_HALO)],
                      [need(wts, "conv_w", pool), rep["conv_b"]], [(dm.CC, F32)], 256)
    y, hts = ssd_fwd(tag + "ssd", xc, dtr, rep["dt_bias"], rep["a_log"], rep["d_skip"])
    (ssm,) = row_call(tag + "gnorm", f_gnorm, [y, z], [rep["ssm_norm_w"]], [(dm.DI, BF16)], 256)
    sv.update(att=att, pool=pool, xc=xc, y=y, hts=hts, ssm=ssm)
    ba = matmul(tag + "br_a", att, need(wts, "w_attn_br", ssm), "nn", BF16, b_blocked=True)
    bp = matmul(tag + "br_p", pool, need(wts, "w_pool_br", ba), "nn", BF16, b_blocked=True)
    bs = matmul(tag + "br_s", ssm, need(wts, "w_ssm_br", bp), "nn", BF16)
    (merged,) = row_call(tag + "merge", f_merge, [gl, ba, bp, bs], [], [(dm.D, BF16)], 256)
    x1 = matmul(tag + "out", merged, need(wts, "w_out", merged), "nn", F32, add=x)
    (h2,) = row_call(tag + "rms2", f_rms, [x1], [rep["ln2_w"]], [(dm.D, BF16)], 256)
    gu = matmul(tag + "gu", h2, need(wts, "w_gate_up", h2), "nn", BF16, b_blocked=True)
    (act,) = row_call(tag + "swiglu", f_swiglu, [gu], [], [(dm.DFF, BF16)], 256)
    x2 = matmul(tag + "down", act, need(wts, "w_down", act), "nn", F32, add=x1)
    sv.update(ba=ba, bp=bp, bs=bs, merged=merged, x1=x1, h2=h2, gu=gu, act=act)
    return x2, sv


class GradSink:
    def __init__(self, dm, li):
        self.dm, self.li, self.started, self.tokens = dm, li, {}, []

    def __setitem__(self, n, g):
        self.started[n] = copies_start(f"rs_l{self.li}_{n}_start", _grad_blocks(self.dm, n, g), False)
        self.tokens.append(self.started[n][4])

    def take(self):
        t, self.tokens = tuple(self.tokens), []
        return t


def _layer_backward(dm, dx2, sv, wts, rep, li, prev_tokens):
    tag = f"l{li}_b_"
    gw, gr = GradSink(dm, li), {}
    dact = matmul(tag + "d_act", dx2, wts["w_down"], "nt", BF16, after=prev_tokens)
    gw["w_down"] = matmul(tag + "g_down", sv["act"], dx2, "tn", BF16)
    (dgu,), _ = row_vjp_call(tag + "swiglu", f_swiglu, [sv["gu"]], [], [dact], [BF16], [], 256)
    dh2 = matmul(tag + "d_h2", dgu, wts["w_gate_up"], "nt", F32, b_blocked=True, after=gw.take())
    gw["w_gate_up"] = matmul(tag + "g_gu", sv["h2"], dgu, "tn", BF16, out_blocks=N_DEV)
    (dx1,), (gr["ln2_w"],) = row_vjp_call(tag + "rms2", f_rms, [sv["x1"]], [rep["ln2_w"]], [dh2], [F32], [True], 256,
                                          adds={0: dx2})
    dmerged = matmul(tag + "d_merged", dx1, wts["w_out"], "nt", BF16, after=gw.take())
    gw["w_out"] = matmul(tag + "g_out", sv["merged"], dx1, "tn", BF16)
    (dgl, dba, dbp, dbs), _ = row_vjp_call(tag + "merge", f_merge, [sv["gl"], sv["ba"], sv["bp"], sv["bs"]], [],
                                           [dmerged], [BF16, BF16, BF16, BF16], [], 256)
    datt = matmul(tag + "d_att", dba, wts["w_attn_br"], "nt", F32, b_blocked=True, after=gw.take())
    gw["w_attn_br"] = matmul(tag + "g_br_a", sv["att"], dba, "tn", BF16, out_blocks=N_DEV)
    dpool = matmul(tag + "d_pool", dbp, wts["w_pool_br"], "nt", F32, b_blocked=True, after=gw.take())
    gw["w_pool_br"] = matmul(tag + "g_br_p", sv["pool"], dbp, "tn", BF16, out_blocks=N_DEV)
    dssm = matmul(tag + "d_ssm", dbs, wts["w_ssm_br"], "nt", F32, after=gw.take())
    gw["w_ssm_br"] = matmul(tag + "g_br_s", sv["ssm"], dbs, "tn", BF16)
    (dy, dz), (gr["ssm_norm_w"],) = row_vjp_call(tag + "gnorm", f_gnorm, [sv["y"], sv["z"]], [rep["ssm_norm_w"]],
                                                 [dssm], [F32, BF16], [True], 256)
    dxc, ddtr, gr["dt_bias"], gr["a_log"], gr["d_skip"] = ssd_bwd(
        tag + "ssd", sv["xc"], sv["dtr"], sv["hts"], rep["dt_bias"], rep["a_log"], rep["d_skip"], dy)
    (dxbc,), (gw["conv_w"], gr["conv_b"]) = halo_vjp_call(
        tag + "conv", f_conv, [(sv["xbc"], dm.CC, 0, CONV_HALO)], [wts["conv_w"], rep["conv_b"]], [dxc],
        [BF16], [True, True], 256)
    (du,), (gw["pool_w"], gr["pool_scale"]) = halo_vjp_call(
        tag + "pool", f_pool, [(sv["u"], dm.PW, 0, POOL_HALO)], [wts["pool_w"], rep["pool_scale"]], [dpool],
        [BF16], [True, True], 256)
    kvi = dm.AW // dm.KVW
    qkv = sv["qkv"]
    (dq, dk, dv), (gr["attn_sink"],) = halo_vjp_call(
        tag + "attn", f_attn, [(qkv, dm.AW, 0, 0), (qkv, dm.KVW, kvi, WINDOW), (qkv, dm.KVW, kvi + 1, WINDOW)],
        [rep["attn_sink"]], [datt], [BF16, BF16, BF16], [True], WINDOW)
    dproj = jnp.concatenate([dq, dk, dv, du, dz, dxbc, ddtr, dgl], axis=1)
    gw["w_in"] = matmul(tag + "g_in", sv["h"], dproj, "tn", BF16, after=gw.take())
    dh = matmul(tag + "d_h", dproj, wts["w_in_int"], "nt", F32, after=gw.take())
    (dx,), (gr["ln1_w"],) = row_vjp_call(tag + "rms1", f_rms, [sv["x"]], [rep["ln1_w"]], [dh], [F32], [True], 256,
                                         adds={0: dx1})
    return dx, gw.started, gr, gw.take()


def kernel(x, ln1_w, w_in, attn_sink, conv_w, conv_b, dt_bias, a_log, d_skip, ssm_norm_w, pool_w, pool_scale, w_attn_br, w_pool_br, w_ssm_br, w_out, ln2_w, w_gate_up, w_down, final_w, loss_target, m_ln1_w, m_w_in, m_attn_sink, m_conv_w, m_conv_b, m_dt_bias, m_a_log, m_d_skip, m_ssm_norm_w, m_pool_w, m_pool_scale, m_w_attn_br, m_w_pool_br, m_w_ssm_br, m_w_out, m_ln2_w, m_w_gate_up, m_w_down, m_final_w, v_ln1_w, v_w_in, v_attn_sink, v_conv_w, v_conv_b, v_dt_bias, v_a_log, v_d_skip, v_ssm_norm_w, v_pool_w, v_pool_scale, v_w_attn_br, v_w_pool_br, v_w_ssm_br, v_w_out, v_ln2_w, v_w_gate_up, v_w_down, v_final_w):
    dm = Dims()
    W = dict(ln1_w=ln1_w, w_in=w_in, attn_sink=attn_sink, conv_w=conv_w, conv_b=conv_b, dt_bias=dt_bias, a_log=a_log,
             d_skip=d_skip, ssm_norm_w=ssm_norm_w, pool_w=pool_w, pool_scale=pool_scale, w_attn_br=w_attn_br,
             w_pool_br=w_pool_br, w_ssm_br=w_ssm_br, w_out=w_out, ln2_w=ln2_w, w_gate_up=w_gate_up, w_down=w_down,
             final_w=final_w)
    M = dict(ln1_w=m_ln1_w, w_in=m_w_in, attn_sink=m_attn_sink, conv_w=m_conv_w, conv_b=m_conv_b, dt_bias=m_dt_bias,
             a_log=m_a_log, d_skip=m_d_skip, ssm_norm_w=m_ssm_norm_w, pool_w=m_pool_w, pool_scale=m_pool_scale,
             w_attn_br=m_w_attn_br, w_pool_br=m_w_pool_br, w_ssm_br=m_w_ssm_br, w_out=m_w_out, ln2_w=m_ln2_w,
             w_gate_up=m_w_gate_up, w_down=m_w_down, final_w=m_final_w)
    V = dict(ln1_w=v_ln1_w, w_in=v_w_in, attn_sink=v_attn_sink, conv_w=v_conv_w, conv_b=v_conv_b, dt_bias=v_dt_bias,
             a_log=v_a_log, d_skip=v_d_skip, ssm_norm_w=v_ssm_norm_w, pool_w=v_pool_w, pool_scale=v_pool_scale,
             w_attn_br=v_w_attn_br, w_pool_br=v_w_pool_br, w_ssm_br=v_w_ssm_br, w_out=v_w_out, ln2_w=v_ln2_w,
             w_gate_up=v_w_gate_up, w_down=v_w_down, final_w=v_final_w)
    xl = x[0]
    target = loss_target[0]

    tokens = []
    full = [_gather_weights(dm, W, li, li == 0, tokens) for li in range(DEPTH)]
    rep = [{n: W[n][li].reshape(1, -1) for n in REPLICATED} for li in range(DEPTH)]
    rep[0]["ln1_w"] = rep[0]["ln1_w"] + tokens[-1][0, 0]

    saved = []
    xa = xl
    for li in range(DEPTH):
        xa, sv = _layer_forward(dm, xa, full[li], rep[li], li)
        saved.append(sv)
    dxa, g_final, loss_blk = loss_head(xa, W["final_w"].reshape(1, -1), target, 256)

    gws, grs, left = [None] * DEPTH, [None] * DEPTH, ()
    for li in reversed(range(DEPTH)):
        dxa, gws[li], grs[li], left = _layer_backward(dm, dxa, saved[li], full[li], rep[li], li, left)
    grad_x = dxa[None]

    sharded_out = {n: None for n in SHARDED}
    for li in reversed(range(DEPTH)):
        for n in SHARDED:
            parts = copies_wait(f"rs_l{li}_{n}_wait", gws[li][n], dxa)
            sharded_out[n] = adamw_sharded(f"adamw_l{li}_{n}", parts, _as_rows(n, W[n]), _as_rows(n, M[n]),
                                           _as_rows(n, V[n]), li, sharded_out[n])
    g_sh, d_sh, m_sh, v_sh = [{n: _from_rows(n, sharded_out[n][k], W[n]) for n in SHARDED} for k in range(4)]

    small_names = [(n, li) for li in range(DEPTH) for n in REPLICATED] + [("final_w", None)]
    small_shape = lambda n, li: W[n].shape if li is None else W[n].shape[1:]
    small_rows = [-(-_size(small_shape(n, li)) // (8 * 128)) * 8 for n, li in small_names]
    loss_row = sum(small_rows)

    def small_pack(get, last=None):
        rows = []
        for (n, li), r in zip(small_names, small_rows):
            a = get(n, li).reshape(-1).astype(F32)
            rows.append(jnp.pad(a, (0, r * 128 - a.size)).reshape(r, 128))
        rows.append(jnp.zeros((8, 128), F32) if last is None else last)
        return jnp.concatenate(rows, axis=0)

    part = small_pack(lambda n, li: g_final if li is None else grs[li][n], loss_blk)
    wsm = small_pack(lambda n, li: W[n] if li is None else W[n][li])
    msm = small_pack(lambda n, li: M[n] if li is None else M[n][li])
    vsm = small_pack(lambda n, li: V[n] if li is None else V[n][li])
    sm = small_allreduce_adamw(part, wsm, msm, vsm)
    loss = sm[0][loss_row, 0]

    def small_unpack(buf):
        out, r0 = {}, 0
        for (n, li), r in zip(small_names, small_rows):
            shp = small_shape(n, li)
            out[(n, li)] = buf[r0:r0 + r].reshape(-1)[:_size(shp)].reshape(shp)
            r0 += r
        return out

    sm_g, sm_d, sm_m, sm_v = [small_unpack(b) for b in sm]

    def assemble(sharded_list, small):
        outs = []
        for n in ("ln1_w", "w_in", "attn_sink", "conv_w", "conv_b", "dt_bias", "a_log", "d_skip", "ssm_norm_w", "pool_w",
                  "pool_scale", "w_attn_br", "w_pool_br", "w_ssm_br", "w_out", "ln2_w", "w_gate_up", "w_down"):
            if n in SHARDED:
                outs.append(sharded_list[n])
            else:
                outs.append(jnp.stack([small[(n, li)] for li in range(DEPTH)]))
        outs.append(small[("final_w", None)])
        return outs

    return (loss, grad_x, *assemble(g_sh, sm_g), *assemble(d_sh, sm_d), *assemble(m_sh, sm_m), *assemble(v_sh, sm_v))
```

```python
import functools

import jax
import jax.numpy as jnp
from jax import lax
from jax.experimental import pallas as pl
from jax.experimental.pallas import tpu as pltpu

D_MODEL = 2048
DEPTH = 2
ATT_HEAD_DIM = 64
ATT_Q_HEADS = 16
ATT_KV_HEADS = 4
WINDOW = 128
POOL_WINDOWS = (2, 4, 8, 16)
POOL_WIDTH = D_MODEL // 2
D_INNER = D_MODEL
SSM_HEAD_DIM = 64
SSM_GROUPS = 4
D_STATE = 128
CONV_K = 4
CHUNK = 128
N_BRANCH = 3
D_FF = 5632
EPS = 1e-6

ADAM_LR = 0.001
ADAM_B1 = 0.9
ADAM_B2 = 0.999
ADAM_EPS = 1e-08
ADAM_WD = 0.01
ADAM_STEP = 10

N_DEV = 8
F32 = jnp.float32
BF16 = jnp.bfloat16
MXU_DTYPE = jnp.bfloat16
NEG = -1e30
VMEM_CAP = 60 * 2**20
CONV_HALO = 8
POOL_HALO = 16
MESH_T = pl.DeviceIdType.MESH


class Dims:
    def __init__(self):
        self.D = D_MODEL
        self.AW = ATT_Q_HEADS * ATT_HEAD_DIM
        self.KVW = ATT_KV_HEADS * ATT_HEAD_DIM
        self.GQ = ATT_Q_HEADS // ATT_KV_HEADS
        self.PW = POOL_WIDTH
        self.PG = len(POOL_WINDOWS)
        self.PC = POOL_WIDTH // len(POOL_WINDOWS)
        self.DI = D_INNER
        self.H = D_INNER // SSM_HEAD_DIM
        self.P = SSM_HEAD_DIM
        self.G = SSM_GROUPS
        self.HPG = self.H // SSM_GROUPS
        self.GW = D_INNER // SSM_GROUPS
        self.N = D_STATE
        self.GN = SSM_GROUPS * D_STATE
        self.CC = D_INNER + 2 * SSM_GROUPS * D_STATE
        self.DTP = -(-self.H // 256) * 256
        self.DFF = D_FF
        self.in_widths = (self.AW, self.KVW, self.KVW, self.PW, self.DI, self.CC, self.H, N_BRANCH * self.D)
        self.IN_COLS = sum(self.in_widths)
        self.QKV = self.AW + 2 * self.KVW
        self.seg = (self.QKV, self.PW, self.DI, self.CC, self.DTP, N_BRANCH * self.D)
        self.IN_INT = sum(self.seg)


def _tile(n, cap, mult=128):
    if n <= cap:
        return n
    best = 0
    for t in range(mult, cap + 1, mult):
        if n % t == 0:
            best = t
    assert best, (n, cap, mult)
    return best


def _nbytes(shape, dtype):
    n = 1
    for s in shape:
        n *= s
    return n * jnp.dtype(dtype).itemsize


def _params(est_bytes, sem=None):
    limit = int(min(VMEM_CAP, max(32 * 2**20, est_bytes * 3 // 2 + 8 * 2**20)))
    kw = dict(vmem_limit_bytes=limit)
    if sem is not None:
        kw["dimension_semantics"] = sem
    return pltpu.CompilerParams(**kw)


def _raw_dot(a, b, ca, cb):
    return lax.dot_general(a.astype(MXU_DTYPE), b.astype(MXU_DTYPE), (((ca,), (cb,)), ((), ())),
                           preferred_element_type=F32)


@functools.partial(jax.custom_vjp, nondiff_argnums=(2, 3))
def bdot(a, b, ca, cb):
    return _raw_dot(a, b, ca, cb)


def _bdot_fwd(a, b, ca, cb):
    return _raw_dot(a, b, ca, cb), (a, b)


def _bdot_bwd(ca, cb, res, g):
    a, b = res
    if (ca, cb) == (1, 0):
        da, db = bdot(g, b, 1, 1), bdot(a, g, 0, 0)
    elif (ca, cb) == (1, 1):
        da, db = bdot(g, b, 1, 0), bdot(g, a, 0, 0)
    else:
        assert (ca, cb) == (0, 0)
        da, db = bdot(b, g, 1, 1), bdot(a, g, 1, 0)
    return da.astype(a.dtype), db.astype(b.dtype)


bdot.defvjp(_bdot_fwd, _bdot_bwd)


def hdot(a, b, ca=1, cb=0):
    return lax.dot_general(a, b, (((ca,), (cb,)), ((), ())), precision=lax.Precision.HIGHEST,
                           preferred_element_type=F32)


def _split_dot(x, e, ca, cb):
    hi = x.astype(BF16)
    lo = (x - hi.astype(F32)).astype(BF16)
    dn = (((ca,), (cb,)), ((), ()))
    eb = e.astype(BF16)
    return (lax.dot_general(hi, eb, dn, preferred_element_type=F32)
            + lax.dot_general(lo, eb, dn, preferred_element_type=F32))


@jax.custom_vjp
def edot(x, e):
    return _split_dot(x, e, 1, 0)


def _edot_fwd(x, e):
    return _split_dot(x, e, 1, 0), e


def _edot_bwd(e, g):
    return _split_dot(g, e, 1, 1), jnp.zeros_like(e)


edot.defvjp(_edot_fwd, _edot_bwd)


def _silu(x):
    return x * jax.nn.sigmoid(x)


def _softplus(x):
    return jnp.maximum(x, 0.0) + jnp.log1p(jnp.exp(-jnp.abs(x)))


def matmul(name, a, b, mode, out_dtype, add=None, b_blocked=False, out_blocks=0, after=()):
    if b_blocked:
        nb, rows, n = b.shape
        b_rows, b_cols = rows, nb * n
    else:
        b_rows, b_cols = b.shape
    if mode == "nn":
        (M, K), (K2, N) = a.shape, (b_rows, b_cols)
    elif mode == "nt":
        (M, K), (N, K2) = a.shape, (b_rows, b_cols)
    else:
        (K, M), (K2, N) = a.shape, (b_rows, b_cols)
    assert K == K2, (name, a.shape, b.shape, mode)
    n_blk = n if b_blocked else (N // out_blocks if out_blocks else 0)
    tm = _tile(M, 1024)
    tn = _tile(n_blk if (n_blk and mode != "nt") else N, 1536)
    tk = _tile(n_blk if (n_blk and mode == "nt") else K, 2048)
    nk = K // tk
    dims = {"nn": (1, 0), "nt": (1, 1), "tn": (0, 0)}[mode]
    a_spec = (pl.BlockSpec((tk, tm), lambda i, j, k: (k, i)) if mode == "tn"
              else pl.BlockSpec((tm, tk), lambda i, j, k: (i, k)))
    if b_blocked and mode == "nt":
        per = n // tk
        b_spec = pl.BlockSpec((None, tn, tk), lambda i, j, k: (k // per, j, k % per))
    elif b_blocked:
        per = n // tn
        b_spec = pl.BlockSpec((None, tk, tn), lambda i, j, k: (j // per, k, j % per))
    elif mode == "nt":
        b_spec = pl.BlockSpec((tn, tk), lambda i, j, k: (j, k))
    else:
        b_spec = pl.BlockSpec((tk, tn), lambda i, j, k: (k, j))
    o_spec = pl.BlockSpec((tm, tn), lambda i, j, k: (i, j))
    out_shape = jax.ShapeDtypeStruct((M, N), out_dtype)
    if out_blocks:
        assert mode == "tn" and add is None
        per_o = n_blk // tn
        o_spec = pl.BlockSpec((None, tm, tn), lambda i, j, k: (j // per_o, i, j % per_o))
        out_shape = jax.ShapeDtypeStruct((out_blocks, M, n_blk), out_dtype)
    has_add = add is not None

    def body(*refs):
        a_ref, b_ref = refs[0], refs[1]
        c_ref = refs[2] if has_add else None
        o_ref = refs[2 + has_add + len(after)]
        p = _raw_dot(a_ref[...], b_ref[...], *dims)
        if nk == 1:
            if has_add:
                p = p + c_ref[...].astype(F32)
            o_ref[...] = p.astype(o_ref.dtype)
            return
        acc = refs[-1]
        k = pl.program_id(2)

        @pl.when(k == 0)
        def _():
            acc[...] = p + c_ref[...].astype(F32) if has_add else p

        @pl.when(k > 0)
        def _():
            acc[...] += p

        @pl.when(k == nk - 1)
        def _():
            o_ref[...] = acc[...].astype(o_ref.dtype)

    est = 2 * (_nbytes((tm, tk), a.dtype) + _nbytes((tk, tn), b.dtype) + _nbytes((tm, tn), out_dtype))
    est += 3 * _nbytes((tm, tn), F32) + _nbytes((tm, tk), MXU_DTYPE) + _nbytes((tk, tn), MXU_DTYPE)
    if has_add:
        est += 2 * _nbytes((tm, tn), add.dtype)
    args = (a, b) + ((add,) if has_add else ()) + tuple(after)
    in_specs = [a_spec, b_spec] + ([o_spec] if has_add else []) + [pl.BlockSpec(memory_space=pl.ANY)] * len(after)
    return pl.pallas_call(
        body, name=name, grid=(M // tm, N // tn, nk),
        in_specs=in_specs, out_specs=o_spec, out_shape=out_shape,
        scratch_shapes=[pltpu.VMEM((tm, tn), F32)] if nk > 1 else [],
        compiler_params=_params(est, ("parallel", "parallel", "arbitrary")),
    )(*args)


def row_call(name, f, rows, pars, outs, tm):
    L = rows[0].shape[0]
    tm = min(tm, L)
    nr, npar = len(rows), len(pars)

    def body(*refs):
        vals = [r[...] for r in refs[:nr + npar]]
        res = f(*vals)
        for o_ref, v in zip(refs[nr + npar:], res):
            o_ref[...] = v.astype(o_ref.dtype)

    est = 2 * sum(_nbytes((tm, a.shape[1]), a.dtype) for a in rows) + 2 * sum(_nbytes((tm, w), d) for w, d in outs)
    est += 4 * sum(_nbytes((tm, a.shape[1]), F32) for a in rows)
    res = pl.pallas_call(
        body, name=name, grid=(L // tm,),
        in_specs=[pl.BlockSpec((tm, a.shape[1]), lambda i: (i, 0)) for a in rows]
        + [pl.BlockSpec(p.shape, lambda i: (0, 0)) for p in pars],
        out_specs=[pl.BlockSpec((tm, w), lambda i: (i, 0)) for w, _ in outs],
        out_shape=[jax.ShapeDtypeStruct((L, w), d) for w, d in outs],
        compiler_params=_params(est, ("parallel",)),
    )(*rows, *pars)
    return res


def row_vjp_call(name, f, rows, pars, cots, row_grad_dtypes, par_grads, tm, adds=None):
    L = rows[0].shape[0]
    tm = min(tm, L)
    nr, npar, nc = len(rows), len(pars), len(cots)
    adds = adds or {}
    add_idx = sorted(adds)
    rg_idx = [i for i, d in enumerate(row_grad_dtypes) if d is not None]
    pg_idx = [i for i, w in enumerate(par_grads) if w]
    diff_idx = rg_idx + [nr + i for i in pg_idx]

    def body(*refs):
        vals = [r[...] for r in refs[:nr + npar]]
        cvals = [r[...] for r in refs[nr + npar:nr + npar + nc]]
        avals = [r[...] for r in refs[nr + npar + nc:nr + npar + nc + len(add_idx)]]
        out_refs = refs[nr + npar + nc + len(add_idx):]

        def g(*dv):
            full = list(vals)
            for i, v in zip(diff_idx, dv):
                full[i] = v
            return tuple(f(*full))

        res, vjp = jax.vjp(g, *[vals[i] for i in diff_idx])
        grads = vjp(tuple(c.astype(r.dtype) for c, r in zip(cvals, res)))
        for n, i in enumerate(rg_idx):
            gval = grads[n].astype(F32)
            if i in adds:
                gval = gval + avals[add_idx.index(i)].astype(F32)
            out_refs[n][...] = gval.astype(out_refs[n].dtype)
        first = pl.program_id(0) == 0
        for n, i in enumerate(pg_idx):
            o_ref = out_refs[len(rg_idx) + n]
            gval = grads[len(rg_idx) + n].astype(F32)

            @pl.when(first)
            def _(o_ref=o_ref, gval=gval):
                o_ref[...] = gval

            @pl.when(jnp.logical_not(first))
            def _(o_ref=o_ref, gval=gval):
                o_ref[...] += gval

    row_spec = lambda a: pl.BlockSpec((tm, a.shape[1]), lambda i: (i, 0))
    est = 2 * sum(_nbytes((tm, a.shape[1]), a.dtype) for a in list(rows) + list(cots))
    est += 10 * sum(_nbytes((tm, a.shape[1]), F32) for a in rows)
    res = pl.pallas_call(
        body, name=name, grid=(L // tm,),
        in_specs=[row_spec(a) for a in rows] + [pl.BlockSpec(p.shape, lambda i: (0, 0)) for p in pars]
        + [row_spec(c) for c in cots] + [row_spec(adds[i]) for i in add_idx],
        out_specs=[row_spec(rows[i]) for i in rg_idx] + [pl.BlockSpec(pars[i].shape, lambda i_: (0, 0)) for i in pg_idx],
        out_shape=[jax.ShapeDtypeStruct(rows[i].shape, row_grad_dtypes[i]) for i in rg_idx]
        + [jax.ShapeDtypeStruct(pars[i].shape, F32) for i in pg_idx],
        compiler_params=_params(est, ("arbitrary",)),
    )(*rows, *pars, *cots, *[adds[i] for i in add_idx])
    return list(res[:len(rg_idx)]), list(res[len(rg_idx):])


def _halo_specs(rows, T, nt, rev):
    specs = []
    for (_, w, ci, hs) in rows:
        if rev:
            specs.append(pl.BlockSpec((T, w), lambda j, ci=ci: (nt - 1 - j, ci)))
        else:
            specs.append(pl.BlockSpec((T, w), lambda i, ci=ci: (i, ci)))
        if hs:
            r = T // hs
            if rev:
                specs.append(pl.BlockSpec((hs, w), lambda j, ci=ci, r=r: (jnp.maximum((nt - 1 - j) * r - 1, 0), ci)))
            else:
                specs.append(pl.BlockSpec((hs, w), lambda i, ci=ci, r=r: (jnp.maximum(i * r - 1, 0), ci)))
    return specs


def _halo_args(rows):
    args = []
    for (a, _, _, hs) in rows:
        args.append(a)
        if hs:
            args.append(a)
    return args


def _halo_vals(rows, refs):
    vals, n = [], 0
    for (_, _, _, hs) in rows:
        if hs:
            vals.append((refs[n + 1][...], refs[n][...]))
            n += 2
        else:
            vals.append(refs[n][...])
            n += 1
    return vals, n


def halo_call(name, f, rows, pars, outs, T):
    L = rows[0][0].shape[0]
    T = min(T, L)
    nt = L // T
    npar = len(pars)

    def body(*refs):
        i = pl.program_id(0)
        vals, n = _halo_vals(rows, refs)
        pv = [r[...] for r in refs[n:n + npar]]
        res = f(i == 0, i * T, *vals, *pv)
        for o_ref, v in zip(refs[n + npar:], res):
            o_ref[...] = v.astype(o_ref.dtype)

    est = 2 * sum(_nbytes((T, w), a.dtype) for a, w, _, _ in rows) + 2 * sum(_nbytes((T, w), d) for w, d in outs)
    est += 8 * sum(_nbytes((T, w), F32) for _, w, _, _ in rows)
    return pl.pallas_call(
        body, name=name, grid=(nt,),
        in_specs=_halo_specs(rows, T, nt, False) + [pl.BlockSpec(p.shape, lambda i, nd=p.ndim: (0,) * nd) for p in pars],
        out_specs=[pl.BlockSpec((T, w), lambda i: (i, 0)) for w, _ in outs],
        out_shape=[jax.ShapeDtypeStruct((L, w), d) for w, d in outs],
        compiler_params=_params(est, ("parallel",)),
    )(*_halo_args(rows), *pars)


def halo_vjp_call(name, f, rows, pars, cots, row_grad_dtypes, par_grads, T):
    L = rows[0][0].shape[0]
    T = min(T, L)
    nt = L // T
    nr, npar, nc = len(rows), len(pars), len(cots)
    pg_idx = [i for i, w in enumerate(par_grads) if w]
    halo_idx = [i for i, r in enumerate(rows) if r[3]]

    def body(*refs):
        j = pl.program_id(0)
        i = nt - 1 - j
        vals, n = _halo_vals(rows, refs)
        pv = [r[...] for r in refs[n:n + npar]]
        cv = [r[...] for r in refs[n + npar:n + npar + nc]]
        out_refs = refs[n + npar + nc:n + npar + nc + nr + len(pg_idx)]
        carries = refs[n + npar + nc + nr + len(pg_idx):]
        first = i == 0

        def g(vals_, pv_):
            return tuple(f(first, i * T, *vals_, *pv_))

        res, vjp = jax.vjp(g, vals, pv)
        dvals, dpv = vjp(tuple(c.astype(r.dtype) for c, r in zip(cv, res)))

        @pl.when(j == 0)
        def _():
            for c_ref in carries:
                c_ref[...] = jnp.zeros_like(c_ref)

        for k in range(nr):
            hs = rows[k][3]
            o_ref = out_refs[k]
            if hs:
                dh, dc = dvals[k]
                c_ref = carries[halo_idx.index(k)]
                dc = dc.astype(F32)
                if hs == T:
                    o_ref[...] = (dc + c_ref[...]).astype(o_ref.dtype)
                else:
                    o_ref[0:T - hs, :] = dc[0:T - hs].astype(o_ref.dtype)
                    o_ref[T - hs:T, :] = (dc[T - hs:T] + c_ref[...]).astype(o_ref.dtype)
                c_ref[...] = dh.astype(F32)
            else:
                o_ref[...] = dvals[k].astype(o_ref.dtype)
        for m, k in enumerate(pg_idx):
            o_ref = out_refs[nr + m]
            gval = dpv[k].astype(F32)

            @pl.when(j == 0)
            def _(o_ref=o_ref, gval=gval):
                o_ref[...] = gval

            @pl.when(j > 0)
            def _(o_ref=o_ref, gval=gval):
                o_ref[...] += gval

    est = 2 * sum(_nbytes((T, w), a.dtype) for a, w, _, _ in rows) + 2 * sum(_nbytes((T, c.shape[1]), c.dtype) for c in cots)
    est += 12 * sum(_nbytes((T, w), F32) for _, w, _, _ in rows)
    res = pl.pallas_call(
        body, name=name, grid=(nt,),
        in_specs=_halo_specs(rows, T, nt, True) + [pl.BlockSpec(p.shape, lambda j, nd=p.ndim: (0,) * nd) for p in pars]
        + [pl.BlockSpec((T, c.shape[1]), lambda j: (nt - 1 - j, 0)) for c in cots],
        out_specs=[pl.BlockSpec((T, w), lambda j: (nt - 1 - j, 0)) for _, w, _, _ in rows]
        + [pl.BlockSpec(pars[k].shape, lambda j, nd=pars[k].ndim: (0,) * nd) for k in pg_idx],
        out_shape=[jax.ShapeDtypeStruct((L, w), row_grad_dtypes[k]) for k, (_, w, _, _) in enumerate(rows)]
        + [jax.ShapeDtypeStruct(pars[k].shape, F32) for k in pg_idx],
        scratch_shapes=[pltpu.VMEM((rows[k][3], rows[k][1]), F32) for k in halo_idx],
        compiler_params=_params(est, ("arbitrary",)),
    )(*_halo_args(rows), *pars, *cots)
    return list(res[:nr]), list(res[nr:])


def f_rms(x, w):
    x = x.astype(F32)
    return (x * lax.rsqrt(jnp.mean(x * x, axis=-1, keepdims=True) + EPS) * w,)


def f_swiglu(gu):
    dff = gu.shape[1] // 2
    gu = gu.astype(F32)
    return (_silu(gu[:, :dff]) * gu[:, dff:],)


def f_merge(gl, a, p, s):
    d = a.shape[1]
    g = jax.nn.sigmoid(gl.astype(F32))
    return (g[:, :d] * a.astype(F32) + g[:, d:2 * d] * p.astype(F32) + g[:, 2 * d:] * s.astype(F32),)


def f_gnorm(y, z, nw):
    dm = Dims()
    g = y * _silu(z.astype(F32))
    outs = []
    for gi in range(dm.G):
        gg = g[:, gi * dm.GW:(gi + 1) * dm.GW]
        outs.append(gg * lax.rsqrt(jnp.mean(gg * gg, axis=-1, keepdims=True) + EPS))
    return (jnp.concatenate(outs, axis=1) * nw,)


def f_conv(first, row0, xs, w, b):
    halo, cur = xs
    halo = jnp.where(first, 0.0, halo)
    ext = jnp.concatenate([halo, cur], axis=0)
    T = cur.shape[0]
    base = CONV_HALO - (CONV_K - 1)
    pre = b
    for k in range(CONV_K):
        pre = pre + w[k:k + 1, :] * ext[base + k:base + k + T]
    return (_silu(pre),)


def f_pool(first, row0, us, pw, scale):
    dm = Dims()
    halo, cur = us
    halo = jnp.where(first, 0.0, halo)
    ext = jnp.concatenate([halo, cur], axis=0)
    T = cur.shape[0]
    t = row0 + lax.broadcasted_iota(jnp.int32, (T, 1), 0)
    outs = []
    for gi, w in enumerate(POOL_WINDOWS):
        assert w & (w - 1) == 0 and w <= POOL_HALO
        s = ext[:, gi * dm.PC:(gi + 1) * dm.PC]
        sh = 1
        while sh < w:
            s = s + jnp.concatenate([jnp.zeros((sh, dm.PC), F32), s[:-sh]], axis=0)
            sh *= 2
        cnt = jnp.minimum(t + 1, w).astype(F32)
        mixed = s[POOL_HALO:] / cnt - cur[:, gi * dm.PC:(gi + 1) * dm.PC]
        outs.append(bdot(mixed, pw[gi], 1, 0))
    return (jnp.concatenate(outs, axis=1) * scale,)


def f_attn(first, row0, q, ks, vs, sink):
    dm = Dims()
    assert WINDOW == q.shape[0]
    kp, kc = ks
    vp, vc = vs
    T = q.shape[0]
    hd = ATT_HEAD_DIM
    cur1 = lax.broadcasted_iota(jnp.int32, (T, T), 1) <= lax.broadcasted_iota(jnp.int32, (T, T), 0)
    cur = jnp.concatenate([cur1] * dm.GQ, axis=0)
    valid = cur | jnp.logical_not(first)
    outs = []
    for k in range(ATT_KV_HEADS):
        ksl = slice(k * hd, (k + 1) * hd)
        heads = [k * dm.GQ + g for g in range(dm.GQ)]
        qs = jnp.concatenate([q[:, h * hd:(h + 1) * hd] for h in heads], axis=0)
        s = jnp.where(cur, bdot(qs, kc[:, ksl], 1, 1), bdot(qs, kp[:, ksl], 1, 1)) * (hd ** -0.5)
        s = jnp.where(valid, s, NEG)
        sk = jnp.concatenate([jnp.broadcast_to(sink[:, h:h + 1], (T, 1)) for h in heads], axis=0)
        m = jnp.maximum(jnp.max(s, axis=-1, keepdims=True), sk)
        p = jnp.exp(s - m)
        den = jnp.sum(p, axis=-1, keepdims=True) + jnp.exp(sk - m)
        pn = p / den
        o = bdot(jnp.where(cur, pn, 0.0), vc[:, ksl], 1, 0) + bdot(jnp.where(cur, 0.0, pn), vp[:, ksl], 1, 0)
        outs += [o[g * T:(g + 1) * T] for g in range(dm.GQ)]
    return (jnp.concatenate(outs, axis=1),)


def f_ssd(xc, dtr, ht, dt_bias, a_log, d_skip):
    dm = Dims()
    Q = xc.shape[0]
    xs = xc[:, :dm.DI]
    bm = xc[:, dm.DI:dm.DI + dm.GN]
    cm = xc[:, dm.DI + dm.GN:]
    expand = (lax.broadcasted_iota(jnp.int32, (dm.H, dm.DI), 1) // dm.P
              == lax.broadcasted_iota(jnp.int32, (dm.H, dm.DI), 0)).astype(F32)
    ri = lax.broadcasted_iota(jnp.int32, (Q, Q), 0)
    ci = lax.broadcasted_iota(jnp.int32, (Q, Q), 1)
    causal = ri >= ci
    tinc = causal.astype(F32)
    dt = _softplus(dtr[:, :dm.H] + dt_bias)
    da = dt * (-jnp.exp(a_log))
    acs = hdot(tinc, da)
    acs_t = hdot(da, tinc, 0, 1)
    eacs = jnp.exp(acs)
    dend = jnp.exp(acs[Q - 1:Q, :] - acs)
    ex = edot(jnp.concatenate([dt, eacs, dend, jnp.broadcast_to(d_skip, (8, dm.H))], axis=0), expand)
    dt_x, eacs_x, dend_x, dsk_x = ex[:Q], ex[Q:2 * Q], ex[2 * Q:3 * Q], ex[3 * Q:3 * Q + 1]
    xdt = xs * dt_x
    ys, hts = [], []
    for g in range(dm.G):
        gs = slice(g * dm.GW, (g + 1) * dm.GW)
        bg = bm[:, g * dm.N:(g + 1) * dm.N]
        cg = cm[:, g * dm.N:(g + 1) * dm.N]
        cb = bdot(cg, bg, 1, 1)
        y_off = bdot(cg, ht[:, gs], 1, 0) * eacs_x[:, gs]
        xg = xdt[:, gs]
        st = bdot(bg, xg * dend_x[:, gs], 0, 0)
        hts.append(ht[:, gs] * eacs_x[Q - 1:Q, gs] + st)
        yd = []
        for e in range(dm.HPG):
            h = g * dm.HPG + e
            seg = acs[:, h:h + 1] - acs_t[h:h + 1, :]
            lm = jnp.exp(jnp.where(causal, seg, NEG))
            yd.append(bdot(cb * lm, xg[:, e * dm.P:(e + 1) * dm.P], 1, 0))
        ys.append(jnp.concatenate(yd, axis=1) + y_off)
    y = jnp.concatenate(ys, axis=1) + dsk_x * xs
    return y, jnp.concatenate(hts, axis=1)


def ssd_fwd(name, xc, dtr, dt_bias, a_log, d_skip):
    dm = Dims()
    L = xc.shape[0]
    Q = CHUNK
    nc = L // Q

    def body(xc_ref, dtr_ref, b_ref, a_ref, s_ref, y_ref, hts_ref, ht):
        @pl.when(pl.program_id(0) == 0)
        def _():
            ht[...] = jnp.zeros_like(ht)

        h0 = ht[...]
        hts_ref[0] = h0
        y, h1 = f_ssd(xc_ref[...], dtr_ref[...], h0, b_ref[...], a_ref[...], s_ref[...])
        y_ref[...] = y
        ht[...] = h1

    par = pl.BlockSpec((1, dm.H), lambda c: (0, 0))
    est = 40 * _nbytes((Q, dm.CC), F32) + 4 * _nbytes((dm.N, dm.DI), F32)
    return pl.pallas_call(
        body, name=name, grid=(nc,),
        in_specs=[pl.BlockSpec((Q, dm.CC), lambda c: (c, 0)), pl.BlockSpec((Q, dm.DTP), lambda c: (c, 0)), par, par, par],
        out_specs=[pl.BlockSpec((Q, dm.DI), lambda c: (c, 0)), pl.BlockSpec((1, dm.N, dm.DI), lambda c: (c, 0, 0))],
        out_shape=[jax.ShapeDtypeStruct((L, dm.DI), F32), jax.ShapeDtypeStruct((nc, dm.N, dm.DI), F32)],
        scratch_shapes=[pltpu.VMEM((dm.N, dm.DI), F32)],
        compiler_params=_params(est, ("arbitrary",)),
    )(xc, dtr, dt_bias, a_log, d_skip)


def ssd_bwd(name, xc, dtr, hts, dt_bias, a_log, d_skip, dy):
    dm = Dims()
    L = xc.shape[0]
    Q = CHUNK
    nc = L // Q

    def body(xc_ref, dtr_ref, hts_ref, b_ref, a_ref, s_ref, dy_ref, dxc_ref, ddtr_ref, db_ref, da_ref, ds_ref, dht):
        j = pl.program_id(0)

        @pl.when(j == 0)
        def _():
            dht[...] = jnp.zeros_like(dht)

        _, vjp = jax.vjp(f_ssd, xc_ref[...], dtr_ref[...], hts_ref[0], b_ref[...], a_ref[...], s_ref[...])
        dxc, ddtr, dh0, db, da, ds = vjp((dy_ref[...], dht[...]))
        dxc_ref[...] = dxc.astype(dxc_ref.dtype)
        ddtr_ref[...] = ddtr.astype(ddtr_ref.dtype)
        dht[...] = dh0
        for o_ref, gval in ((db_ref, db), (da_ref, da), (ds_ref, ds)):
            @pl.when(j == 0)
            def _(o_ref=o_ref, gval=gval):
                o_ref[...] = gval

            @pl.when(j > 0)
            def _(o_ref=o_ref, gval=gval):
                o_ref[...] += gval

    par = pl.BlockSpec((1, dm.H), lambda j: (0, 0))
    rev = lambda w: pl.BlockSpec((Q, w), lambda j: (nc - 1 - j, 0))
    est = 80 * _nbytes((Q, dm.CC), F32) + 6 * _nbytes((dm.N, dm.DI), F32)
    return pl.pallas_call(
        body, name=name, grid=(nc,),
        in_specs=[rev(dm.CC), rev(dm.DTP), pl.BlockSpec((1, dm.N, dm.DI), lambda j: (nc - 1 - j, 0, 0)), par, par, par, rev(dm.DI)],
        out_specs=[rev(dm.CC), rev(dm.DTP), par, par, par],
        out_shape=[jax.ShapeDtypeStruct((L, dm.CC), F32), jax.ShapeDtypeStruct((L, dm.DTP), BF16)]
        + [jax.ShapeDtypeStruct((1, dm.H), F32)] * 3,
        scratch_shapes=[pltpu.VMEM((dm.N, dm.DI), F32)],
        compiler_params=_params(est, ("arbitrary",)),
    )(xc, dtr, hts, dt_bias, a_log, d_skip, dy)


def conv_bwd(name, x, w, b, dy):
    L, CC = x.shape
    T = min(256, L)
    nt = L // T
    cb = _tile(CC, 512)
    hs, base = CONV_HALO, CONV_HALO - (CONV_K - 1)

    def body(x_ref, halo_ref, w_ref, b_ref, dy_ref, dx_ref, dw_ref, db_ref, carry):
        j = pl.program_id(1)
        first = j == nt - 1

        @pl.when(j == 0)
        def _():
            carry[...] = jnp.zeros_like(carry)

        wv = w_ref[...]
        ext = jnp.concatenate([jnp.where(first, 0.0, halo_ref[...]), x_ref[...]], axis=0)
        taps = [ext[base + k:base + k + T] for k in range(CONV_K)]
        pre = b_ref[...]
        for k in range(CONV_K):
            pre = pre + wv[k:k + 1, :] * taps[k]
        s = jax.nn.sigmoid(pre)
        dpre = dy_ref[...] * (s * (1.0 + pre * (1.0 - s)))
        dw = jnp.concatenate([jnp.sum(dpre * taps[k], axis=0, keepdims=True) for k in range(CONV_K)], axis=0)
        db = jnp.sum(dpre, axis=0, keepdims=True)
        zpad = jnp.zeros((hs, dpre.shape[1]), F32)
        dpad = jnp.concatenate([zpad, dpre, zpad], axis=0)
        dext = wv[0:1, :] * dpad[hs - base:hs - base + hs + T]
        for k in range(1, CONV_K):
            dext = dext + wv[k:k + 1, :] * dpad[hs - base - k:hs - base - k + hs + T]
        dx_ref[0:T - hs, :] = dext[hs:T].astype(dx_ref.dtype)
        dx_ref[T - hs:T, :] = (dext[T:T + hs] + carry[...]).astype(dx_ref.dtype)
        carry[...] = dext[0:hs]

        @pl.when(j == 0)
        def _():
            dw_ref[...] = dw
            db_ref[...] = db

        @pl.when(j > 0)
        def _():
            dw_ref[...] += dw
            db_ref[...] += db

    r = T // hs
    tile = pl.BlockSpec((T, cb), lambda c, j: (nt - 1 - j, c))
    est = 40 * _nbytes((T + 2 * hs, cb), F32)
    return pl.pallas_call(
        body, name=name, grid=(CC // cb, nt),
        in_specs=[tile, pl.BlockSpec((hs, cb), lambda c, j: (jnp.maximum((nt - 1 - j) * r - 1, 0), c)),
                  pl.BlockSpec((CONV_K, cb), lambda c, j: (0, c)), pl.BlockSpec((1, cb), lambda c, j: (0, c)), tile],
        out_specs=[tile, pl.BlockSpec((CONV_K, cb), lambda c, j: (0, c)), pl.BlockSpec((1, cb), lambda c, j: (0, c))],
        out_shape=[jax.ShapeDtypeStruct((L, CC), BF16), jax.ShapeDtypeStruct((CONV_K, CC), F32),
                   jax.ShapeDtypeStruct((1, CC), F32)],
        scratch_shapes=[pltpu.VMEM((hs, cb), F32)],
        compiler_params=_params(est, ("arbitrary", "arbitrary")),
    )(x, x, w, b, dy)


def loss_head(x, w, target, tm):
    L, D = x.shape
    tm = min(tm, L)

    def tile_loss(xv, wv, tv):
        (y,) = f_rms(xv, wv)
        return 0.5 * jnp.sum(jnp.mean(jnp.square(y - tv), axis=-1))

    def body(x_ref, w_ref, t_ref, dx_ref, dw_ref, loss_ref):
        val, (dx, dw) = jax.value_and_grad(tile_loss, argnums=(0, 1))(x_ref[...], w_ref[...], t_ref[...])
        dx_ref[...] = dx
        first = pl.program_id(0) == 0
        lv = jnp.full((8, 128), val, F32)

        @pl.when(first)
        def _():
            dw_ref[...] = dw
            loss_ref[...] = lv

        @pl.when(jnp.logical_not(first))
        def _():
            dw_ref[...] += dw
            loss_ref[...] += lv

    row = pl.BlockSpec((tm, D), lambda i: (i, 0))
    est = 16 * _nbytes((tm, D), F32)
    return pl.pallas_call(
        body, name="loss_head", grid=(L // tm,),
        in_specs=[row, pl.BlockSpec((1, D), lambda i: (0, 0)), row],
        out_specs=[row, pl.BlockSpec((1, D), lambda i: (0, 0)), pl.BlockSpec((8, 128), lambda i: (0, 0))],
        out_shape=[jax.ShapeDtypeStruct((L, D), F32), jax.ShapeDtypeStruct((1, D), F32), jax.ShapeDtypeStruct((8, 128), F32)],
        compiler_params=_params(est, ("arbitrary",)),
    )(x, w, target)


def _adamw(w, g, m, v):
    m = ADAM_B1 * m + (1.0 - ADAM_B1) * g
    v = ADAM_B2 * v + (1.0 - ADAM_B2) * jnp.square(g)
    m_hat = m / (1.0 - ADAM_B1 ** ADAM_STEP)
    v_hat = v / (1.0 - ADAM_B2 ** ADAM_STEP)
    delta = -ADAM_LR * (m_hat / (jnp.sqrt(v_hat) + ADAM_EPS) + ADAM_WD * w)
    return delta, m, v


def adamw_sharded(name, parts, w, m, v, li, carried):
    depth, R, C = w.shape
    tr = _tile(R, 128, 16)
    n_in = 4 + (4 if carried else 0)

    def body(*refs):
        p_ref, w_ref, m_ref, v_ref = refs[:4]
        g_ref, d_ref, nm_ref, nv_ref = refs[n_in:n_in + 4]
        g = p_ref[0].astype(F32)
        for s in range(1, N_DEV):
            g = g + p_ref[s].astype(F32)
        d, nm, nv = _adamw(w_ref[...], g, m_ref[...], v_ref[...])
        g_ref[...] = g
        d_ref[...] = d
        nm_ref[...] = nm
        nv_ref[...] = nv

    row = pl.BlockSpec((None, tr, C), lambda i: (li, i, 0))
    est = 2 * _nbytes((N_DEV, tr, C), parts.dtype) + 20 * _nbytes((tr, C), F32)
    return pl.pallas_call(
        body, name=name, grid=(R // tr,),
        in_specs=[pl.BlockSpec((N_DEV, tr, C), lambda i: (0, i, 0)), row, row, row]
        + ([pl.BlockSpec(memory_space=pl.ANY)] * 4 if carried else []),
        out_specs=[row] * 4,
        out_shape=[jax.ShapeDtypeStruct((depth, R, C), F32)] * 4,
        input_output_aliases={4 + k: k for k in range(4)} if carried else {},
        compiler_params=_params(est, ("parallel",)),
    )(parts, w, m, v, *(carried or ()))


def _my_place():
    return lax.axis_index("x"), lax.axis_index("y"), lax.axis_index("c")


def all_gather_hbm(name, shard):
    R, W = shard.shape

    def body(x_ref, out_ref, send_sems, recv_sems, local_sem):
        x, y, c = _my_place()
        me, sibling = (x, y, c), (x, y, 1 - c)
        chips = [(1 - x, y), (x, 1 - y), (1 - x, 1 - y)]

        def slot(px, py, pc):
            return out_ref.at[4 * px + 2 * py + pc]

        def copy(k, block, to, src=None):
            return pltpu.make_async_remote_copy(
                src_ref=slot(*block) if src is None else src, dst_ref=slot(*block),
                send_sem=send_sems.at[k], recv_sem=recv_sems.at[k], device_id=to, device_id_type=MESH_T)

        mine = pltpu.make_async_copy(x_ref, slot(*me), local_sem)
        mine.start()
        first = [copy(0, me, sibling, src=x_ref)]
        first += [copy(1 + j, me, (*chip, c), src=x_ref) for j, chip in enumerate(chips)]
        for cp in first:
            cp.start()
        passed = [copy(4 + j, (*chip, c), sibling) for j, chip in enumerate(chips)]
        for j, chip in enumerate(chips):
            copy(1 + j, (*chip, c), me).wait_recv()
            passed[j].start()
        copy(0, sibling, me).wait_recv()
        for j, chip in enumerate(chips):
            copy(4 + j, (*chip, 1 - c), me).wait_recv()
        for cp in first + passed:
            cp.wait_send()
        mine.wait()

    return pl.pallas_call(
        body, name=name,
        out_shape=jax.ShapeDtypeStruct((N_DEV, R, W), shard.dtype),
        in_specs=[pl.BlockSpec(memory_space=pl.ANY)],
        out_specs=pl.BlockSpec(memory_space=pl.ANY),
        scratch_shapes=[pltpu.SemaphoreType.DMA((7,)), pltpu.SemaphoreType.DMA((7,)), pltpu.SemaphoreType.DMA],
    )(shard)


HBM_SPEC = pl.BlockSpec(memory_space=pltpu.HBM)
SEM_SPEC = pl.BlockSpec(memory_space=pltpu.SEMAPHORE)
DATAFLOW = pltpu.SideEffectType.DATAFLOW_SIDE_EFFECTING


def _me():
    x, y, c = _my_place()
    return 4 * x + 2 * y + c


def copies_start(name, src, gather, after=()):
    blk = src.shape if gather else src.shape[1:]
    mine = src[None] if gather else lax.dynamic_slice_in_dim(src, _me(), 1, axis=0)
    land = lax.dynamic_update_slice(lax.empty((N_DEV,) + tuple(blk), src.dtype), mine, (_me(), 0, 0))
    n_after = len(after)

    def body(*refs):
        src_ref, land_ref = refs[0], refs[1]
        send_sem, recv_sem = refs[2 + n_after], refs[3 + n_after]
        token = refs[-1]
        x, y, c = _my_place()
        me = 4 * x + 2 * y + c
        for k in range(1, N_DEV):
            px, py, pc = (x + (k >> 2)) % 2, (y + ((k >> 1) & 1)) % 2, (c + (k & 1)) % 2
            pltpu.make_async_remote_copy(
                src_ref=src_ref if gather else src_ref.at[4 * px + 2 * py + pc], dst_ref=land_ref.at[me],
                send_sem=send_sem, recv_sem=recv_sem, device_id=(px, py, pc), device_id_type=MESH_T).start()
        token[...] = jnp.zeros_like(token)

    return pl.pallas_call(
        body, name=name,
        out_shape=(pltpu.SemaphoreType.DMA(()), pltpu.SemaphoreType.DMA(()), pltpu.HBM(src.shape, src.dtype),
                   pltpu.HBM(land.shape, land.dtype), jax.ShapeDtypeStruct((8, 128), F32)),
        in_specs=(HBM_SPEC, HBM_SPEC) + (pl.BlockSpec(memory_space=pl.ANY),) * n_after,
        out_specs=(SEM_SPEC, SEM_SPEC, HBM_SPEC, HBM_SPEC, pl.BlockSpec(memory_space=pltpu.VMEM)),
        input_output_aliases={0: 2, 1: 3},
        compiler_params=pltpu.CompilerParams(has_side_effects=DATAFLOW),
    )(pltpu.with_memory_space_constraint(src, pltpu.HBM), pltpu.with_memory_space_constraint(land, pltpu.HBM), *after)


def copies_wait(name, started, after):
    send_sem, recv_sem, src_thru, land_thru, _ = started

    def body(src_ref, land_ref, send_sem, recv_sem, after_ref, src_dead, got_ref):
        seven = land_ref.at[pl.ds(0, N_DEV - 1)]
        all_seven = pltpu.make_async_remote_copy(src_ref=seven, dst_ref=seven, send_sem=send_sem, recv_sem=recv_sem,
                                                 device_id=_my_place(), device_id_type=MESH_T)
        all_seven.wait_send()
        all_seven.wait_recv()

    return pl.pallas_call(
        body, name=name,
        out_shape=(pltpu.HBM(src_thru.shape, src_thru.dtype), pltpu.HBM(land_thru.shape, land_thru.dtype)),
        in_specs=(HBM_SPEC, HBM_SPEC, SEM_SPEC, SEM_SPEC, pl.BlockSpec(memory_space=pl.ANY)),
        out_specs=(HBM_SPEC, HBM_SPEC), input_output_aliases={0: 0, 1: 1},
        compiler_params=pltpu.CompilerParams(has_side_effects=DATAFLOW),
    )(src_thru, land_thru, send_sem, recv_sem, after)[1]


def small_allreduce_adamw(part, w, m, v):
    R, W = part.shape

    def body(x_ref, w_ref, m_ref, v_ref, g_ref, d_ref, nm_ref, nv_ref, all_ref, send_sems, recv_sems, local_sem):
        x, y, c = _my_place()
        me, sibling = (x, y, c), (x, y, 1 - c)
        chips = [(1 - x, y), (x, 1 - y), (1 - x, 1 - y)]

        def slot(px, py, pc):
            return all_ref.at[4 * px + 2 * py + pc]

        def copy(k, block, to, src=None):
            return pltpu.make_async_remote_copy(
                src_ref=slot(*block) if src is None else src, dst_ref=slot(*block),
                send_sem=send_sems.at[k], recv_sem=recv_sems.at[k], device_id=to, device_id_type=MESH_T)

        mine = pltpu.make_async_copy(x_ref, slot(*me), local_sem)
        mine.start()
        first = [copy(0, me, sibling, src=x_ref)]
        first += [copy(1 + j, me, (*chip, c), src=x_ref) for j, chip in enumerate(chips)]
        for cp in first:
            cp.start()
        passed = [copy(4 + j, (*chip, c), sibling) for j, chip in enumerate(chips)]
        for j, chip in enumerate(chips):
            copy(1 + j, (*chip, c), me).wait_recv()
            passed[j].start()
        copy(0, sibling, me).wait_recv()
        for j, chip in enumerate(chips):
            copy(4 + j, (*chip, 1 - c), me).wait_recv()
        for cp in first + passed:
            cp.wait_send()
        mine.wait()
        g = all_ref[0]
        for s in range(1, N_DEV):
            g = g + all_ref[s]
        d, nm, nv = _adamw(w_ref[...], g, m_ref[...], v_ref[...])
        g_ref[...] = g
        d_ref[...] = d
        nm_ref[...] = nm
        nv_ref[...] = nv

    vm = pl.BlockSpec(memory_space=pltpu.VMEM)
    return pl.pallas_call(
        body, name="small_allreduce_adamw",
        out_shape=[jax.ShapeDtypeStruct((R, W), F32)] * 4,
        in_specs=[vm] * 4, out_specs=[vm] * 4,
        scratch_shapes=[pltpu.VMEM((N_DEV, R, W), F32), pltpu.SemaphoreType.DMA((7,)), pltpu.SemaphoreType.DMA((7,)),
                        pltpu.SemaphoreType.DMA],
    )(part, w, m, v)


SHARDED = ("w_in", "conv_w", "pool_w", "w_attn_br", "w_pool_br", "w_ssm_br", "w_out", "w_gate_up", "w_down")
REPLICATED = ("ln1_w", "attn_sink", "conv_b", "dt_bias", "a_log", "d_skip", "ssm_norm_w", "pool_scale", "ln2_w")
CONV_ROWS = 8


class Pending:
    def __init__(self, name, started, layout):
        self.name, self.started, self.layout = name, started, layout


def need(fw, n, after):
    if n not in fw:
        n_src = "w_in"
        fw.update(fw[n_src].layout(copies_wait(fw[n_src].name, fw[n_src].started, after)))
        del fw[n_src]
    elif isinstance(fw[n], Pending):
        fw[n] = fw[n].layout(copies_wait(fw[n].name, fw[n].started, after))
    return fw[n]


def _w_in_layout(dm, g_in):
    win = jnp.concatenate([g_in[d] for d in range(N_DEV)], axis=1)
    pts, acc = [], 0
    for wd in dm.in_widths:
        pts.append((acc, acc + wd))
        acc += wd
    cols = lambda k: win[:, pts[k][0]:pts[k][1]]
    fw = {"w_qkv": win[:, :pts[2][1]], "w_u": cols(3), "w_z": cols(4), "w_xbc": cols(5), "w_gl": cols(7)}
    fw["w_dt"] = jnp.pad(cols(6), ((0, 0), (0, dm.DTP - dm.H)))
    fw["w_in_int"] = jnp.concatenate([win[:, :pts[5][1]], fw["w_dt"], fw["w_gl"]], axis=1)
    return fw


def _gather_weights(dm, W, li, w_in_now, tokens):
    tag = f"ag_l{li}_"
    fw = {}

    def start(n, shard, layout):
        st = copies_start(tag + n + "_start", shard, True, after=tokens[-1:])
        tokens.append(st[4])
        fw[n] = Pending(tag + n + "_wait", st, layout)

    if w_in_now:
        g_in = all_gather_hbm(tag + "w_in", W["w_in"][li].astype(BF16))
        tokens.append(g_in)
        fw.update(_w_in_layout(dm, g_in))
    else:
        start("w_in", W["w_in"][li].astype(BF16), functools.partial(_w_in_layout, dm))
    start("pool_w", W["pool_w"][li].astype(BF16).reshape(dm.PG * dm.PC // N_DEV, dm.PC),
          lambda g: g.reshape(N_DEV, dm.PG, dm.PC // N_DEV, dm.PC).transpose(1, 0, 2, 3)
          .reshape(dm.PG, dm.PC, dm.PC).astype(F32))
    start("conv_w", jnp.pad(W["conv_w"][li], ((0, CONV_ROWS - CONV_K), (0, 0))),
          lambda g: g[:, :CONV_K].transpose(1, 0, 2).reshape(CONV_K, dm.CC))
    for n in ("w_attn_br", "w_pool_br", "w_ssm_br", "w_out", "w_gate_up", "w_down"):
        if n in ("w_ssm_br", "w_out", "w_down"):
            layout = lambda g: g.reshape(g.shape[0] * g.shape[1], g.shape[2])
        else:
            layout = lambda g: g
        start(n, W[n][li].astype(BF16), layout)
    return fw


def _grad_blocks(dm, n, g):
    if n == "w_in":
        o = [0]
        for wd in dm.seg:
            o.append(o[-1] + wd)
        ref_cols = jnp.concatenate([g[:, :o[4]], g[:, o[4]:o[4] + dm.H], g[:, o[5]:]], axis=1)
        per = dm.IN_COLS // N_DEV
        return jnp.stack([ref_cols[:, d * per:(d + 1) * per] for d in range(N_DEV)])
    if n in ("w_attn_br", "w_pool_br", "w_gate_up"):
        return g
    if n in ("w_ssm_br", "w_out", "w_down"):
        return g.reshape(N_DEV, g.shape[0] // N_DEV, g.shape[1])
    if n == "pool_w":
        g = g.reshape(dm.PG, N_DEV, dm.PC // N_DEV, dm.PC).transpose(1, 0, 2, 3)
        return g.reshape(N_DEV, dm.PG * dm.PC // N_DEV, dm.PC).astype(BF16)
    assert n == "conv_w"
    g = g.reshape(CONV_K, N_DEV, dm.CC // N_DEV).transpose(1, 0, 2)
    return jnp.pad(g, ((0, 0), (0, CONV_ROWS - CONV_K), (0, 0)))


def _as_rows(n, a):
    if n == "pool_w":
        return a.reshape(a.shape[0], a.shape[1] * a.shape[2], a.shape[3])
    if n == "conv_w":
        return jnp.pad(a, ((0, 0), (0, CONV_ROWS - CONV_K), (0, 0)))
    return a


def _from_rows(n, a, like):
    if n == "pool_w":
        return a.reshape(like.shape)
    if n == "conv_w":
        return a[:, :CONV_K]
    return a


def _size(shape):
    n = 1
    for s in shape:
        n *= s
    return n


def _layer_forward(dm, x, wts, rep, li):
    tag = f"l{li}_"
    sv = {"x": x}
    (h,) = row_call(tag + "rms1", f_rms, [x], [rep["ln1_w"]], [(dm.D, BF16)], 256)
    sv["h"] = h
    qkv = matmul(tag + "p_qkv", h, need(wts, "w_qkv", x), "nn", BF16)
    u = matmul(tag + "p_u", h, wts["w_u"], "nn", F32)
    z = matmul(tag + "p_z", h, wts["w_z"], "nn", F32)
    xbc = matmul(tag + "p_xbc", h, wts["w_xbc"], "nn", F32)
    dtr = matmul(tag + "p_dt", h, wts["w_dt"], "nn", F32)
    gl = matmul(tag + "p_gl", h, wts["w_gl"], "nn", BF16)
    sv.update(qkv=qkv, u=u, z=z, xbc=xbc, dtr=dtr, gl=gl)
    kvi = dm.AW // dm.KVW
    (att,) = halo_call(tag + "attn", f_attn,
                       [(qkv, dm.AW, 0, 0), (qkv, dm.KVW, kvi, WINDOW), (qkv, dm.KVW, kvi + 1, WINDOW)],
                       [rep["attn_sink"]], [(dm.AW, BF16)], WINDOW)
    (pool,) = halo_call(tag + "pool", f_pool, [(u, dm.PW, 0, POOL_HALO)],
                        [need(wts, "pool_w", att), rep["pool_scale"]], [(dm.PW, BF16)], 256)
    (xc,) = halo_call(tag + "conv", f_conv, [(xbc, dm.CC, 0, CONV_HALO)],
                      [need(wts, "conv_w", pool), rep["conv_b"]], [(dm.CC, F32)], 256)
    y, hts = ssd_fwd(tag + "ssd", xc, dtr, rep["dt_bias"], rep["a_log"], rep["d_skip"])
    (ssm,) = row_call(tag + "gnorm", f_gnorm, [y, z], [rep["ssm_norm_w"]], [(dm.DI, BF16)], 256)
    sv.update(att=att, pool=pool, xc=xc, y=y, hts=hts, ssm=ssm)
    ba = matmul(tag + "br_a", att, need(wts, "w_attn_br", ssm), "nn", BF16, b_blocked=True)
    bp = matmul(tag + "br_p", pool, need(wts, "w_pool_br", ba), "nn", BF16, b_blocked=True)
    bs = matmul(tag + "br_s", ssm, need(wts, "w_ssm_br", bp), "nn", BF16)
    (merged,) = row_call(tag + "merge", f_merge, [gl, ba, bp, bs], [], [(dm.D, BF16)], 256)
    x1 = matmul(tag + "out", merged, need(wts, "w_out", merged), "nn", F32, add=x)
    (h2,) = row_call(tag + "rms2", f_rms, [x1], [rep["ln2_w"]], [(dm.D, BF16)], 256)
    gu = matmul(tag + "gu", h2, need(wts, "w_gate_up", h2), "nn", BF16, b_blocked=True)
    (act,) = row_call(tag + "swiglu", f_swiglu, [gu], [], [(dm.DFF, BF16)], 256)
    x2 = matmul(tag + "down", act, need(wts, "w_down", act), "nn", F32, add=x1)
    sv.update(ba=ba, bp=bp, bs=bs, merged=merged, x1=x1, h2=h2, gu=gu, act=act)
    return x2, sv


class GradSink:
    def __init__(self, dm, li):
        self.dm, self.li, self.started, self.tokens = dm, li, {}, []

    def __setitem__(self, n, g):
        self.started[n] = copies_start(f"rs_l{self.li}_{n}_start", _grad_blocks(self.dm, n, g), False)
        self.tokens.append(self.started[n][4])

    def take(self):
        t, self.tokens = tuple(self.tokens), []
        return t


def _layer_backward(dm, dx2, sv, wts, rep, li, prev_tokens):
    tag = f"l{li}_b_"
    gw, gr = GradSink(dm, li), {}
    dact = matmul(tag + "d_act", dx2, wts["w_down"], "nt", BF16, after=prev_tokens)
    gw["w_down"] = matmul(tag + "g_down", sv["act"], dx2, "tn", BF16)
    (dgu,), _ = row_vjp_call(tag + "swiglu", f_swiglu, [sv["gu"]], [], [dact], [BF16], [], 256)
    dh2 = matmul(tag + "d_h2", dgu, wts["w_gate_up"], "nt", F32, b_blocked=True, after=gw.take())
    gw["w_gate_up"] = matmul(tag + "g_gu", sv["h2"], dgu, "tn", BF16, out_blocks=N_DEV)
    (dx1,), (gr["ln2_w"],) = row_vjp_call(tag + "rms2", f_rms, [sv["x1"]], [rep["ln2_w"]], [dh2], [F32], [True], 256,
                                          adds={0: dx2})
    dmerged = matmul(tag + "d_merged", dx1, wts["w_out"], "nt", BF16, after=gw.take())
    gw["w_out"] = matmul(tag + "g_out", sv["merged"], dx1, "tn", BF16)
    (dgl, dba, dbp, dbs), _ = row_vjp_call(tag + "merge", f_merge, [sv["gl"], sv["ba"], sv["bp"], sv["bs"]], [],
                                           [dmerged], [BF16, BF16, BF16, BF16], [], 256)
    datt = matmul(tag + "d_att", dba, wts["w_attn_br"], "nt", F32, b_blocked=True, after=gw.take())
    gw["w_attn_br"] = matmul(tag + "g_br_a", sv["att"], dba, "tn", BF16, out_blocks=N_DEV)
    dpool = matmul(tag + "d_pool", dbp, wts["w_pool_br"], "nt", F32, b_blocked=True, after=gw.take())
    gw["w_pool_br"] = matmul(tag + "g_br_p", sv["pool"], dbp, "tn", BF16, out_blocks=N_DEV)
    dssm = matmul(tag + "d_ssm", dbs, wts["w_ssm_br"], "nt", F32, after=gw.take())
    gw["w_ssm_br"] = matmul(tag + "g_br_s", sv["ssm"], dbs, "tn", BF16)
    (dy, dz), (gr["ssm_norm_w"],) = row_vjp_call(tag + "gnorm", f_gnorm, [sv["y"], sv["z"]], [rep["ssm_norm_w"]],
                                                 [dssm], [F32, BF16], [True], 256)
    dxc, ddtr, gr["dt_bias"], gr["a_log"], gr["d_skip"] = ssd_bwd(
        tag + "ssd", sv["xc"], sv["dtr"], sv["hts"], rep["dt_bias"], rep["a_log"], rep["d_skip"], dy)
    dxbc, gw["conv_w"], gr["conv_b"] = conv_bwd(tag + "conv", sv["xbc"], wts["conv_w"], rep["conv_b"], dxc)
    (du,), (gw["pool_w"], gr["pool_scale"]) = halo_vjp_call(
        tag + "pool", f_pool, [(sv["u"], dm.PW, 0, POOL_HALO)], [wts["pool_w"], rep["pool_scale"]], [dpool],
        [BF16], [True, True], 256)
    kvi = dm.AW // dm.KVW
    qkv = sv["qkv"]
    (dq, dk, dv), (gr["attn_sink"],) = halo_vjp_call(
        tag + "attn", f_attn, [(qkv, dm.AW, 0, 0), (qkv, dm.KVW, kvi, WINDOW), (qkv, dm.KVW, kvi + 1, WINDOW)],
        [rep["attn_sink"]], [datt], [BF16, BF16, BF16], [True], WINDOW)
    dproj = jnp.concatenate([dq, dk, dv, du, dz, dxbc, ddtr, dgl], axis=1)
    gw["w_in"] = matmul(tag + "g_in", sv["h"], dproj, "tn", BF16, after=gw.take())
    dh = matmul(tag + "d_h", dproj, wts["w_in_int"], "nt", F32, after=gw.take())
    (dx,), (gr["ln1_w"],) = row_vjp_call(tag + "rms1", f_rms, [sv["x"]], [rep["ln1_w"]], [dh], [F32], [True], 256,
                                         adds={0: dx1})
    return dx, gw.started, gr, gw.take()


def kernel(x, ln1_w, w_in, attn_sink, conv_w, conv_b, dt_bias, a_log, d_skip, ssm_norm_w, pool_w, pool_scale, w_attn_br, w_pool_br, w_ssm_br, w_out, ln2_w, w_gate_up, w_down, final_w, loss_target, m_ln1_w, m_w_in, m_attn_sink, m_conv_w, m_conv_b, m_dt_bias, m_a_log, m_d_skip, m_ssm_norm_w, m_pool_w, m_pool_scale, m_w_attn_br, m_w_pool_br, m_w_ssm_br, m_w_out, m_ln2_w, m_w_gate_up, m_w_down, m_final_w, v_ln1_w, v_w_in, v_attn_sink, v_conv_w, v_conv_b, v_dt_bias, v_a_log, v_d_skip, v_ssm_norm_w, v_pool_w, v_pool_scale, v_w_attn_br, v_w_pool_br, v_w_ssm_br, v_w_out, v_ln2_w, v_w_gate_up, v_w_down, v_final_w):
    dm = Dims()
    W = dict(ln1_w=ln1_w, w_in=w_in, attn_sink=attn_sink, conv_w=conv_w, conv_b=conv_b, dt_bias=dt_bias, a_log=a_log,
             d_skip=d_skip, ssm_norm_w=ssm_norm_w, pool_w=pool_w, pool_scale=pool_scale, w_attn_br=w_attn_br,
             w_pool_br=w_pool_br, w_ssm_br=w_ssm_br, w_out=w_out, ln2_w=ln2_w, w_gate_up=w_gate_up, w_down=w_down,
             final_w=final_w)
    M = dict(ln1_w=m_ln1_w, w_in=m_w_in, attn_sink=m_attn_sink, conv_w=m_conv_w, conv_b=m_conv_b, dt_bias=m_dt_bias,
             a_log=m_a_log, d_skip=m_d_skip, ssm_norm_w=m_ssm_norm_w, pool_w=m_pool_w, pool_scale=m_pool_scale,
             w_attn_br=m_w_attn_br, w_pool_br=m_w_pool_br, w_ssm_br=m_w_ssm_br, w_out=m_w_out, ln2_w=m_ln2_w,
             w_gate_up=m_w_gate_up, w_down=m_w_down, final_w=m_final_w)
    V = dict(ln1_w=v_ln1_w, w_in=v_w_in, attn_sink=v_attn_sink, conv_w=v_conv_w, conv_b=v_conv_b, dt_bias=v_dt_bias,
             a_log=v_a_log, d_skip=v_d_skip, ssm_norm_w=v_ssm_norm_w, pool_w=v_pool_w, pool_scale=v_pool_scale,
             w_attn_br=v_w_attn_br, w_pool_br=v_w_pool_br, w_ssm_br=v_w_ssm_br, w_out=v_w_out, ln2_w=v_ln2_w,
             w_gate_up=v_w_gate_up, w_down=v_w_down, final_w=v_final_w)
    xl = x[0]
    target = loss_target[0]

    tokens = []
    full = [_gather_weights(dm, W, li, li == 0, tokens) for li in range(DEPTH)]
    rep = [{n: W[n][li].reshape(1, -1) for n in REPLICATED} for li in range(DEPTH)]
    rep[0]["ln1_w"] = rep[0]["ln1_w"] + tokens[-1][0, 0]

    saved = []
    xa = xl
    for li in range(DEPTH):
        xa, sv = _layer_forward(dm, xa, full[li], rep[li], li)
        saved.append(sv)
    dxa, g_final, loss_blk = loss_head(xa, W["final_w"].reshape(1, -1), target, 256)

    gws, grs, left = [None] * DEPTH, [None] * DEPTH, ()
    for li in reversed(range(DEPTH)):
        dxa, gws[li], grs[li], left = _layer_backward(dm, dxa, saved[li], full[li], rep[li], li, left)
    grad_x = dxa[None]

    sharded_out = {n: None for n in SHARDED}
    for li in reversed(range(DEPTH)):
        for n in SHARDED:
            parts = copies_wait(f"rs_l{li}_{n}_wait", gws[li][n], dxa)
            sharded_out[n] = adamw_sharded(f"adamw_l{li}_{n}", parts, _as_rows(n, W[n]), _as_rows(n, M[n]),
                                           _as_rows(n, V[n]), li, sharded_out[n])
    g_sh, d_sh, m_sh, v_sh = [{n: _from_rows(n, sharded_out[n][k], W[n]) for n in SHARDED} for k in range(4)]

    small_names = [(n, li) for li in range(DEPTH) for n in REPLICATED] + [("final_w", None)]
    small_shape = lambda n, li: W[n].shape if li is None else W[n].shape[1:]
    small_rows = [-(-_size(small_shape(n, li)) // (8 * 128)) * 8 for n, li in small_names]
    loss_row = sum(small_rows)

    def small_pack(get, last=None):
        rows = []
        for (n, li), r in zip(small_names, small_rows):
            a = get(n, li).reshape(-1).astype(F32)
            rows.append(jnp.pad(a, (0, r * 128 - a.size)).reshape(r, 128))
        rows.append(jnp.zeros((8, 128), F32) if last is None else last)
        return jnp.concatenate(rows, axis=0)

    part = small_pack(lambda n, li: g_final if li is None else grs[li][n], loss_blk)
    wsm = small_pack(lambda n, li: W[n] if li is None else W[n][li])
    msm = small_pack(lambda n, li: M[n] if li is None else M[n][li])
    vsm = small_pack(lambda n, li: V[n] if li is None else V[n][li])
    sm = small_allreduce_adamw(part, wsm, msm, vsm)
    loss = sm[0][loss_row, 0]

    def small_unpack(buf):
        out, r0 = {}, 0
        for (n, li), r in zip(small_names, small_rows):
            shp = small_shape(n, li)
            out[(n, li)] = buf[r0:r0 + r].reshape(-1)[:_size(shp)].reshape(shp)
            r0 += r
        return out

    sm_g, sm_d, sm_m, sm_v = [small_unpack(b) for b in sm]

    def assemble(sharded_list, small):
        outs = []
        for n in ("ln1_w", "w_in", "attn_sink", "conv_w", "conv_b", "dt_bias", "a_log", "d_skip", "ssm_norm_w", "pool_w",
                  "pool_scale", "w_attn_br", "w_pool_br", "w_ssm_br", "w_out", "ln2_w", "w_gate_up", "w_down"):
            if n in SHARDED:
                outs.append(sharded_list[n])
            else:
                outs.append(jnp.stack([small[(n, li)] for li in range(DEPTH)]))
        outs.append(small[("final_w", None)])
        return outs

    return (loss, grad_x, *assemble(g_sh, sm_g), *assemble(d_sh, sm_d), *assemble(m_sh, sm_m), *assemble(v_sh, sm_v))
```

```python
import functools

import jax
import jax.numpy as jnp
from jax import lax
from jax.experimental import pallas as pl
from jax.experimental.pallas import tpu as pltpu

D_MODEL = 2048
DEPTH = 2
ATT_HEAD_DIM = 64
ATT_Q_HEADS = 16
ATT_KV_HEADS = 4
WINDOW = 128
POOL_WINDOWS = (2, 4, 8, 16)
POOL_WIDTH = D_MODEL // 2
D_INNER = D_MODEL
SSM_HEAD_DIM = 64
SSM_GROUPS = 4
D_STATE = 128
CONV_K = 4
CHUNK = 128
N_BRANCH = 3
D_FF = 5632
EPS = 1e-6

ADAM_LR = 0.001
ADAM_B1 = 0.9
ADAM_B2 = 0.999
ADAM_EPS = 1e-08
ADAM_WD = 0.01
ADAM_STEP = 10

N_DEV = 8
F32 = jnp.float32
BF16 = jnp.bfloat16
MXU_DTYPE = jnp.bfloat16
NEG = -1e30
VMEM_CAP = 60 * 2**20
CONV_HALO = 8
POOL_HALO = 16
MESH_T = pl.DeviceIdType.MESH


class Dims:
    def __init__(self):
        self.D = D_MODEL
        self.AW = ATT_Q_HEADS * ATT_HEAD_DIM
        self.KVW = ATT_KV_HEADS * ATT_HEAD_DIM
        self.GQ = ATT_Q_HEADS // ATT_KV_HEADS
        self.PW = POOL_WIDTH
        self.PG = len(POOL_WINDOWS)
        self.PC = POOL_WIDTH // len(POOL_WINDOWS)
        self.DI = D_INNER
        self.H = D_INNER // SSM_HEAD_DIM
        self.P = SSM_HEAD_DIM
        self.G = SSM_GROUPS
        self.HPG = self.H // SSM_GROUPS
        self.GW = D_INNER // SSM_GROUPS
        self.N = D_STATE
        self.GN = SSM_GROUPS * D_STATE
        self.CC = D_INNER + 2 * SSM_GROUPS * D_STATE
        self.DTP = -(-self.H // 256) * 256
        self.DFF = D_FF
        self.in_widths = (self.AW, self.KVW, self.KVW, self.PW, self.DI, self.CC, self.H, N_BRANCH * self.D)
        self.IN_COLS = sum(self.in_widths)
        self.QKV = self.AW + 2 * self.KVW
        self.seg = (self.QKV, self.PW, self.DI, self.CC, self.DTP, N_BRANCH * self.D)
        self.IN_INT = sum(self.seg)


def _tile(n, cap, mult=128):
    if n <= cap:
        return n
    best = 0
    for t in range(mult, cap + 1, mult):
        if n % t == 0:
            best = t
    assert best, (n, cap, mult)
    return best


def _nbytes(shape, dtype):
    n = 1
    for s in shape:
        n *= s
    return n * jnp.dtype(dtype).itemsize


def _params(est_bytes, sem=None):
    limit = int(min(VMEM_CAP, max(32 * 2**20, est_bytes * 3 // 2 + 8 * 2**20)))
    kw = dict(vmem_limit_bytes=limit)
    if sem is not None:
        kw["dimension_semantics"] = sem
    return pltpu.CompilerParams(**kw)


def _raw_dot(a, b, ca, cb):
    return lax.dot_general(a.astype(MXU_DTYPE), b.astype(MXU_DTYPE), (((ca,), (cb,)), ((), ())),
                           preferred_element_type=F32)


@functools.partial(jax.custom_vjp, nondiff_argnums=(2, 3))
def bdot(a, b, ca, cb):
    return _raw_dot(a, b, ca, cb)


def _bdot_fwd(a, b, ca, cb):
    return _raw_dot(a, b, ca, cb), (a, b)


def _bdot_bwd(ca, cb, res, g):
    a, b = res
    if (ca, cb) == (1, 0):
        da, db = bdot(g, b, 1, 1), bdot(a, g, 0, 0)
    elif (ca, cb) == (1, 1):
        da, db = bdot(g, b, 1, 0), bdot(g, a, 0, 0)
    else:
        assert (ca, cb) == (0, 0)
        da, db = bdot(b, g, 1, 1), bdot(a, g, 1, 0)
    return da.astype(a.dtype), db.astype(b.dtype)


bdot.defvjp(_bdot_fwd, _bdot_bwd)


def hdot(a, b, ca=1, cb=0):
    return lax.dot_general(a, b, (((ca,), (cb,)), ((), ())), precision=lax.Precision.HIGHEST,
                           preferred_element_type=F32)


def _split_dot(x, e, ca, cb):
    hi = x.astype(BF16)
    lo = (x - hi.astype(F32)).astype(BF16)
    dn = (((ca,), (cb,)), ((), ()))
    eb = e.astype(BF16)
    return (lax.dot_general(hi, eb, dn, preferred_element_type=F32)
            + lax.dot_general(lo, eb, dn, preferred_element_type=F32))


@jax.custom_vjp
def edot(x, e):
    return _split_dot(x, e, 1, 0)


def _edot_fwd(x, e):
    return _split_dot(x, e, 1, 0), e


def _edot_bwd(e, g):
    return _split_dot(g, e, 1, 1), jnp.zeros_like(e)


edot.defvjp(_edot_fwd, _edot_bwd)


def _silu(x):
    return x * jax.nn.sigmoid(x)


def _softplus(x):
    return jnp.maximum(x, 0.0) + jnp.log1p(jnp.exp(-jnp.abs(x)))


def matmul(name, a, b, mode, out_dtype, add=None, b_blocked=False, out_blocks=0, after=()):
    if b_blocked:
        nb, rows, n = b.shape
        b_rows, b_cols = rows, nb * n
    else:
        b_rows, b_cols = b.shape
    if mode == "nn":
        (M, K), (K2, N) = a.shape, (b_rows, b_cols)
    elif mode == "nt":
        (M, K), (N, K2) = a.shape, (b_rows, b_cols)
    else:
        (K, M), (K2, N) = a.shape, (b_rows, b_cols)
    assert K == K2, (name, a.shape, b.shape, mode)
    n_blk = n if b_blocked else (N // out_blocks if out_blocks else 0)
    tm = _tile(M, 1024)
    tn = _tile(n_blk if (n_blk and mode != "nt") else N, 1536)
    tk = _tile(n_blk if (n_blk and mode == "nt") else K, 2048)
    nk = K // tk
    dims = {"nn": (1, 0), "nt": (1, 1), "tn": (0, 0)}[mode]
    a_spec = (pl.BlockSpec((tk, tm), lambda i, j, k: (k, i)) if mode == "tn"
              else pl.BlockSpec((tm, tk), lambda i, j, k: (i, k)))
    if b_blocked and mode == "nt":
        per = n // tk
        b_spec = pl.BlockSpec((None, tn, tk), lambda i, j, k: (k // per, j, k % per))
    elif b_blocked:
        per = n // tn
        b_spec = pl.BlockSpec((None, tk, tn), lambda i, j, k: (j // per, k, j % per))
    elif mode == "nt":
        b_spec = pl.BlockSpec((tn, tk), lambda i, j, k: (j, k))
    else:
        b_spec = pl.BlockSpec((tk, tn), lambda i, j, k: (k, j))
    o_spec = pl.BlockSpec((tm, tn), lambda i, j, k: (i, j))
    out_shape = jax.ShapeDtypeStruct((M, N), out_dtype)
    if out_blocks:
        assert mode == "tn" and add is None
        per_o = n_blk // tn
        o_spec = pl.BlockSpec((None, tm, tn), lambda i, j, k: (j // per_o, i, j % per_o))
        out_shape = jax.ShapeDtypeStruct((out_blocks, M, n_blk), out_dtype)
    has_add = add is not None

    def body(*refs):
        a_ref, b_ref = refs[0], refs[1]
        c_ref = refs[2] if has_add else None
        o_ref = refs[2 + has_add + len(after)]
        p = _raw_dot(a_ref[...], b_ref[...], *dims)
        if nk == 1:
            if has_add:
                p = p + c_ref[...].astype(F32)
            o_ref[...] = p.astype(o_ref.dtype)
            return
        acc = refs[-1]
        k = pl.program_id(2)

        @pl.when(k == 0)
        def _():
            acc[...] = p + c_ref[...].astype(F32) if has_add else p

        @pl.when(k > 0)
        def _():
            acc[...] += p

        @pl.when(k == nk - 1)
        def _():
            o_ref[...] = acc[...].astype(o_ref.dtype)

    est = 2 * (_nbytes((tm, tk), a.dtype) + _nbytes((tk, tn), b.dtype) + _nbytes((tm, tn), out_dtype))
    est += 3 * _nbytes((tm, tn), F32) + _nbytes((tm, tk), MXU_DTYPE) + _nbytes((tk, tn), MXU_DTYPE)
    if has_add:
        est += 2 * _nbytes((tm, tn), add.dtype)
    args = (a, b) + ((add,) if has_add else ()) + tuple(after)
    in_specs = [a_spec, b_spec] + ([o_spec] if has_add else []) + [pl.BlockSpec(memory_space=pl.ANY)] * len(after)
    return pl.pallas_call(
        body, name=name, grid=(M // tm, N // tn, nk),
        in_specs=in_specs, out_specs=o_spec, out_shape=out_shape,
        scratch_shapes=[pltpu.VMEM((tm, tn), F32)] if nk > 1 else [],
        compiler_params=_params(est, ("parallel", "parallel", "arbitrary")),
    )(*args)


def row_call(name, f, rows, pars, outs, tm):
    L = rows[0].shape[0]
    tm = min(tm, L)
    nr, npar = len(rows), len(pars)

    def body(*refs):
        vals = [r[...] for r in refs[:nr + npar]]
        res = f(*vals)
        for o_ref, v in zip(refs[nr + npar:], res):
            o_ref[...] = v.astype(o_ref.dtype)

    est = 2 * sum(_nbytes((tm, a.shape[1]), a.dtype) for a in rows) + 2 * sum(_nbytes((tm, w), d) for w, d in outs)
    est += 4 * sum(_nbytes((tm, a.shape[1]), F32) for a in rows)
    res = pl.pallas_call(
        body, name=name, grid=(L // tm,),
        in_specs=[pl.BlockSpec((tm, a.shape[1]), lambda i: (i, 0)) for a in rows]
        + [pl.BlockSpec(p.shape, lambda i: (0, 0)) for p in pars],
        out_specs=[pl.BlockSpec((tm, w), lambda i: (i, 0)) for w, _ in outs],
        out_shape=[jax.ShapeDtypeStruct((L, w), d) for w, d in outs],
        compiler_params=_params(est, ("parallel",)),
    )(*rows, *pars)
    return res


def row_vjp_call(name, f, rows, pars, cots, row_grad_dtypes, par_grads, tm, adds=None):
    L = rows[0].shape[0]
    tm = min(tm, L)
    nr, npar, nc = len(rows), len(pars), len(cots)
    adds = adds or {}
    add_idx = sorted(adds)
    rg_idx = [i for i, d in enumerate(row_grad_dtypes) if d is not None]
    pg_idx = [i for i, w in enumerate(par_grads) if w]
    diff_idx = rg_idx + [nr + i for i in pg_idx]

    def body(*refs):
        vals = [r[...] for r in refs[:nr + npar]]
        cvals = [r[...] for r in refs[nr + npar:nr + npar + nc]]
        avals = [r[...] for r in refs[nr + npar + nc:nr + npar + nc + len(add_idx)]]
        out_refs = refs[nr + npar + nc + len(add_idx):]

        def g(*dv):
            full = list(vals)
            for i, v in zip(diff_idx, dv):
                full[i] = v
            return tuple(f(*full))

        res, vjp = jax.vjp(g, *[vals[i] for i in diff_idx])
        grads = vjp(tuple(c.astype(r.dtype) for c, r in zip(cvals, res)))
        for n, i in enumerate(rg_idx):
            gval = grads[n].astype(F32)
            if i in adds:
                gval = gval + avals[add_idx.index(i)].astype(F32)
            out_refs[n][...] = gval.astype(out_refs[n].dtype)
        first = pl.program_id(0) == 0
        for n, i in enumerate(pg_idx):
            o_ref = out_refs[len(rg_idx) + n]
            gval = grads[len(rg_idx) + n].astype(F32)

            @pl.when(first)
            def _(o_ref=o_ref, gval=gval):
                o_ref[...] = gval

            @pl.when(jnp.logical_not(first))
            def _(o_ref=o_ref, gval=gval):
                o_ref[...] += gval

    row_spec = lambda a: pl.BlockSpec((tm, a.shape[1]), lambda i: (i, 0))
    est = 2 * sum(_nbytes((tm, a.shape[1]), a.dtype) for a in list(rows) + list(cots))
    est += 10 * sum(_nbytes((tm, a.shape[1]), F32) for a in rows)
    res = pl.pallas_call(
        body, name=name, grid=(L // tm,),
        in_specs=[row_spec(a) for a in rows] + [pl.BlockSpec(p.shape, lambda i: (0, 0)) for p in pars]
        + [row_spec(c) for c in cots] + [row_spec(adds[i]) for i in add_idx],
        out_specs=[row_spec(rows[i]) for i in rg_idx] + [pl.BlockSpec(pars[i].shape, lambda i_: (0, 0)) for i in pg_idx],
        out_shape=[jax.ShapeDtypeStruct(rows[i].shape, row_grad_dtypes[i]) for i in rg_idx]
        + [jax.ShapeDtypeStruct(pars[i].shape, F32) for i in pg_idx],
        compiler_params=_params(est, ("arbitrary",)),
    )(*rows, *pars, *cots, *[adds[i] for i in add_idx])
    return list(res[:len(rg_idx)]), list(res[len(rg_idx):])


def _halo_specs(rows, T, nt, rev):
    specs = []
    for (_, w, ci, hs) in rows:
        if rev:
            specs.append(pl.BlockSpec((T, w), lambda j, ci=ci: (nt - 1 - j, ci)))
        else:
            specs.append(pl.BlockSpec((T, w), lambda i, ci=ci: (i, ci)))
        if hs:
            r = T // hs
            if rev:
                specs.append(pl.BlockSpec((hs, w), lambda j, ci=ci, r=r: (jnp.maximum((nt - 1 - j) * r - 1, 0), ci)))
            else:
                specs.append(pl.BlockSpec((hs, w), lambda i, ci=ci, r=r: (jnp.maximum(i * r - 1, 0), ci)))
    return specs


def _halo_args(rows):
    args = []
    for (a, _, _, hs) in rows:
        args.append(a)
        if hs:
            args.append(a)
    return args


def _halo_vals(rows, refs):
    vals, n = [], 0
    for (_, _, _, hs) in rows:
        if hs:
            vals.append((refs[n + 1][...], refs[n][...]))
            n += 2
        else:
            vals.append(refs[n][...])
            n += 1
    return vals, n


def halo_call(name, f, rows, pars, outs, T):
    L = rows[0][0].shape[0]
    T = min(T, L)
    nt = L // T
    npar = len(pars)

    def body(*refs):
        i = pl.program_id(0)
        vals, n = _halo_vals(rows, refs)
        pv = [r[...] for r in refs[n:n + npar]]
        res = f(i == 0, i * T, *vals, *pv)
        for o_ref, v in zip(refs[n + npar:], res):
            o_ref[...] = v.astype(o_ref.dtype)

    est = 2 * sum(_nbytes((T, w), a.dtype) for a, w, _, _ in rows) + 2 * sum(_nbytes((T, w), d) for w, d in outs)
    est += 8 * sum(_nbytes((T, w), F32) for _, w, _, _ in rows)
    return pl.pallas_call(
        body, name=name, grid=(nt,),
        in_specs=_halo_specs(rows, T, nt, False) + [pl.BlockSpec(p.shape, lambda i, nd=p.ndim: (0,) * nd) for p in pars],
        out_specs=[pl.BlockSpec((T, w), lambda i: (i, 0)) for w, _ in outs],
        out_shape=[jax.ShapeDtypeStruct((L, w), d) for w, d in outs],
        compiler_params=_params(est, ("parallel",)),
    )(*_halo_args(rows), *pars)


def halo_vjp_call(name, f, rows, pars, cots, row_grad_dtypes, par_grads, T):
    L = rows[0][0].shape[0]
    T = min(T, L)
    nt = L // T
    nr, npar, nc = len(rows), len(pars), len(cots)
    pg_idx = [i for i, w in enumerate(par_grads) if w]
    halo_idx = [i for i, r in enumerate(rows) if r[3]]

    def body(*refs):
        j = pl.program_id(0)
        i = nt - 1 - j
        vals, n = _halo_vals(rows, refs)
        pv = [r[...] for r in refs[n:n + npar]]
        cv = [r[...] for r in refs[n + npar:n + npar + nc]]
        out_refs = refs[n + npar + nc:n + npar + nc + nr + len(pg_idx)]
        carries = refs[n + npar + nc + nr + len(pg_idx):]
        first = i == 0

        def g(vals_, pv_):
            return tuple(f(first, i * T, *vals_, *pv_))

        res, vjp = jax.vjp(g, vals, pv)
        dvals, dpv = vjp(tuple(c.astype(r.dtype) for c, r in zip(cv, res)))

        @pl.when(j == 0)
        def _():
            for c_ref in carries:
                c_ref[...] = jnp.zeros_like(c_ref)

        for k in range(nr):
            hs = rows[k][3]
            o_ref = out_refs[k]
            if hs:
                dh, dc = dvals[k]
                c_ref = carries[halo_idx.index(k)]
                dc = dc.astype(F32)
                if hs == T:
                    o_ref[...] = (dc + c_ref[...]).astype(o_ref.dtype)
                else:
                    o_ref[0:T - hs, :] = dc[0:T - hs].astype(o_ref.dtype)
                    o_ref[T - hs:T, :] = (dc[T - hs:T] + c_ref[...]).astype(o_ref.dtype)
                c_ref[...] = dh.astype(F32)
            else:
                o_ref[...] = dvals[k].astype(o_ref.dtype)
        for m, k in enumerate(pg_idx):
            o_ref = out_refs[nr + m]
            gval = dpv[k].astype(F32)

            @pl.when(j == 0)
            def _(o_ref=o_ref, gval=gval):
                o_ref[...] = gval

            @pl.when(j > 0)
            def _(o_ref=o_ref, gval=gval):
                o_ref[...] += gval

    est = 2 * sum(_nbytes((T, w), a.dtype) for a, w, _, _ in rows) + 2 * sum(_nbytes((T, c.shape[1]), c.dtype) for c in cots)
    est += 12 * sum(_nbytes((T, w), F32) for _, w, _, _ in rows)
    res = pl.pallas_call(
        body, name=name, grid=(nt,),
        in_specs=_halo_specs(rows, T, nt, True) + [pl.BlockSpec(p.shape, lambda j, nd=p.ndim: (0,) * nd) for p in pars]
        + [pl.BlockSpec((T, c.shape[1]), lambda j: (nt - 1 - j, 0)) for c in cots],
        out_specs=[pl.BlockSpec((T, w), lambda j: (nt - 1 - j, 0)) for _, w, _, _ in rows]
        + [pl.BlockSpec(pars[k].shape, lambda j, nd=pars[k].ndim: (0,) * nd) for k in pg_idx],
        out_shape=[jax.ShapeDtypeStruct((L, w), row_grad_dtypes[k]) for k, (_, w, _, _) in enumerate(rows)]
        + [jax.ShapeDtypeStruct(pars[k].shape, F32) for k in pg_idx],
        scratch_shapes=[pltpu.VMEM((rows[k][3], rows[k][1]), F32) for k in halo_idx],
        compiler_params=_params(est, ("arbitrary",)),
    )(*_halo_args(rows), *pars, *cots)
    return list(res[:nr]), list(res[nr:])


def f_rms(x, w):
    x = x.astype(F32)
    return (x * lax.rsqrt(jnp.mean(x * x, axis=-1, keepdims=True) + EPS) * w,)


def f_swiglu(gu):
    dff = gu.shape[1] // 2
    gu = gu.astype(F32)
    return (_silu(gu[:, :dff]) * gu[:, dff:],)


def f_merge(gl, a, p, s):
    d = a.shape[1]
    g = jax.nn.sigmoid(gl.astype(F32))
    return (g[:, :d] * a.astype(F32) + g[:, d:2 * d] * p.astype(F32) + g[:, 2 * d:] * s.astype(F32),)


def f_gnorm(y, z, nw):
    dm = Dims()
    g = y * _silu(z.astype(F32))
    outs = []
    for gi in range(dm.G):
        gg = g[:, gi * dm.GW:(gi + 1) * dm.GW]
        outs.append(gg * lax.rsqrt(jnp.mean(gg * gg, axis=-1, keepdims=True) + EPS))
    return (jnp.concatenate(outs, axis=1) * nw,)


def f_conv(first, row0, xs, w, b):
    halo, cur = xs
    halo = jnp.where(first, 0.0, halo)
    ext = jnp.concatenate([halo, cur], axis=0)
    T = cur.shape[0]
    base = CONV_HALO - (CONV_K - 1)
    pre = b
    for k in range(CONV_K):
        pre = pre + w[k:k + 1, :] * ext[base + k:base + k + T]
    return (_silu(pre),)


def f_pool(first, row0, us, pw, scale):
    dm = Dims()
    halo, cur = us
    halo = jnp.where(first, 0.0, halo)
    ext = jnp.concatenate([halo, cur], axis=0)
    T = cur.shape[0]
    t = row0 + lax.broadcasted_iota(jnp.int32, (T, 1), 0)
    outs = []
    for gi, w in enumerate(POOL_WINDOWS):
        assert w & (w - 1) == 0 and w <= POOL_HALO
        s = ext[:, gi * dm.PC:(gi + 1) * dm.PC]
        sh = 1
        while sh < w:
            s = s + jnp.concatenate([jnp.zeros((sh, dm.PC), F32), s[:-sh]], axis=0)
            sh *= 2
        cnt = jnp.minimum(t + 1, w).astype(F32)
        mixed = s[POOL_HALO:] / cnt - cur[:, gi * dm.PC:(gi + 1) * dm.PC]
        outs.append(bdot(mixed, pw[gi], 1, 0))
    return (jnp.concatenate(outs, axis=1) * scale,)


def f_attn(first, row0, q, ks, vs, sink):
    dm = Dims()
    assert WINDOW == q.shape[0]
    kp, kc = ks
    vp, vc = vs
    T = q.shape[0]
    hd = ATT_HEAD_DIM
    cur1 = lax.broadcasted_iota(jnp.int32, (T, T), 1) <= lax.broadcasted_iota(jnp.int32, (T, T), 0)
    cur = jnp.concatenate([cur1] * dm.GQ, axis=0)
    valid = cur | jnp.logical_not(first)
    outs = []
    for k in range(ATT_KV_HEADS):
        ksl = slice(k * hd, (k + 1) * hd)
        heads = [k * dm.GQ + g for g in range(dm.GQ)]
        qs = jnp.concatenate([q[:, h * hd:(h + 1) * hd] for h in heads], axis=0)
        s = jnp.where(cur, bdot(qs, kc[:, ksl], 1, 1), bdot(qs, kp[:, ksl], 1, 1)) * (hd ** -0.5)
        s = jnp.where(valid, s, NEG)
        sk = jnp.concatenate([jnp.broadcast_to(sink[:, h:h + 1], (T, 1)) for h in heads], axis=0)
        m = jnp.maximum(jnp.max(s, axis=-1, keepdims=True), sk)
        p = jnp.exp(s - m)
        den = jnp.sum(p, axis=-1, keepdims=True) + jnp.exp(sk - m)
        pn = p / den
        o = bdot(jnp.where(cur, pn, 0.0), vc[:, ksl], 1, 0) + bdot(jnp.where(cur, 0.0, pn), vp[:, ksl], 1, 0)
        outs += [o[g * T:(g + 1) * T] for g in range(dm.GQ)]
    return (jnp.concatenate(outs, axis=1),)


def f_ssd(xc, dtr, ht, dt_bias, a_log, d_skip):
    dm = Dims()
    Q = xc.shape[0]
    xs = xc[:, :dm.DI]
    bm = xc[:, dm.DI:dm.DI + dm.GN]
    cm = xc[:, dm.DI + dm.GN:]
    expand = (lax.broadcasted_iota(jnp.int32, (dm.H, dm.DI), 1) // dm.P
              == lax.broadcasted_iota(jnp.int32, (dm.H, dm.DI), 0)).astype(F32)
    ri = lax.broadcasted_iota(jnp.int32, (Q, Q), 0)
    ci = lax.broadcasted_iota(jnp.int32, (Q, Q), 1)
    causal = ri >= ci
    tinc = causal.astype(F32)
    dt = _softplus(dtr[:, :dm.H] + dt_bias)
    da = dt * (-jnp.exp(a_log))
    acs = hdot(tinc, da)
    acs_t = hdot(da, tinc, 0, 1)
    eacs = jnp.exp(acs)
    dend = jnp.exp(acs[Q - 1:Q, :] - acs)
    ex = edot(jnp.concatenate([dt, eacs, dend, jnp.broadcast_to(d_skip, (8, dm.H))], axis=0), expand)
    dt_x, eacs_x, dend_x, dsk_x = ex[:Q], ex[Q:2 * Q], ex[2 * Q:3 * Q], ex[3 * Q:3 * Q + 1]
    xdt = xs * dt_x
    ys, hts = [], []
    for g in range(dm.G):
        gs = slice(g * dm.GW, (g + 1) * dm.GW)
        bg = bm[:, g * dm.N:(g + 1) * dm.N]
        cg = cm[:, g * dm.N:(g + 1) * dm.N]
        cb = bdot(cg, bg, 1, 1)
        y_off = bdot(cg, ht[:, gs], 1, 0) * eacs_x[:, gs]
        xg = xdt[:, gs]
        st = bdot(bg, xg * dend_x[:, gs], 0, 0)
        hts.append(ht[:, gs] * eacs_x[Q - 1:Q, gs] + st)
        yd = []
        for e in range(dm.HPG):
            h = g * dm.HPG + e
            seg = acs[:, h:h + 1] - acs_t[h:h + 1, :]
            lm = jnp.exp(jnp.where(causal, seg, NEG))
            yd.append(bdot(cb * lm, xg[:, e * dm.P:(e + 1) * dm.P], 1, 0))
        ys.append(jnp.concatenate(yd, axis=1) + y_off)
    y = jnp.concatenate(ys, axis=1) + dsk_x * xs
    return y, jnp.concatenate(hts, axis=1)


def ssd_fwd(name, xc, dtr, dt_bias, a_log, d_skip):
    dm = Dims()
    L = xc.shape[0]
    Q = CHUNK
    nc = L // Q

    def body(xc_ref, dtr_ref, b_ref, a_ref, s_ref, y_ref, hts_ref, ht):
        @pl.when(pl.program_id(0) == 0)
        def _():
            ht[...] = jnp.zeros_like(ht)

        h0 = ht[...]
        hts_ref[0] = h0
        y, h1 = f_ssd(xc_ref[...], dtr_ref[...], h0, b_ref[...], a_ref[...], s_ref[...])
        y_ref[...] = y
        ht[...] = h1

    par = pl.BlockSpec((1, dm.H), lambda c: (0, 0))
    est = 40 * _nbytes((Q, dm.CC), F32) + 4 * _nbytes((dm.N, dm.DI), F32)
    return pl.pallas_call(
        body, name=name, grid=(nc,),
        in_specs=[pl.BlockSpec((Q, dm.CC), lambda c: (c, 0)), pl.BlockSpec((Q, dm.DTP), lambda c: (c, 0)), par, par, par],
        out_specs=[pl.BlockSpec((Q, dm.DI), lambda c: (c, 0)), pl.BlockSpec((1, dm.N, dm.DI), lambda c: (c, 0, 0))],
        out_shape=[jax.ShapeDtypeStruct((L, dm.DI), F32), jax.ShapeDtypeStruct((nc, dm.N, dm.DI), F32)],
        scratch_shapes=[pltpu.VMEM((dm.N, dm.DI), F32)],
        compiler_params=_params(est, ("arbitrary",)),
    )(xc, dtr, dt_bias, a_log, d_skip)


def ssd_bwd(name, xc, dtr, hts, dt_bias, a_log, d_skip, dy):
    dm = Dims()
    L = xc.shape[0]
    Q = CHUNK
    nc = L // Q

    def body(xc_ref, dtr_ref, hts_ref, b_ref, a_ref, s_ref, dy_ref, dxc_ref, ddtr_ref, db_ref, da_ref, ds_ref, dht):
        j = pl.program_id(0)

        @pl.when(j == 0)
        def _():
            dht[...] = jnp.zeros_like(dht)

        _, vjp = jax.vjp(f_ssd, xc_ref[...], dtr_ref[...], hts_ref[0], b_ref[...], a_ref[...], s_ref[...])
        dxc, ddtr, dh0, db, da, ds = vjp((dy_ref[...], dht[...]))
        dxc_ref[...] = dxc.astype(dxc_ref.dtype)
        ddtr_ref[...] = ddtr.astype(ddtr_ref.dtype)
        dht[...] = dh0
        for o_ref, gval in ((db_ref, db), (da_ref, da), (ds_ref, ds)):
            @pl.when(j == 0)
            def _(o_ref=o_ref, gval=gval):
                o_ref[...] = gval

            @pl.when(j > 0)
            def _(o_ref=o_ref, gval=gval):
                o_ref[...] += gval

    par = pl.BlockSpec((1, dm.H), lambda j: (0, 0))
    rev = lambda w: pl.BlockSpec((Q, w), lambda j: (nc - 1 - j, 0))
    est = 80 * _nbytes((Q, dm.CC), F32) + 6 * _nbytes((dm.N, dm.DI), F32)
    return pl.pallas_call(
        body, name=name, grid=(nc,),
        in_specs=[rev(dm.CC), rev(dm.DTP), pl.BlockSpec((1, dm.N, dm.DI), lambda j: (nc - 1 - j, 0, 0)), par, par, par, rev(dm.DI)],
        out_specs=[rev(dm.CC), rev(dm.DTP), par, par, par],
        out_shape=[jax.ShapeDtypeStruct((L, dm.CC), BF16), jax.ShapeDtypeStruct((L, dm.DTP), BF16)]
        + [jax.ShapeDtypeStruct((1, dm.H), F32)] * 3,
        scratch_shapes=[pltpu.VMEM((dm.N, dm.DI), F32)],
        compiler_params=_params(est, ("arbitrary",)),
    )(xc, dtr, hts, dt_bias, a_log, d_skip, dy)


def conv_bwd(name, x, w, b, dy):
    L, CC = x.shape
    T = min(256, L)
    nt = L // T
    cb = _tile(CC, 512)
    hs, base = CONV_HALO, CONV_HALO - (CONV_K - 1)

    def body(x_ref, halo_ref, w_ref, b_ref, dy_ref, dx_ref, dw_ref, db_ref, carry):
        j = pl.program_id(1)
        first = j == nt - 1

        @pl.when(j == 0)
        def _():
            carry[...] = jnp.zeros_like(carry)

        wv = w_ref[...]
        ext = jnp.concatenate([jnp.where(first, 0.0, halo_ref[...]), x_ref[...]], axis=0)
        taps = [ext[base + k:base + k + T] for k in range(CONV_K)]
        pre = b_ref[...]
        for k in range(CONV_K):
            pre = pre + wv[k:k + 1, :] * taps[k]
        s = jax.nn.sigmoid(pre)
        dpre = dy_ref[...] * (s * (1.0 + pre * (1.0 - s)))
        dw = jnp.concatenate([jnp.sum(dpre * taps[k], axis=0, keepdims=True) for k in range(CONV_K)], axis=0)
        db = jnp.sum(dpre, axis=0, keepdims=True)
        zpad = jnp.zeros((hs, dpre.shape[1]), F32)
        dpad = jnp.concatenate([zpad, dpre, zpad], axis=0)
        dext = wv[0:1, :] * dpad[hs - base:hs - base + hs + T]
        for k in range(1, CONV_K):
            dext = dext + wv[k:k + 1, :] * dpad[hs - base - k:hs - base - k + hs + T]
        dx_ref[0:T - hs, :] = dext[hs:T].astype(dx_ref.dtype)
        dx_ref[T - hs:T, :] = (dext[T:T + hs] + carry[...]).astype(dx_ref.dtype)
        carry[...] = dext[0:hs]

        @pl.when(j == 0)
        def _():
            dw_ref[...] = dw
            db_ref[...] = db

        @pl.when(j > 0)
        def _():
            dw_ref[...] += dw
            db_ref[...] += db

    r = T // hs
    tile = pl.BlockSpec((T, cb), lambda c, j: (nt - 1 - j, c))
    est = 40 * _nbytes((T + 2 * hs, cb), F32)
    return pl.pallas_call(
        body, name=name, grid=(CC // cb, nt),
        in_specs=[tile, pl.BlockSpec((hs, cb), lambda c, j: (jnp.maximum((nt - 1 - j) * r - 1, 0), c)),
                  pl.BlockSpec((CONV_K, cb), lambda c, j: (0, c)), pl.BlockSpec((1, cb), lambda c, j: (0, c)), tile],
        out_specs=[tile, pl.BlockSpec((CONV_K, cb), lambda c, j: (0, c)), pl.BlockSpec((1, cb), lambda c, j: (0, c))],
        out_shape=[jax.ShapeDtypeStruct((L, CC), BF16), jax.ShapeDtypeStruct((CONV_K, CC), F32),
                   jax.ShapeDtypeStruct((1, CC), F32)],
        scratch_shapes=[pltpu.VMEM((hs, cb), F32)],
        compiler_params=_params(est, ("arbitrary", "arbitrary")),
    )(x, x, w, b, dy)


def loss_head(x, w, target, tm):
    L, D = x.shape
    tm = min(tm, L)

    def tile_loss(xv, wv, tv):
        (y,) = f_rms(xv, wv)
        return 0.5 * jnp.sum(jnp.mean(jnp.square(y - tv), axis=-1))

    def body(x_ref, w_ref, t_ref, dx_ref, dw_ref, loss_ref):
        val, (dx, dw) = jax.value_and_grad(tile_loss, argnums=(0, 1))(x_ref[...], w_ref[...], t_ref[...])
        dx_ref[...] = dx
        first = pl.program_id(0) == 0
        lv = jnp.full((8, 128), val, F32)

        @pl.when(first)
        def _():
            dw_ref[...] = dw
            loss_ref[...] = lv

        @pl.when(jnp.logical_not(first))
        def _():
            dw_ref[...] += dw
            loss_ref[...] += lv

    row = pl.BlockSpec((tm, D), lambda i: (i, 0))
    est = 16 * _nbytes((tm, D), F32)
    return pl.pallas_call(
        body, name="loss_head", grid=(L // tm,),
        in_specs=[row, pl.BlockSpec((1, D), lambda i: (0, 0)), row],
        out_specs=[row, pl.BlockSpec((1, D), lambda i: (0, 0)), pl.BlockSpec((8, 128), lambda i: (0, 0))],
        out_shape=[jax.ShapeDtypeStruct((L, D), F32), jax.ShapeDtypeStruct((1, D), F32), jax.ShapeDtypeStruct((8, 128), F32)],
        compiler_params=_params(est, ("arbitrary",)),
    )(x, w, target)


def _adamw(w, g, m, v):
    m = ADAM_B1 * m + (1.0 - ADAM_B1) * g
    v = ADAM_B2 * v + (1.0 - ADAM_B2) * jnp.square(g)
    m_hat = m / (1.0 - ADAM_B1 ** ADAM_STEP)
    v_hat = v / (1.0 - ADAM_B2 ** ADAM_STEP)
    delta = -ADAM_LR * (m_hat / (jnp.sqrt(v_hat) + ADAM_EPS) + ADAM_WD * w)
    return delta, m, v


def adamw_sharded(name, parts, w, m, v, li, carried):
    depth, R, C = w.shape
    tr = _tile(R, 128, 16)
    n_in = 4 + (4 if carried else 0)

    def body(*refs):
        p_ref, w_ref, m_ref, v_ref = refs[:4]
        g_ref, d_ref, nm_ref, nv_ref = refs[n_in:n_in + 4]
        g = p_ref[0].astype(F32)
        for s in range(1, N_DEV):
            g = g + p_ref[s].astype(F32)
        d, nm, nv = _adamw(w_ref[...], g, m_ref[...], v_ref[...])
        g_ref[...] = g
        d_ref[...] = d
        nm_ref[...] = nm
        nv_ref[...] = nv

    row = pl.BlockSpec((None, tr, C), lambda i: (li, i, 0))
    est = 2 * _nbytes((N_DEV, tr, C), parts.dtype) + 20 * _nbytes((tr, C), F32)
    return pl.pallas_call(
        body, name=name, grid=(R // tr,),
        in_specs=[pl.BlockSpec((N_DEV, tr, C), lambda i: (0, i, 0)), row, row, row]
        + ([pl.BlockSpec(memory_space=pl.ANY)] * 4 if carried else []),
        out_specs=[row] * 4,
        out_shape=[jax.ShapeDtypeStruct((depth, R, C), F32)] * 4,
        input_output_aliases={4 + k: k for k in range(4)} if carried else {},
        compiler_params=_params(est, ("parallel",)),
    )(parts, w, m, v, *(carried or ()))


def _my_place():
    return lax.axis_index("x"), lax.axis_index("y"), lax.axis_index("c")


def all_gather_hbm(name, shard):
    R, W = shard.shape

    def body(x_ref, out_ref, send_sems, recv_sems, local_sem):
        x, y, c = _my_place()
        me, sibling = (x, y, c), (x, y, 1 - c)
        chips = [(1 - x, y), (x, 1 - y), (1 - x, 1 - y)]

        def slot(px, py, pc):
            return out_ref.at[4 * px + 2 * py + pc]

        def copy(k, block, to, src=None):
            return pltpu.make_async_remote_copy(
                src_ref=slot(*block) if src is None else src, dst_ref=slot(*block),
                send_sem=send_sems.at[k], recv_sem=recv_sems.at[k], device_id=to, device_id_type=MESH_T)

        mine = pltpu.make_async_copy(x_ref, slot(*me), local_sem)
        mine.start()
        first = [copy(0, me, sibling, src=x_ref)]
        first += [copy(1 + j, me, (*chip, c), src=x_ref) for j, chip in enumerate(chips)]
        for cp in first:
            cp.start()
        passed = [copy(4 + j, (*chip, c), sibling) for j, chip in enumerate(chips)]
        for j, chip in enumerate(chips):
            copy(1 + j, (*chip, c), me).wait_recv()
            passed[j].start()
        copy(0, sibling, me).wait_recv()
        for j, chip in enumerate(chips):
            copy(4 + j, (*chip, 1 - c), me).wait_recv()
        for cp in first + passed:
            cp.wait_send()
        mine.wait()

    return pl.pallas_call(
        body, name=name,
        out_shape=jax.ShapeDtypeStruct((N_DEV, R, W), shard.dtype),
        in_specs=[pl.BlockSpec(memory_space=pl.ANY)],
        out_specs=pl.BlockSpec(memory_space=pl.ANY),
        scratch_shapes=[pltpu.SemaphoreType.DMA((7,)), pltpu.SemaphoreType.DMA((7,)), pltpu.SemaphoreType.DMA],
    )(shard)


HBM_SPEC = pl.BlockSpec(memory_space=pltpu.HBM)
SEM_SPEC = pl.BlockSpec(memory_space=pltpu.SEMAPHORE)
DATAFLOW = pltpu.SideEffectType.DATAFLOW_SIDE_EFFECTING


def _me():
    x, y, c = _my_place()
    return 4 * x + 2 * y + c


def copies_start(name, src, gather, after=()):
    blk = src.shape if gather else src.shape[1:]
    mine = src[None] if gather else lax.dynamic_slice_in_dim(src, _me(), 1, axis=0)
    land = lax.dynamic_update_slice(lax.empty((N_DEV,) + tuple(blk), src.dtype), mine, (_me(), 0, 0))
    n_after = len(after)

    def body(*refs):
        src_ref, land_ref = refs[0], refs[1]
        send_sem, recv_sem = refs[2 + n_after], refs[3 + n_after]
        token = refs[-1]
        x, y, c = _my_place()
        me = 4 * x + 2 * y + c
        for k in range(1, N_DEV):
            px, py, pc = (x + (k >> 2)) % 2, (y + ((k >> 1) & 1)) % 2, (c + (k & 1)) % 2
            pltpu.make_async_remote_copy(
                src_ref=src_ref if gather else src_ref.at[4 * px + 2 * py + pc], dst_ref=land_ref.at[me],
                send_sem=send_sem, recv_sem=recv_sem, device_id=(px, py, pc), device_id_type=MESH_T).start()
        token[...] = jnp.zeros_like(token)

    return pl.pallas_call(
        body, name=name,
        out_shape=(pltpu.SemaphoreType.DMA(()), pltpu.SemaphoreType.DMA(()), pltpu.HBM(src.shape, src.dtype),
                   pltpu.HBM(land.shape, land.dtype), jax.ShapeDtypeStruct((8, 128), F32)),
        in_specs=(HBM_SPEC, HBM_SPEC) + (pl.BlockSpec(memory_space=pl.ANY),) * n_after,
        out_specs=(SEM_SPEC, SEM_SPEC, HBM_SPEC, HBM_SPEC, pl.BlockSpec(memory_space=pltpu.VMEM)),
        input_output_aliases={0: 2, 1: 3},
        compiler_params=pltpu.CompilerParams(has_side_effects=DATAFLOW),
    )(pltpu.with_memory_space_constraint(src, pltpu.HBM), pltpu.with_memory_space_constraint(land, pltpu.HBM), *after)


def copies_wait(name, started, after):
    send_sem, recv_sem, src_thru, land_thru, _ = started

    def body(src_ref, land_ref, send_sem, recv_sem, after_ref, src_dead, got_ref):
        seven = land_ref.at[pl.ds(0, N_DEV - 1)]
        all_seven = pltpu.make_async_remote_copy(src_ref=seven, dst_ref=seven, send_sem=send_sem, recv_sem=recv_sem,
                                                 device_id=_my_place(), device_id_type=MESH_T)
        all_seven.wait_send()
        all_seven.wait_recv()

    return pl.pallas_call(
        body, name=name,
        out_shape=(pltpu.HBM(src_thru.shape, src_thru.dtype), pltpu.HBM(land_thru.shape, land_thru.dtype)),
        in_specs=(HBM_SPEC, HBM_SPEC, SEM_SPEC, SEM_SPEC, pl.BlockSpec(memory_space=pl.ANY)),
        out_specs=(HBM_SPEC, HBM_SPEC), input_output_aliases={0: 0, 1: 1},
        compiler_params=pltpu.CompilerParams(has_side_effects=DATAFLOW),
    )(src_thru, land_thru, send_sem, recv_sem, after)[1]


def small_allreduce_adamw(part, w, m, v):
    R, W = part.shape

    def body(x_ref, w_ref, m_ref, v_ref, g_ref, d_ref, nm_ref, nv_ref, all_ref, send_sems, recv_sems, local_sem):
        x, y, c = _my_place()
        me, sibling = (x, y, c), (x, y, 1 - c)
        chips = [(1 - x, y), (x, 1 - y), (1 - x, 1 - y)]

        def slot(px, py, pc):
            return all_ref.at[4 * px + 2 * py + pc]

        def copy(k, block, to, src=None):
            return pltpu.make_async_remote_copy(
                src_ref=slot(*block) if src is None else src, dst_ref=slot(*block),
                send_sem=send_sems.at[k], recv_sem=recv_sems.at[k], device_id=to, device_id_type=MESH_T)

        mine = pltpu.make_async_copy(x_ref, slot(*me), local_sem)
        mine.start()
        first = [copy(0, me, sibling, src=x_ref)]
        first += [copy(1 + j, me, (*chip, c), src=x_ref) for j, chip in enumerate(chips)]
        for cp in first:
            cp.start()
        passed = [copy(4 + j, (*chip, c), sibling) for j, chip in enumerate(chips)]
        for j, chip in enumerate(chips):
            copy(1 + j, (*chip, c), me).wait_recv()
            passed[j].start()
        copy(0, sibling, me).wait_recv()
        for j, chip in enumerate(chips):
            copy(4 + j, (*chip, 1 - c), me).wait_recv()
        for cp in first + passed:
            cp.wait_send()
        mine.wait()
        g = all_ref[0]
        for s in range(1, N_DEV):
            g = g + all_ref[s]
        d, nm, nv = _adamw(w_ref[...], g, m_ref[...], v_ref[...])
        g_ref[...] = g
        d_ref[...] = d
        nm_ref[...] = nm
        nv_ref[...] = nv

    vm = pl.BlockSpec(memory_space=pltpu.VMEM)
    return pl.pallas_call(
        body, name="small_allreduce_adamw",
        out_shape=[jax.ShapeDtypeStruct((R, W), F32)] * 4,
        in_specs=[vm] * 4, out_specs=[vm] * 4,
        scratch_shapes=[pltpu.VMEM((N_DEV, R, W), F32), pltpu.SemaphoreType.DMA((7,)), pltpu.SemaphoreType.DMA((7,)),
                        pltpu.SemaphoreType.DMA],
    )(part, w, m, v)


SHARDED = ("w_in", "conv_w", "pool_w", "w_attn_br", "w_pool_br", "w_ssm_br", "w_out", "w_gate_up", "w_down")
REPLICATED = ("ln1_w", "attn_sink", "conv_b", "dt_bias", "a_log", "d_skip", "ssm_norm_w", "pool_scale", "ln2_w")
CONV_ROWS = 8


class Pending:
    def __init__(self, name, started, layout):
        self.name, self.started, self.layout = name, started, layout


def need(fw, n, after):
    if n not in fw:
        n_src = "w_in"
        fw.update(fw[n_src].layout(copies_wait(fw[n_src].name, fw[n_src].started, after)))
        del fw[n_src]
    elif isinstance(fw[n], Pending):
        fw[n] = fw[n].layout(copies_wait(fw[n].name, fw[n].started, after))
    return fw[n]


def _w_in_layout(dm, g_in):
    win = jnp.concatenate([g_in[d] for d in range(N_DEV)], axis=1)
    pts, acc = [], 0
    for wd in dm.in_widths:
        pts.append((acc, acc + wd))
        acc += wd
    cols = lambda k: win[:, pts[k][0]:pts[k][1]]
    fw = {"w_qkv": win[:, :pts[2][1]], "w_u": cols(3), "w_z": cols(4), "w_xbc": cols(5), "w_gl": cols(7)}
    fw["w_dt"] = jnp.pad(cols(6), ((0, 0), (0, dm.DTP - dm.H)))
    fw["w_in_int"] = jnp.concatenate([win[:, :pts[5][1]], fw["w_dt"], fw["w_gl"]], axis=1)
    return fw


def _gather_weights(dm, W, li, w_in_now, tokens):
    tag = f"ag_l{li}_"
    fw = {}

    def start(n, shard, layout):
        st = copies_start(tag + n + "_start", shard, True, after=tokens[-1:])
        tokens.append(st[4])
        fw[n] = Pending(tag + n + "_wait", st, layout)

    if w_in_now:
        g_in = all_gather_hbm(tag + "w_in", W["w_in"][li].astype(BF16))
        tokens.append(g_in)
        fw.update(_w_in_layout(dm, g_in))
    else:
        start("w_in", W["w_in"][li].astype(BF16), functools.partial(_w_in_layout, dm))
    start("pool_w", W["pool_w"][li].astype(BF16).reshape(dm.PG * dm.PC // N_DEV, dm.PC),
          lambda g: g.reshape(N_DEV, dm.PG, dm.PC // N_DEV, dm.PC).transpose(1, 0, 2, 3)
          .reshape(dm.PG, dm.PC, dm.PC).astype(F32))
    start("conv_w", jnp.pad(W["conv_w"][li], ((0, CONV_ROWS - CONV_K), (0, 0))),
          lambda g: g[:, :CONV_K].transpose(1, 0, 2).reshape(CONV_K, dm.CC))
    for n in ("w_attn_br", "w_pool_br", "w_ssm_br", "w_out", "w_gate_up", "w_down"):
        if n in ("w_ssm_br", "w_out", "w_down"):
            layout = lambda g: g.reshape(g.shape[0] * g.shape[1], g.shape[2])
        else:
            layout = lambda g: g
        start(n, W[n][li].astype(BF16), layout)
    return fw


def _grad_blocks(dm, n, g):
    if n == "w_in":
        o = [0]
        for wd in dm.seg:
            o.append(o[-1] + wd)
        ref_cols = jnp.concatenate([g[:, :o[4]], g[:, o[4]:o[4] + dm.H], g[:, o[5]:]], axis=1)
        per = dm.IN_COLS // N_DEV
        return jnp.stack([ref_cols[:, d * per:(d + 1) * per] for d in range(N_DEV)])
    if n in ("w_attn_br", "w_pool_br", "w_gate_up"):
        return g
    if n in ("w_ssm_br", "w_out", "w_down"):
        return g.reshape(N_DEV, g.shape[0] // N_DEV, g.shape[1])
    if n == "pool_w":
        g = g.reshape(dm.PG, N_DEV, dm.PC // N_DEV, dm.PC).transpose(1, 0, 2, 3)
        return g.reshape(N_DEV, dm.PG * dm.PC // N_DEV, dm.PC).astype(BF16)
    assert n == "conv_w"
    g = g.reshape(CONV_K, N_DEV, dm.CC // N_DEV).transpose(1, 0, 2)
    return jnp.pad(g, ((0, 0), (0, CONV_ROWS - CONV_K), (0, 0)))


def _as_rows(n, a):
    if n == "pool_w":
        return a.reshape(a.shape[0], a.shape[1] * a.shape[2], a.shape[3])
    if n == "conv_w":
        return jnp.pad(a, ((0, 0), (0, CONV_ROWS - CONV_K), (0, 0)))
    return a


def _from_rows(n, a, like):
    if n == "pool_w":
        return a.reshape(like.shape)
    if n == "conv_w":
        return a[:, :CONV_K]
    return a


def _size(shape):
    n = 1
    for s in shape:
        n *= s
    return n


def _layer_forward(dm, x, wts, rep, li):
    tag = f"l{li}_"
    sv = {"x": x}
    (h,) = row_call(tag + "rms1", f_rms, [x], [rep["ln1_w"]], [(dm.D, BF16)], 256)
    sv["h"] = h
    qkv = matmul(tag + "p_qkv", h, need(wts, "w_qkv", x), "nn", BF16)
    u = matmul(tag + "p_u", h, wts["w_u"], "nn", F32)
    z = matmul(tag + "p_z", h, wts["w_z"], "nn", F32)
    xbc = matmul(tag + "p_xbc", h, wts["w_xbc"], "nn", F32)
    dtr = matmul(tag + "p_dt", h, wts["w_dt"], "nn", F32)
    gl = matmul(tag + "p_gl", h, wts["w_gl"], "nn", BF16)
    sv.update(qkv=qkv, u=u, z=z, xbc=xbc, dtr=dtr, gl=gl)
    kvi = dm.AW // dm.KVW
    (att,) = halo_call(tag + "attn", f_attn,
                       [(qkv, dm.AW, 0, 0), (qkv, dm.KVW, kvi, WINDOW), (qkv, dm.KVW, kvi + 1, WINDOW)],
                       [rep["attn_sink"]], [(dm.AW, BF16)], WINDOW)
    (pool,) = halo_call(tag + "pool", f_pool, [(u, dm.PW, 0, POOL_HALO)],
                        [need(wts, "pool_w", att), rep["pool_scale"]], [(dm.PW, BF16)], 256)
    (xc,) = halo_call(tag + "conv", f_conv, [(xbc, dm.CC, 0, CONV_HALO)],
                      [need(wts, "conv_w", pool), rep["conv_b"]], [(dm.CC, F32)], 256)
    y, hts = ssd_fwd(tag + "ssd", xc, dtr, rep["dt_bias"], rep["a_log"], rep["d_skip"])
    (ssm,) = row_call(tag + "gnorm", f_gnorm, [y, z], [rep["ssm_norm_w"]], [(dm.DI, BF16)], 256)
    sv.update(att=att, pool=pool, xc=xc, y=y, hts=hts, ssm=ssm)
    ba = matmul(tag + "br_a", att, need(wts, "w_attn_br", ssm), "nn", BF16, b_blocked=True)
    bp = matmul(tag + "br_p", pool, need(wts, "w_pool_br", ba), "nn", BF16, b_blocked=True)
    bs = matmul(tag + "br_s", ssm, need(wts, "w_ssm_br", bp), "nn", BF16)
    (merged,) = row_call(tag + "merge", f_merge, [gl, ba, bp, bs], [], [(dm.D, BF16)], 256)
    x1 = matmul(tag + "out", merged, need(wts, "w_out", merged), "nn", F32, add=x)
    (h2,) = row_call(tag + "rms2", f_rms, [x1], [rep["ln2_w"]], [(dm.D, BF16)], 256)
    gu = matmul(tag + "gu", h2, need(wts, "w_gate_up", h2), "nn", BF16, b_blocked=True)
    (act,) = row_call(tag + "swiglu", f_swiglu, [gu], [], [(dm.DFF, BF16)], 256)
    x2 = matmul(tag + "down", act, need(wts, "w_down", act), "nn", F32, add=x1)
    sv.update(ba=ba, bp=bp, bs=bs, merged=merged, x1=x1, h2=h2, gu=gu, act=act)
    return x2, sv


class GradSink:
    def __init__(self, dm, li):
        self.dm, self.li, self.started, self.tokens = dm, li, {}, []

    def __setitem__(self, n, g):
        self.started[n] = copies_start(f"rs_l{self.li}_{n}_start", _grad_blocks(self.dm, n, g), False)
        self.tokens.append(self.started[n][4])

    def take(self):
        t, self.tokens = tuple(self.tokens), []
        return t


def _layer_backward(dm, dx2, sv, wts, rep, li, prev_tokens):
    tag = f"l{li}_b_"
    gw, gr = GradSink(dm, li), {}
    dact = matmul(tag + "d_act", dx2, wts["w_down"], "nt", BF16, after=prev_tokens)
    gw["w_down"] = matmul(tag + "g_down", sv["act"], dx2, "tn", BF16)
    (dgu,), _ = row_vjp_call(tag + "swiglu", f_swiglu, [sv["gu"]], [], [dact], [BF16], [], 256)
    dh2 = matmul(tag + "d_h2", dgu, wts["w_gate_up"], "nt", BF16, b_blocked=True, after=gw.take())
    gw["w_gate_up"] = matmul(tag + "g_gu", sv["h2"], dgu, "tn", BF16, out_blocks=N_DEV)
    (dx1,), (gr["ln2_w"],) = row_vjp_call(tag + "rms2", f_rms, [sv["x1"]], [rep["ln2_w"]], [dh2], [F32], [True], 256,
                                          adds={0: dx2})
    dmerged = matmul(tag + "d_merged", dx1, wts["w_out"], "nt", BF16, after=gw.take())
    gw["w_out"] = matmul(tag + "g_out", sv["merged"], dx1, "tn", BF16)
    (dgl, dba, dbp, dbs), _ = row_vjp_call(tag + "merge", f_merge, [sv["gl"], sv["ba"], sv["bp"], sv["bs"]], [],
                                           [dmerged], [BF16, BF16, BF16, BF16], [], 256)
    datt = matmul(tag + "d_att", dba, wts["w_attn_br"], "nt", BF16, b_blocked=True, after=gw.take())
    gw["w_attn_br"] = matmul(tag + "g_br_a", sv["att"], dba, "tn", BF16, out_blocks=N_DEV)
    dpool = matmul(tag + "d_pool", dbp, wts["w_pool_br"], "nt", BF16, b_blocked=True, after=gw.take())
    gw["w_pool_br"] = matmul(tag + "g_br_p", sv["pool"], dbp, "tn", BF16, out_blocks=N_DEV)
    dssm = matmul(tag + "d_ssm", dbs, wts["w_ssm_br"], "nt", BF16, after=gw.take())
    gw["w_ssm_br"] = matmul(tag + "g_br_s", sv["ssm"], dbs, "tn", BF16)
    (dy, dz), (gr["ssm_norm_w"],) = row_vjp_call(tag + "gnorm", f_gnorm, [sv["y"], sv["z"]], [rep["ssm_norm_w"]],
                                                 [dssm], [F32, BF16], [True], 256)
    dxc, ddtr, gr["dt_bias"], gr["a_log"], gr["d_skip"] = ssd_bwd(
        tag + "ssd", sv["xc"], sv["dtr"], sv["hts"], rep["dt_bias"], rep["a_log"], rep["d_skip"], dy)
    dxbc, gw["conv_w"], gr["conv_b"] = conv_bwd(tag + "conv", sv["xbc"], wts["conv_w"], rep["conv_b"], dxc)
    (du,), (gw["pool_w"], gr["pool_scale"]) = halo_vjp_call(
        tag + "pool", f_pool, [(sv["u"], dm.PW, 0, POOL_HALO)], [wts["pool_w"], rep["pool_scale"]], [dpool],
        [BF16], [True, True], 256)
    kvi = dm.AW // dm.KVW
    qkv = sv["qkv"]
    (dq, dk, dv), (gr["attn_sink"],) = halo_vjp_call(
        tag + "attn", f_attn, [(qkv, dm.AW, 0, 0), (qkv, dm.KVW, kvi, WINDOW), (qkv, dm.KVW, kvi + 1, WINDOW)],
        [rep["attn_sink"]], [datt], [BF16, BF16, BF16], [True], WINDOW)
    dproj = jnp.concatenate([dq, dk, dv, du, dz, dxbc, ddtr, dgl], axis=1)
    gw["w_in"] = matmul(tag + "g_in", sv["h"], dproj, "tn", BF16, after=gw.take())
    dh = matmul(tag + "d_h", dproj, wts["w_in_int"], "nt", BF16, after=gw.take())
    (dx,), (gr["ln1_w"],) = row_vjp_call(tag + "rms1", f_rms, [sv["x"]], [rep["ln1_w"]], [dh], [F32], [True], 256,
                                         adds={0: dx1})
    return dx, gw.started, gr, gw.take()


def kernel(x, ln1_w, w_in, attn_sink, conv_w, conv_b, dt_bias, a_log, d_skip, ssm_norm_w, pool_w, pool_scale, w_attn_br, w_pool_br, w_ssm_br, w_out, ln2_w, w_gate_up, w_down, final_w, loss_target, m_ln1_w, m_w_in, m_attn_sink, m_conv_w, m_conv_b, m_dt_bias, m_a_log, m_d_skip, m_ssm_norm_w, m_pool_w, m_pool_scale, m_w_attn_br, m_w_pool_br, m_w_ssm_br, m_w_out, m_ln2_w, m_w_gate_up, m_w_down, m_final_w, v_ln1_w, v_w_in, v_attn_sink, v_conv_w, v_conv_b, v_dt_bias, v_a_log, v_d_skip, v_ssm_norm_w, v_pool_w, v_pool_scale, v_w_attn_br, v_w_pool_br, v_w_ssm_br, v_w_out, v_ln2_w, v_w_gate_up, v_w_down, v_final_w):
    dm = Dims()
    W = dict(ln1_w=ln1_w, w_in=w_in, attn_sink=attn_sink, conv_w=conv_w, conv_b=conv_b, dt_bias=dt_bias, a_log=a_log,
             d_skip=d_skip, ssm_norm_w=ssm_norm_w, pool_w=pool_w, pool_scale=pool_scale, w_attn_br=w_attn_br,
             w_pool_br=w_pool_br, w_ssm_br=w_ssm_br, w_out=w_out, ln2_w=ln2_w, w_gate_up=w_gate_up, w_down=w_down,
             final_w=final_w)
    M = dict(ln1_w=m_ln1_w, w_in=m_w_in, attn_sink=m_attn_sink, conv_w=m_conv_w, conv_b=m_conv_b, dt_bias=m_dt_bias,
             a_log=m_a_log, d_skip=m_d_skip, ssm_norm_w=m_ssm_norm_w, pool_w=m_pool_w, pool_scale=m_pool_scale,
             w_attn_br=m_w_attn_br, w_pool_br=m_w_pool_br, w_ssm_br=m_w_ssm_br, w_out=m_w_out, ln2_w=m_ln2_w,
             w_gate_up=m_w_gate_up, w_down=m_w_down, final_w=m_final_w)
    V = dict(ln1_w=v_ln1_w, w_in=v_w_in, attn_sink=v_attn_sink, conv_w=v_conv_w, conv_b=v_conv_b, dt_bias=v_dt_bias,
             a_log=v_a_log, d_skip=v_d_skip, ssm_norm_w=v_ssm_norm_w, pool_w=v_pool_w, pool_scale=v_pool_scale,
             w_attn_br=v_w_attn_br, w_pool_br=v_w_pool_br, w_ssm_br=v_w_ssm_br, w_out=v_w_out, ln2_w=v_ln2_w,
             w_gate_up=v_w_gate_up, w_down=v_w_down, final_w=v_final_w)
    xl = x[0]
    target = loss_target[0]

    tokens = []
    full = [_gather_weights(dm, W, li, li == 0, tokens) for li in range(DEPTH)]
    rep = [{n: W[n][li].reshape(1, -1) for n in REPLICATED} for li in range(DEPTH)]
    rep[0]["ln1_w"] = rep[0]["ln1_w"] + tokens[-1][0, 0]

    saved = []
    xa = xl
    for li in range(DEPTH):
        xa, sv = _layer_forward(dm, xa, full[li], rep[li], li)
        saved.append(sv)
    dxa, g_final, loss_blk = loss_head(xa, W["final_w"].reshape(1, -1), target, 256)

    gws, grs, left = [None] * DEPTH, [None] * DEPTH, ()
    for li in reversed(range(DEPTH)):
        dxa, gws[li], grs[li], left = _layer_backward(dm, dxa, saved[li], full[li], rep[li], li, left)
    grad_x = dxa[None]

    sharded_out = {n: None for n in SHARDED}
    for li in reversed(range(DEPTH)):
        for n in SHARDED:
            parts = copies_wait(f"rs_l{li}_{n}_wait", gws[li][n], dxa)
            sharded_out[n] = adamw_sharded(f"adamw_l{li}_{n}", parts, _as_rows(n, W[n]), _as_rows(n, M[n]),
                                           _as_rows(n, V[n]), li, sharded_out[n])
    g_sh, d_sh, m_sh, v_sh = [{n: _from_rows(n, sharded_out[n][k], W[n]) for n in SHARDED} for k in range(4)]

    small_names = [(n, li) for li in range(DEPTH) for n in REPLICATED] + [("final_w", None)]
    small_shape = lambda n, li: W[n].shape if li is None else W[n].shape[1:]
    small_rows = [-(-_size(small_shape(n, li)) // (8 * 128)) * 8 for n, li in small_names]
    loss_row = sum(small_rows)

    def small_pack(get, last=None):
        rows = []
        for (n, li), r in zip(small_names, small_rows):
            a = get(n, li).reshape(-1).astype(F32)
            rows.append(jnp.pad(a, (0, r * 128 - a.size)).reshape(r, 128))
        rows.append(jnp.zeros((8, 128), F32) if last is None else last)
        return jnp.concatenate(rows, axis=0)

    part = small_pack(lambda n, li: g_final if li is None else grs[li][n], loss_blk)
    wsm = small_pack(lambda n, li: W[n] if li is None else W[n][li])
    msm = small_pack(lambda n, li: M[n] if li is None else M[n][li])
    vsm = small_pack(lambda n, li: V[n] if li is None else V[n][li])
    sm = small_allreduce_adamw(part, wsm, msm, vsm)
    loss = sm[0][loss_row, 0]

    def small_unpack(buf):
        out, r0 = {}, 0
        for (n, li), r in zip(small_names, small_rows):
            shp = small_shape(n, li)
            out[(n, li)] = buf[r0:r0 + r].reshape(-1)[:_size(shp)].reshape(shp)
            r0 += r
        return out

    sm_g, sm_d, sm_m, sm_v = [small_unpack(b) for b in sm]

    def assemble(sharded_list, small):
        outs = []
        for n in ("ln1_w", "w_in", "attn_sink", "conv_w", "conv_b", "dt_bias", "a_log", "d_skip", "ssm_norm_w", "pool_w",
                  "pool_scale", "w_attn_br", "w_pool_br", "w_ssm_br", "w_out", "ln2_w", "w_gate_up", "w_down"):
            if n in SHARDED:
                outs.append(sharded_list[n])
            else:
                outs.append(jnp.stack([small[(n, li)] for li in range(DEPTH)]))
        outs.append(small[("final_w", None)])
        return outs

    return (loss, grad_x, *assemble(g_sh, sm_g), *assemble(d_sh, sm_d), *assemble(m_sh, sm_m), *assemble(v_sh, sm_v))
```

```python
import functools

import jax
import jax.numpy as jnp
from jax import lax
from jax.experimental import pallas as pl
from jax.experimental.pallas import tpu as pltpu

D_MODEL = 2048
DEPTH = 2
ATT_HEAD_DIM = 64
ATT_Q_HEADS = 16
ATT_KV_HEADS = 4
WINDOW = 128
POOL_WINDOWS = (2, 4, 8, 16)
POOL_WIDTH = D_MODEL // 2
D_INNER = D_MODEL
SSM_HEAD_DIM = 64
SSM_GROUPS = 4
D_STATE = 128
CONV_K = 4
CHUNK = 128
N_BRANCH = 3
D_FF = 5632
EPS = 1e-6

ADAM_LR = 0.001
ADAM_B1 = 0.9
ADAM_B2 = 0.999
ADAM_EPS = 1e-08
ADAM_WD = 0.01
ADAM_STEP = 10

N_DEV = 8
F32 = jnp.float32
BF16 = jnp.bfloat16
MXU_DTYPE = jnp.bfloat16
NEG = -1e30
VMEM_CAP = 60 * 2**20
CONV_HALO = 8
POOL_HALO = 16
MESH_T = pl.DeviceIdType.MESH


class Dims:
    def __init__(self):
        self.D = D_MODEL
        self.AW = ATT_Q_HEADS * ATT_HEAD_DIM
        self.KVW = ATT_KV_HEADS * ATT_HEAD_DIM
        self.GQ = ATT_Q_HEADS // ATT_KV_HEADS
        self.PW = POOL_WIDTH
        self.PG = len(POOL_WINDOWS)
        self.PC = POOL_WIDTH // len(POOL_WINDOWS)
        self.DI = D_INNER
        self.H = D_INNER // SSM_HEAD_DIM
        self.P = SSM_HEAD_DIM
        self.G = SSM_GROUPS
        self.HPG = self.H // SSM_GROUPS
        self.GW = D_INNER // SSM_GROUPS
        self.N = D_STATE
        self.GN = SSM_GROUPS * D_STATE
        self.CC = D_INNER + 2 * SSM_GROUPS * D_STATE
        self.DTP = -(-self.H // 256) * 256
        self.DFF = D_FF
        self.in_widths = (self.AW, self.KVW, self.KVW, self.PW, self.DI, self.CC, self.H, N_BRANCH * self.D)
        self.IN_COLS = sum(self.in_widths)
        self.QKV = self.AW + 2 * self.KVW
        self.seg = (self.QKV, self.PW, self.DI, self.CC, self.DTP, N_BRANCH * self.D)
        self.IN_INT = sum(self.seg)


def _tile(n, cap, mult=128):
    if n <= cap:
        return n
    best = 0
    for t in range(mult, cap + 1, mult):
        if n % t == 0:
            best = t
    assert best, (n, cap, mult)
    return best


def _nbytes(shape, dtype):
    n = 1
    for s in shape:
        n *= s
    return n * jnp.dtype(dtype).itemsize


def _params(est_bytes, sem=None):
    limit = int(min(VMEM_CAP, max(32 * 2**20, est_bytes * 3 // 2 + 8 * 2**20)))
    kw = dict(vmem_limit_bytes=limit)
    if sem is not None:
        kw["dimension_semantics"] = sem
    return pltpu.CompilerParams(**kw)


def _raw_dot(a, b, ca, cb):
    return lax.dot_general(a.astype(MXU_DTYPE), b.astype(MXU_DTYPE), (((ca,), (cb,)), ((), ())),
                           preferred_element_type=F32)


@functools.partial(jax.custom_vjp, nondiff_argnums=(2, 3))
def bdot(a, b, ca, cb):
    return _raw_dot(a, b, ca, cb)


def _bdot_fwd(a, b, ca, cb):
    return _raw_dot(a, b, ca, cb), (a, b)


def _bdot_bwd(ca, cb, res, g):
    a, b = res
    if (ca, cb) == (1, 0):
        da, db = bdot(g, b, 1, 1), bdot(a, g, 0, 0)
    elif (ca, cb) == (1, 1):
        da, db = bdot(g, b, 1, 0), bdot(g, a, 0, 0)
    else:
        assert (ca, cb) == (0, 0)
        da, db = bdot(b, g, 1, 1), bdot(a, g, 1, 0)
    return da.astype(a.dtype), db.astype(b.dtype)


bdot.defvjp(_bdot_fwd, _bdot_bwd)


def hdot(a, b, ca=1, cb=0):
    return lax.dot_general(a, b, (((ca,), (cb,)), ((), ())), precision=lax.Precision.HIGHEST,
                           preferred_element_type=F32)


def _split_dot(x, e, ca, cb):
    hi = x.astype(BF16)
    lo = (x - hi.astype(F32)).astype(BF16)
    dn = (((ca,), (cb,)), ((), ()))
    eb = e.astype(BF16)
    return (lax.dot_general(hi, eb, dn, preferred_element_type=F32)
            + lax.dot_general(lo, eb, dn, preferred_element_type=F32))


@jax.custom_vjp
def edot(x, e):
    return _split_dot(x, e, 1, 0)


def _edot_fwd(x, e):
    return _split_dot(x, e, 1, 0), e


def _edot_bwd(e, g):
    return _split_dot(g, e, 1, 1), jnp.zeros_like(e)


edot.defvjp(_edot_fwd, _edot_bwd)


def _silu(x):
    return x * jax.nn.sigmoid(x)


def _softplus(x):
    return jnp.maximum(x, 0.0) + jnp.log1p(jnp.exp(-jnp.abs(x)))


def matmul(name, a, b, mode, out_dtype, add=None, b_blocked=False, out_blocks=0, after=()):
    if b_blocked:
        nb, rows, n = b.shape
        b_rows, b_cols = rows, nb * n
    else:
        b_rows, b_cols = b.shape
    if mode == "nn":
        (M, K), (K2, N) = a.shape, (b_rows, b_cols)
    elif mode == "nt":
        (M, K), (N, K2) = a.shape, (b_rows, b_cols)
    else:
        (K, M), (K2, N) = a.shape, (b_rows, b_cols)
    assert K == K2, (name, a.shape, b.shape, mode)
    n_blk = n if b_blocked else (N // out_blocks if out_blocks else 0)
    tm = _tile(M, 1024)
    tn = _tile(n_blk if (n_blk and mode != "nt") else N, 1536)
    tk = _tile(n_blk if (n_blk and mode == "nt") else K, 2816)
    nk = K // tk
    dims = {"nn": (1, 0), "nt": (1, 1), "tn": (0, 0)}[mode]
    a_spec = (pl.BlockSpec((tk, tm), lambda i, j, k: (k, i)) if mode == "tn"
              else pl.BlockSpec((tm, tk), lambda i, j, k: (i, k)))
    if b_blocked and mode == "nt":
        per = n // tk
        b_spec = pl.BlockSpec((None, tn, tk), lambda i, j, k: (k // per, j, k % per))
    elif b_blocked:
        per = n // tn
        b_spec = pl.BlockSpec((None, tk, tn), lambda i, j, k: (j // per, k, j % per))
    elif mode == "nt":
        b_spec = pl.BlockSpec((tn, tk), lambda i, j, k: (j, k))
    else:
        b_spec = pl.BlockSpec((tk, tn), lambda i, j, k: (k, j))
    o_spec = pl.BlockSpec((tm, tn), lambda i, j, k: (i, j))
    out_shape = jax.ShapeDtypeStruct((M, N), out_dtype)
    if out_blocks:
        assert mode == "tn" and add is None
        per_o = n_blk // tn
        o_spec = pl.BlockSpec((None, tm, tn), lambda i, j, k: (j // per_o, i, j % per_o))
        out_shape = jax.ShapeDtypeStruct((out_blocks, M, n_blk), out_dtype)
    has_add = add is not None

    def body(*refs):
        a_ref, b_ref = refs[0], refs[1]
        c_ref = refs[2] if has_add else None
        o_ref = refs[2 + has_add + len(after)]
        p = _raw_dot(a_ref[...], b_ref[...], *dims)
        if nk == 1:
            if has_add:
                p = p + c_ref[...].astype(F32)
            o_ref[...] = p.astype(o_ref.dtype)
            return
        acc = refs[-1]
        k = pl.program_id(2)

        @pl.when(k == 0)
        def _():
            acc[...] = p + c_ref[...].astype(F32) if has_add else p

        @pl.when(k > 0)
        def _():
            acc[...] += p

        @pl.when(k == nk - 1)
        def _():
            o_ref[...] = acc[...].astype(o_ref.dtype)

    est = 2 * (_nbytes((tm, tk), a.dtype) + _nbytes((tk, tn), b.dtype) + _nbytes((tm, tn), out_dtype))
    est += 3 * _nbytes((tm, tn), F32) + _nbytes((tm, tk), MXU_DTYPE) + _nbytes((tk, tn), MXU_DTYPE)
    if has_add:
        est += 2 * _nbytes((tm, tn), add.dtype)
    args = (a, b) + ((add,) if has_add else ()) + tuple(after)
    in_specs = [a_spec, b_spec] + ([o_spec] if has_add else []) + [pl.BlockSpec(memory_space=pl.ANY)] * len(after)
    return pl.pallas_call(
        body, name=name, grid=(M // tm, N // tn, nk),
        in_specs=in_specs, out_specs=o_spec, out_shape=out_shape,
        scratch_shapes=[pltpu.VMEM((tm, tn), F32)] if nk > 1 else [],
        compiler_params=_params(est, ("parallel", "parallel", "arbitrary")),
    )(*args)


def row_call(name, f, rows, pars, outs, tm):
    L = rows[0].shape[0]
    tm = min(tm, L)
    nr, npar = len(rows), len(pars)

    def body(*refs):
        vals = [r[...] for r in refs[:nr + npar]]
        res = f(*vals)
        for o_ref, v in zip(refs[nr + npar:], res):
            o_ref[...] = v.astype(o_ref.dtype)

    est = 2 * sum(_nbytes((tm, a.shape[1]), a.dtype) for a in rows) + 2 * sum(_nbytes((tm, w), d) for w, d in outs)
    est += 4 * sum(_nbytes((tm, a.shape[1]), F32) for a in rows)
    res = pl.pallas_call(
        body, name=name, grid=(L // tm,),
        in_specs=[pl.BlockSpec((tm, a.shape[1]), lambda i: (i, 0)) for a in rows]
        + [pl.BlockSpec(p.shape, lambda i: (0, 0)) for p in pars],
        out_specs=[pl.BlockSpec((tm, w), lambda i: (i, 0)) for w, _ in outs],
        out_shape=[jax.ShapeDtypeStruct((L, w), d) for w, d in outs],
        compiler_params=_params(est, ("parallel",)),
    )(*rows, *pars)
    return res


def row_vjp_call(name, f, rows, pars, cots, row_grad_dtypes, par_grads, tm, adds=None):
    L = rows[0].shape[0]
    tm = min(tm, L)
    nr, npar, nc = len(rows), len(pars), len(cots)
    adds = adds or {}
    add_idx = sorted(adds)
    rg_idx = [i for i, d in enumerate(row_grad_dtypes) if d is not None]
    pg_idx = [i for i, w in enumerate(par_grads) if w]
    diff_idx = rg_idx + [nr + i for i in pg_idx]

    def body(*refs):
        vals = [r[...] for r in refs[:nr + npar]]
        cvals = [r[...] for r in refs[nr + npar:nr + npar + nc]]
        avals = [r[...] for r in refs[nr + npar + nc:nr + npar + nc + len(add_idx)]]
        out_refs = refs[nr + npar + nc + len(add_idx):]

        def g(*dv):
            full = list(vals)
            for i, v in zip(diff_idx, dv):
                full[i] = v
            return tuple(f(*full))

        res, vjp = jax.vjp(g, *[vals[i] for i in diff_idx])
        grads = vjp(tuple(c.astype(r.dtype) for c, r in zip(cvals, res)))
        for n, i in enumerate(rg_idx):
            gval = grads[n].astype(F32)
            if i in adds:
                gval = gval + avals[add_idx.index(i)].astype(F32)
            out_refs[n][...] = gval.astype(out_refs[n].dtype)
        first = pl.program_id(0) == 0
        for n, i in enumerate(pg_idx):
            o_ref = out_refs[len(rg_idx) + n]
            gval = grads[len(rg_idx) + n].astype(F32)

            @pl.when(first)
            def _(o_ref=o_ref, gval=gval):
                o_ref[...] = gval

            @pl.when(jnp.logical_not(first))
            def _(o_ref=o_ref, gval=gval):
                o_ref[...] += gval

    row_spec = lambda a: pl.BlockSpec((tm, a.shape[1]), lambda i: (i, 0))
    est = 2 * sum(_nbytes((tm, a.shape[1]), a.dtype) for a in list(rows) + list(cots))
    est += 10 * sum(_nbytes((tm, a.shape[1]), F32) for a in rows)
    res = pl.pallas_call(
        body, name=name, grid=(L // tm,),
        in_specs=[row_spec(a) for a in rows] + [pl.BlockSpec(p.shape, lambda i: (0, 0)) for p in pars]
        + [row_spec(c) for c in cots] + [row_spec(adds[i]) for i in add_idx],
        out_specs=[row_spec(rows[i]) for i in rg_idx] + [pl.BlockSpec(pars[i].shape, lambda i_: (0, 0)) for i in pg_idx],
        out_shape=[jax.ShapeDtypeStruct(rows[i].shape, row_grad_dtypes[i]) for i in rg_idx]
        + [jax.ShapeDtypeStruct(pars[i].shape, F32) for i in pg_idx],
        compiler_params=_params(est, ("arbitrary",)),
    )(*rows, *pars, *cots, *[adds[i] for i in add_idx])
    return list(res[:len(rg_idx)]), list(res[len(rg_idx):])


def _halo_specs(rows, T, nt, rev):
    specs = []
    for (_, w, ci, hs) in rows:
        if rev:
            specs.append(pl.BlockSpec((T, w), lambda j, ci=ci: (nt - 1 - j, ci)))
        else:
            specs.append(pl.BlockSpec((T, w), lambda i, ci=ci: (i, ci)))
        if hs:
            r = T // hs
            if rev:
                specs.append(pl.BlockSpec((hs, w), lambda j, ci=ci, r=r: (jnp.maximum((nt - 1 - j) * r - 1, 0), ci)))
            else:
                specs.append(pl.BlockSpec((hs, w), lambda i, ci=ci, r=r: (jnp.maximum(i * r - 1, 0), ci)))
    return specs


def _halo_args(rows):
    args = []
    for (a, _, _, hs) in rows:
        args.append(a)
        if hs:
            args.append(a)
    return args


def _halo_vals(rows, refs):
    vals, n = [], 0
    for (_, _, _, hs) in rows:
        if hs:
            vals.append((refs[n + 1][...], refs[n][...]))
            n += 2
        else:
            vals.append(refs[n][...])
            n += 1
    return vals, n


def halo_call(name, f, rows, pars, outs, T):
    L = rows[0][0].shape[0]
    T = min(T, L)
    nt = L // T
    npar = len(pars)

    def body(*refs):
        i = pl.program_id(0)
        vals, n = _halo_vals(rows, refs)
        pv = [r[...] for r in refs[n:n + npar]]
        res = f(i == 0, i * T, *vals, *pv)
        for o_ref, v in zip(refs[n + npar:], res):
            o_ref[...] = v.astype(o_ref.dtype)

    est = 2 * sum(_nbytes((T, w), a.dtype) for a, w, _, _ in rows) + 2 * sum(_nbytes((T, w), d) for w, d in outs)
    est += 8 * sum(_nbytes((T, w), F32) for _, w, _, _ in rows)
    return pl.pallas_call(
        body, name=name, grid=(nt,),
        in_specs=_halo_specs(rows, T, nt, False) + [pl.BlockSpec(p.shape, lambda i, nd=p.ndim: (0,) * nd) for p in pars],
        out_specs=[pl.BlockSpec((T, w), lambda i: (i, 0)) for w, _ in outs],
        out_shape=[jax.ShapeDtypeStruct((L, w), d) for w, d in outs],
        compiler_params=_params(est, ("parallel",)),
    )(*_halo_args(rows), *pars)


def halo_vjp_call(name, f, rows, pars, cots, row_grad_dtypes, par_grads, T):
    L = rows[0][0].shape[0]
    T = min(T, L)
    nt = L // T
    nr, npar, nc = len(rows), len(pars), len(cots)
    pg_idx = [i for i, w in enumerate(par_grads) if w]
    halo_idx = [i for i, r in enumerate(rows) if r[3]]

    def body(*refs):
        j = pl.program_id(0)
        i = nt - 1 - j
        vals, n = _halo_vals(rows, refs)
        pv = [r[...] for r in refs[n:n + npar]]
        cv = [r[...] for r in refs[n + npar:n + npar + nc]]
        out_refs = refs[n + npar + nc:n + npar + nc + nr + len(pg_idx)]
        carries = refs[n + npar + nc + nr + len(pg_idx):]
        first = i == 0

        def g(vals_, pv_):
            return tuple(f(first, i * T, *vals_, *pv_))

        res, vjp = jax.vjp(g, vals, pv)
        dvals, dpv = vjp(tuple(c.astype(r.dtype) for c, r in zip(cv, res)))

        @pl.when(j == 0)
        def _():
            for c_ref in carries:
                c_ref[...] = jnp.zeros_like(c_ref)

        for k in range(nr):
            hs = rows[k][3]
            o_ref = out_refs[k]
            if hs:
                dh, dc = dvals[k]
                c_ref = carries[halo_idx.index(k)]
                dc = dc.astype(F32)
                if hs == T:
                    o_ref[...] = (dc + c_ref[...]).astype(o_ref.dtype)
                else:
                    o_ref[0:T - hs, :] = dc[0:T - hs].astype(o_ref.dtype)
                    o_ref[T - hs:T, :] = (dc[T - hs:T] + c_ref[...]).astype(o_ref.dtype)
                c_ref[...] = dh.astype(F32)
            else:
                o_ref[...] = dvals[k].astype(o_ref.dtype)
        for m, k in enumerate(pg_idx):
            o_ref = out_refs[nr + m]
            gval = dpv[k].astype(F32)

            @pl.when(j == 0)
            def _(o_ref=o_ref, gval=gval):
                o_ref[...] = gval

            @pl.when(j > 0)
            def _(o_ref=o_ref, gval=gval):
                o_ref[...] += gval

    est = 2 * sum(_nbytes((T, w), a.dtype) for a, w, _, _ in rows) + 2 * sum(_nbytes((T, c.shape[1]), c.dtype) for c in cots)
    est += 12 * sum(_nbytes((T, w), F32) for _, w, _, _ in rows)
    res = pl.pallas_call(
        body, name=name, grid=(nt,),
        in_specs=_halo_specs(rows, T, nt, True) + [pl.BlockSpec(p.shape, lambda j, nd=p.ndim: (0,) * nd) for p in pars]
        + [pl.BlockSpec((T, c.shape[1]), lambda j: (nt - 1 - j, 0)) for c in cots],
        out_specs=[pl.BlockSpec((T, w), lambda j: (nt - 1 - j, 0)) for _, w, _, _ in rows]
        + [pl.BlockSpec(pars[k].shape, lambda j, nd=pars[k].ndim: (0,) * nd) for k in pg_idx],
        out_shape=[jax.ShapeDtypeStruct((L, w), row_grad_dtypes[k]) for k, (_, w, _, _) in enumerate(rows)]
        + [jax.ShapeDtypeStruct(pars[k].shape, F32) for k in pg_idx],
        scratch_shapes=[pltpu.VMEM((rows[k][3], rows[k][1]), F32) for k in halo_idx],
        compiler_params=_params(est, ("arbitrary",)),
    )(*_halo_args(rows), *pars, *cots)
    return list(res[:nr]), list(res[nr:])


def f_rms(x, w):
    x = x.astype(F32)
    return (x * lax.rsqrt(jnp.mean(x * x, axis=-1, keepdims=True) + EPS) * w,)


def f_swiglu(gu):
    dff = gu.shape[1] // 2
    gu = gu.astype(F32)
    return (_silu(gu[:, :dff]) * gu[:, dff:],)


def f_merge(gl, a, p, s):
    d = a.shape[1]
    g = jax.nn.sigmoid(gl.astype(F32))
    return (g[:, :d] * a.astype(F32) + g[:, d:2 * d] * p.astype(F32) + g[:, 2 * d:] * s.astype(F32),)


def f_gnorm(y, z, nw):
    dm = Dims()
    g = y * _silu(z.astype(F32))
    outs = []
    for gi in range(dm.G):
        gg = g[:, gi * dm.GW:(gi + 1) * dm.GW]
        outs.append(gg * lax.rsqrt(jnp.mean(gg * gg, axis=-1, keepdims=True) + EPS))
    return (jnp.concatenate(outs, axis=1) * nw,)


def f_conv(first, row0, xs, w, b):
    halo, cur = xs
    halo = jnp.where(first, 0.0, halo)
    ext = jnp.concatenate([halo, cur], axis=0)
    T = cur.shape[0]
    base = CONV_HALO - (CONV_K - 1)
    pre = b
    for k in range(CONV_K):
        pre = pre + w[k:k + 1, :] * ext[base + k:base + k + T]
    return (_silu(pre),)


def f_pool(first, row0, us, pw, scale):
    dm = Dims()
    halo, cur = us
    halo = jnp.where(first, 0.0, halo)
    ext = jnp.concatenate([halo, cur], axis=0)
    T = cur.shape[0]
    t = row0 + lax.broadcasted_iota(jnp.int32, (T, 1), 0)
    outs = []
    for gi, w in enumerate(POOL_WINDOWS):
        assert w & (w - 1) == 0 and w <= POOL_HALO
        s = ext[:, gi * dm.PC:(gi + 1) * dm.PC]
        sh = 1
        while sh < w:
            s = s + jnp.concatenate([jnp.zeros((sh, dm.PC), F32), s[:-sh]], axis=0)
            sh *= 2
        cnt = jnp.minimum(t + 1, w).astype(F32)
        mixed = s[POOL_HALO:] / cnt - cur[:, gi * dm.PC:(gi + 1) * dm.PC]
        outs.append(bdot(mixed, pw[gi], 1, 0))
    return (jnp.concatenate(outs, axis=1) * scale,)


def f_attn(first, row0, q, ks, vs, sink):
    dm = Dims()
    assert WINDOW == q.shape[0]
    kp, kc = ks
    vp, vc = vs
    T = q.shape[0]
    hd = ATT_HEAD_DIM
    cur1 = lax.broadcasted_iota(jnp.int32, (T, T), 1) <= lax.broadcasted_iota(jnp.int32, (T, T), 0)
    cur = jnp.concatenate([cur1] * dm.GQ, axis=0)
    valid = cur | jnp.logical_not(first)
    outs = []
    for k in range(ATT_KV_HEADS):
        ksl = slice(k * hd, (k + 1) * hd)
        heads = [k * dm.GQ + g for g in range(dm.GQ)]
        qs = jnp.concatenate([q[:, h * hd:(h + 1) * hd] for h in heads], axis=0)
        s = jnp.where(cur, bdot(qs, kc[:, ksl], 1, 1), bdot(qs, kp[:, ksl], 1, 1)) * (hd ** -0.5)
        s = jnp.where(valid, s, NEG)
        sk = jnp.concatenate([jnp.broadcast_to(sink[:, h:h + 1], (T, 1)) for h in heads], axis=0)
        m = jnp.maximum(jnp.max(s, axis=-1, keepdims=True), sk)
        p = jnp.exp(s - m)
        den = jnp.sum(p, axis=-1, keepdims=True) + jnp.exp(sk - m)
        pn = p / den
        o = bdot(jnp.where(cur, pn, 0.0), vc[:, ksl], 1, 0) + bdot(jnp.where(cur, 0.0, pn), vp[:, ksl], 1, 0)
        outs += [o[g * T:(g + 1) * T] for g in range(dm.GQ)]
    return (jnp.concatenate(outs, axis=1),)


def f_ssd(xc, dtr, ht, dt_bias, a_log, d_skip):
    dm = Dims()
    Q = xc.shape[0]
    xs = xc[:, :dm.DI]
    bm = xc[:, dm.DI:dm.DI + dm.GN]
    cm = xc[:, dm.DI + dm.GN:]
    expand = (lax.broadcasted_iota(jnp.int32, (dm.H, dm.DI), 1) // dm.P
              == lax.broadcasted_iota(jnp.int32, (dm.H, dm.DI), 0)).astype(F32)
    ri = lax.broadcasted_iota(jnp.int32, (Q, Q), 0)
    ci = lax.broadcasted_iota(jnp.int32, (Q, Q), 1)
    causal = ri >= ci
    tinc = causal.astype(F32)
    dt = _softplus(dtr[:, :dm.H] + dt_bias)
    da = dt * (-jnp.exp(a_log))
    acs = hdot(tinc, da)
    acs_t = hdot(da, tinc, 0, 1)
    eacs = jnp.exp(acs)
    dend = jnp.exp(acs[Q - 1:Q, :] - acs)
    ex = edot(jnp.concatenate([dt, eacs, dend, jnp.broadcast_to(d_skip, (8, dm.H))], axis=0), expand)
    dt_x, eacs_x, dend_x, dsk_x = ex[:Q], ex[Q:2 * Q], ex[2 * Q:3 * Q], ex[3 * Q:3 * Q + 1]
    xdt = xs * dt_x
    ys, hts = [], []
    for g in range(dm.G):
        gs = slice(g * dm.GW, (g + 1) * dm.GW)
        bg = bm[:, g * dm.N:(g + 1) * dm.N]
        cg = cm[:, g * dm.N:(g + 1) * dm.N]
        cb = bdot(cg, bg, 1, 1)
        y_off = bdot(cg, ht[:, gs], 1, 0) * eacs_x[:, gs]
        xg = xdt[:, gs]
        st = bdot(bg, xg * dend_x[:, gs], 0, 0)
        hts.append(ht[:, gs] * eacs_x[Q - 1:Q, gs] + st)
        yd = []
        for e in range(dm.HPG):
            h = g * dm.HPG + e
            seg = acs[:, h:h + 1] - acs_t[h:h + 1, :]
            lm = jnp.exp(jnp.where(causal, seg, NEG))
            yd.append(bdot(cb * lm, xg[:, e * dm.P:(e + 1) * dm.P], 1, 0))
        ys.append(jnp.concatenate(yd, axis=1) + y_off)
    y = jnp.concatenate(ys, axis=1) + dsk_x * xs
    return y, jnp.concatenate(hts, axis=1)


def ssd_fwd(name, xc, dtr, dt_bias, a_log, d_skip):
    dm = Dims()
    L = xc.shape[0]
    Q = CHUNK
    nc = L // Q

    def body(xc_ref, dtr_ref, b_ref, a_ref, s_ref, y_ref, hts_ref, ht):
        @pl.when(pl.program_id(0) == 0)
        def _():
            ht[...] = jnp.zeros_like(ht)

        h0 = ht[...]
        hts_ref[0] = h0
        y, h1 = f_ssd(xc_ref[...], dtr_ref[...], h0, b_ref[...], a_ref[...], s_ref[...])
        y_ref[...] = y
        ht[...] = h1

    par = pl.BlockSpec((1, dm.H), lambda c: (0, 0))
    est = 40 * _nbytes((Q, dm.CC), F32) + 4 * _nbytes((dm.N, dm.DI), F32)
    return pl.pallas_call(
        body, name=name, grid=(nc,),
        in_specs=[pl.BlockSpec((Q, dm.CC), lambda c: (c, 0)), pl.BlockSpec((Q, dm.DTP), lambda c: (c, 0)), par, par, par],
        out_specs=[pl.BlockSpec((Q, dm.DI), lambda c: (c, 0)), pl.BlockSpec((1, dm.N, dm.DI), lambda c: (c, 0, 0))],
        out_shape=[jax.ShapeDtypeStruct((L, dm.DI), F32), jax.ShapeDtypeStruct((nc, dm.N, dm.DI), F32)],
        scratch_shapes=[pltpu.VMEM((dm.N, dm.DI), F32)],
        compiler_params=_params(est, ("arbitrary",)),
    )(xc, dtr, dt_bias, a_log, d_skip)


def ssd_bwd(name, xc, dtr, hts, dt_bias, a_log, d_skip, dy):
    dm = Dims()
    L = xc.shape[0]
    Q = CHUNK
    nc = L // Q

    def body(xc_ref, dtr_ref, hts_ref, b_ref, a_ref, s_ref, dy_ref, dxc_ref, ddtr_ref, db_ref, da_ref, ds_ref, dht):
        j = pl.program_id(0)

        @pl.when(j == 0)
        def _():
            dht[...] = jnp.zeros_like(dht)

        _, vjp = jax.vjp(f_ssd, xc_ref[...], dtr_ref[...], hts_ref[0], b_ref[...], a_ref[...], s_ref[...])
        dxc, ddtr, dh0, db, da, ds = vjp((dy_ref[...], dht[...]))
        dxc_ref[...] = dxc.astype(dxc_ref.dtype)
        ddtr_ref[...] = ddtr.astype(ddtr_ref.dtype)
        dht[...] = dh0
        for o_ref, gval in ((db_ref, db), (da_ref, da), (ds_ref, ds)):
            @pl.when(j == 0)
            def _(o_ref=o_ref, gval=gval):
                o_ref[...] = gval

            @pl.when(j > 0)
            def _(o_ref=o_ref, gval=gval):
                o_ref[...] += gval

    par = pl.BlockSpec((1, dm.H), lambda j: (0, 0))
    rev = lambda w: pl.BlockSpec((Q, w), lambda j: (nc - 1 - j, 0))
    est = 80 * _nbytes((Q, dm.CC), F32) + 6 * _nbytes((dm.N, dm.DI), F32)
    return pl.pallas_call(
        body, name=name, grid=(nc,),
        in_specs=[rev(dm.CC), rev(dm.DTP), pl.BlockSpec((1, dm.N, dm.DI), lambda j: (nc - 1 - j, 0, 0)), par, par, par, rev(dm.DI)],
        out_specs=[rev(dm.CC), rev(dm.DTP), par, par, par],
        out_shape=[jax.ShapeDtypeStruct((L, dm.CC), BF16), jax.ShapeDtypeStruct((L, dm.DTP), BF16)]
        + [jax.ShapeDtypeStruct((1, dm.H), F32)] * 3,
        scratch_shapes=[pltpu.VMEM((dm.N, dm.DI), F32)],
        compiler_params=_params(est, ("arbitrary",)),
    )(xc, dtr, hts, dt_bias, a_log, d_skip, dy)


def conv_bwd(name, x, w, b, dy):
    L, CC = x.shape
    T = min(256, L)
    nt = L // T
    cb = _tile(CC, 512)
    hs, base = CONV_HALO, CONV_HALO - (CONV_K - 1)

    def body(x_ref, halo_ref, w_ref, b_ref, dy_ref, dx_ref, dw_ref, db_ref, carry):
        j = pl.program_id(1)
        first = j == nt - 1

        @pl.when(j == 0)
        def _():
            carry[...] = jnp.zeros_like(carry)

        wv = w_ref[...]
        ext = jnp.concatenate([jnp.where(first, 0.0, halo_ref[...]), x_ref[...]], axis=0)
        taps = [ext[base + k:base + k + T] for k in range(CONV_K)]
        pre = b_ref[...]
        for k in range(CONV_K):
            pre = pre + wv[k:k + 1, :] * taps[k]
        s = jax.nn.sigmoid(pre)
        dpre = dy_ref[...] * (s * (1.0 + pre * (1.0 - s)))
        dw = jnp.concatenate([jnp.sum(dpre * taps[k], axis=0, keepdims=True) for k in range(CONV_K)], axis=0)
        db = jnp.sum(dpre, axis=0, keepdims=True)
        zpad = jnp.zeros((hs, dpre.shape[1]), F32)
        dpad = jnp.concatenate([zpad, dpre, zpad], axis=0)
        dext = wv[0:1, :] * dpad[hs - base:hs - base + hs + T]
        for k in range(1, CONV_K):
            dext = dext + wv[k:k + 1, :] * dpad[hs - base - k:hs - base - k + hs + T]
        dx_ref[0:T - hs, :] = dext[hs:T].astype(dx_ref.dtype)
        dx_ref[T - hs:T, :] = (dext[T:T + hs] + carry[...]).astype(dx_ref.dtype)
        carry[...] = dext[0:hs]

        @pl.when(j == 0)
        def _():
            dw_ref[...] = dw
            db_ref[...] = db

        @pl.when(j > 0)
        def _():
            dw_ref[...] += dw
            db_ref[...] += db

    r = T // hs
    tile = pl.BlockSpec((T, cb), lambda c, j: (nt - 1 - j, c))
    est = 40 * _nbytes((T + 2 * hs, cb), F32)
    return pl.pallas_call(
        body, name=name, grid=(CC // cb, nt),
        in_specs=[tile, pl.BlockSpec((hs, cb), lambda c, j: (jnp.maximum((nt - 1 - j) * r - 1, 0), c)),
                  pl.BlockSpec((CONV_K, cb), lambda c, j: (0, c)), pl.BlockSpec((1, cb), lambda c, j: (0, c)), tile],
        out_specs=[tile, pl.BlockSpec((CONV_K, cb), lambda c, j: (0, c)), pl.BlockSpec((1, cb), lambda c, j: (0, c))],
        out_shape=[jax.ShapeDtypeStruct((L, CC), BF16), jax.ShapeDtypeStruct((CONV_K, CC), F32),
                   jax.ShapeDtypeStruct((1, CC), F32)],
        scratch_shapes=[pltpu.VMEM((hs, cb), F32)],
        compiler_params=_params(est, ("arbitrary", "arbitrary")),
    )(x, x, w, b, dy)


def loss_head(x, w, target, tm):
    L, D = x.shape
    tm = min(tm, L)

    def tile_loss(xv, wv, tv):
        (y,) = f_rms(xv, wv)
        return 0.5 * jnp.sum(jnp.mean(jnp.square(y - tv), axis=-1))

    def body(x_ref, w_ref, t_ref, dx_ref, dw_ref, loss_ref):
        val, (dx, dw) = jax.value_and_grad(tile_loss, argnums=(0, 1))(x_ref[...], w_ref[...], t_ref[...])
        dx_ref[...] = dx
        first = pl.program_id(0) == 0
        lv = jnp.full((8, 128), val, F32)

        @pl.when(first)
        def _():
            dw_ref[...] = dw
            loss_ref[...] = lv

        @pl.when(jnp.logical_not(first))
        def _():
            dw_ref[...] += dw
            loss_ref[...] += lv

    row = pl.BlockSpec((tm, D), lambda i: (i, 0))
    est = 16 * _nbytes((tm, D), F32)
    return pl.pallas_call(
        body, name="loss_head", grid=(L // tm,),
        in_specs=[row, pl.BlockSpec((1, D), lambda i: (0, 0)), row],
        out_specs=[row, pl.BlockSpec((1, D), lambda i: (0, 0)), pl.BlockSpec((8, 128), lambda i: (0, 0))],
        out_shape=[jax.ShapeDtypeStruct((L, D), F32), jax.ShapeDtypeStruct((1, D), F32), jax.ShapeDtypeStruct((8, 128), F32)],
        compiler_params=_params(est, ("arbitrary",)),
    )(x, w, target)


def _adamw(w, g, m, v):
    m = ADAM_B1 * m + (1.0 - ADAM_B1) * g
    v = ADAM_B2 * v + (1.0 - ADAM_B2) * jnp.square(g)
    m_hat = m / (1.0 - ADAM_B1 ** ADAM_STEP)
    v_hat = v / (1.0 - ADAM_B2 ** ADAM_STEP)
    delta = -ADAM_LR * (m_hat / (jnp.sqrt(v_hat) + ADAM_EPS) + ADAM_WD * w)
    return delta, m, v


def adamw_sharded(name, parts, w, m, v, li, carried):
    depth, R, C = w.shape
    tr = _tile(R, 128, 16)
    n_in = 4 + (4 if carried else 0)

    def body(*refs):
        p_ref, w_ref, m_ref, v_ref = refs[:4]
        g_ref, d_ref, nm_ref, nv_ref = refs[n_in:n_in + 4]
        g = p_ref[0].astype(F32)
        for s in range(1, N_DEV):
            g = g + p_ref[s].astype(F32)
        d, nm, nv = _adamw(w_ref[...], g, m_ref[...], v_ref[...])
        g_ref[...] = g
        d_ref[...] = d
        nm_ref[...] = nm
        nv_ref[...] = nv

    row = pl.BlockSpec((None, tr, C), lambda i: (li, i, 0))
    est = 2 * _nbytes((N_DEV, tr, C), parts.dtype) + 20 * _nbytes((tr, C), F32)
    return pl.pallas_call(
        body, name=name, grid=(R // tr,),
        in_specs=[pl.BlockSpec((N_DEV, tr, C), lambda i: (0, i, 0)), row, row, row]
        + ([pl.BlockSpec(memory_space=pl.ANY)] * 4 if carried else []),
        out_specs=[row] * 4,
        out_shape=[jax.ShapeDtypeStruct((depth, R, C), F32)] * 4,
        input_output_aliases={4 + k: k for k in range(4)} if carried else {},
        compiler_params=_params(est, ("parallel",)),
    )(parts, w, m, v, *(carried or ()))


def _my_place():
    return lax.axis_index("x"), lax.axis_index("y"), lax.axis_index("c")


def all_gather_hbm(name, shard):
    R, W = shard.shape

    def body(x_ref, out_ref, send_sems, recv_sems, local_sem):
        x, y, c = _my_place()
        me, sibling = (x, y, c), (x, y, 1 - c)
        chips = [(1 - x, y), (x, 1 - y), (1 - x, 1 - y)]

        def slot(px, py, pc):
            return out_ref.at[4 * px + 2 * py + pc]

        def copy(k, block, to, src=None):
            return pltpu.make_async_remote_copy(
                src_ref=slot(*block) if src is None else src, dst_ref=slot(*block),
                send_sem=send_sems.at[k], recv_sem=recv_sems.at[k], device_id=to, device_id_type=MESH_T)

        mine = pltpu.make_async_copy(x_ref, slot(*me), local_sem)
        mine.start()
        first = [copy(0, me, sibling, src=x_ref)]
        first += [copy(1 + j, me, (*chip, c), src=x_ref) for j, chip in enumerate(chips)]
        for cp in first:
            cp.start()
        passed = [copy(4 + j, (*chip, c), sibling) for j, chip in enumerate(chips)]
        for j, chip in enumerate(chips):
            copy(1 + j, (*chip, c), me).wait_recv()
            passed[j].start()
        copy(0, sibling, me).wait_recv()
        for j, chip in enumerate(chips):
            copy(4 + j, (*chip, 1 - c), me).wait_recv()
        for cp in first + passed:
            cp.wait_send()
        mine.wait()

    return pl.pallas_call(
        body, name=name,
        out_shape=jax.ShapeDtypeStruct((N_DEV, R, W), shard.dtype),
        in_specs=[pl.BlockSpec(memory_space=pl.ANY)],
        out_specs=pl.BlockSpec(memory_space=pl.ANY),
        scratch_shapes=[pltpu.SemaphoreType.DMA((7,)), pltpu.SemaphoreType.DMA((7,)), pltpu.SemaphoreType.DMA],
    )(shard)


HBM_SPEC = pl.BlockSpec(memory_space=pltpu.HBM)
SEM_SPEC = pl.BlockSpec(memory_space=pltpu.SEMAPHORE)
DATAFLOW = pltpu.SideEffectType.DATAFLOW_SIDE_EFFECTING


def _me():
    x, y, c = _my_place()
    return 4 * x + 2 * y + c


def copies_start(name, src, gather, after=()):
    blk = src.shape if gather else src.shape[1:]
    mine = src[None] if gather else lax.dynamic_slice_in_dim(src, _me(), 1, axis=0)
    land = lax.dynamic_update_slice(lax.empty((N_DEV,) + tuple(blk), src.dtype), mine, (_me(), 0, 0))
    n_after = len(after)

    def body(*refs):
        src_ref, land_ref = refs[0], refs[1]
        send_sem, recv_sem = refs[2 + n_after], refs[3 + n_after]
        token = refs[-1]
        x, y, c = _my_place()
        me = 4 * x + 2 * y + c
        for k in range(1, N_DEV):
            px, py, pc = (x + (k >> 2)) % 2, (y + ((k >> 1) & 1)) % 2, (c + (k & 1)) % 2
            pltpu.make_async_remote_copy(
                src_ref=src_ref if gather else src_ref.at[4 * px + 2 * py + pc], dst_ref=land_ref.at[me],
                send_sem=send_sem, recv_sem=recv_sem, device_id=(px, py, pc), device_id_type=MESH_T).start()
        token[...] = jnp.zeros_like(token)

    return pl.pallas_call(
        body, name=name,
        out_shape=(pltpu.SemaphoreType.DMA(()), pltpu.SemaphoreType.DMA(()), pltpu.HBM(src.shape, src.dtype),
                   pltpu.HBM(land.shape, land.dtype), jax.ShapeDtypeStruct((8, 128), F32)),
        in_specs=(HBM_SPEC, HBM_SPEC) + (pl.BlockSpec(memory_space=pl.ANY),) * n_after,
        out_specs=(SEM_SPEC, SEM_SPEC, HBM_SPEC, HBM_SPEC, pl.BlockSpec(memory_space=pltpu.VMEM)),
        input_output_aliases={0: 2, 1: 3},
        compiler_params=pltpu.CompilerParams(has_side_effects=DATAFLOW),
    )(pltpu.with_memory_space_constraint(src, pltpu.HBM), pltpu.with_memory_space_constraint(land, pltpu.HBM), *after)


def copies_wait(name, started, after):
    send_sem, recv_sem, src_thru, land_thru, _ = started

    def body(src_ref, land_ref, send_sem, recv_sem, after_ref, src_dead, got_ref):
        seven = land_ref.at[pl.ds(0, N_DEV - 1)]
        all_seven = pltpu.make_async_remote_copy(src_ref=seven, dst_ref=seven, send_sem=send_sem, recv_sem=recv_sem,
                                                 device_id=_my_place(), device_id_type=MESH_T)
        all_seven.wait_send()
        all_seven.wait_recv()

    return pl.pallas_call(
        body, name=name,
        out_shape=(pltpu.HBM(src_thru.shape, src_thru.dtype), pltpu.HBM(land_thru.shape, land_thru.dtype)),
        in_specs=(HBM_SPEC, HBM_SPEC, SEM_SPEC, SEM_SPEC, pl.BlockSpec(memory_space=pl.ANY)),
        out_specs=(HBM_SPEC, HBM_SPEC), input_output_aliases={0: 0, 1: 1},
        compiler_params=pltpu.CompilerParams(has_side_effects=DATAFLOW),
    )(src_thru, land_thru, send_sem, recv_sem, after)[1]


def small_allreduce_adamw(part, w, m, v):
    R, W = part.shape

    def body(x_ref, w_ref, m_ref, v_ref, g_ref, d_ref, nm_ref, nv_ref, all_ref, send_sems, recv_sems, local_sem):
        x, y, c = _my_place()
        me, sibling = (x, y, c), (x, y, 1 - c)
        chips = [(1 - x, y), (x, 1 - y), (1 - x, 1 - y)]

        def slot(px, py, pc):
            return all_ref.at[4 * px + 2 * py + pc]

        def copy(k, block, to, src=None):
            return pltpu.make_async_remote_copy(
                src_ref=slot(*block) if src is None else src, dst_ref=slot(*block),
                send_sem=send_sems.at[k], recv_sem=recv_sems.at[k], device_id=to, device_id_type=MESH_T)

        mine = pltpu.make_async_copy(x_ref, slot(*me), local_sem)
        mine.start()
        first = [copy(0, me, sibling, src=x_ref)]
        first += [copy(1 + j, me, (*chip, c), src=x_ref) for j, chip in enumerate(chips)]
        for cp in first:
            cp.start()
        passed = [copy(4 + j, (*chip, c), sibling) for j, chip in enumerate(chips)]
        for j, chip in enumerate(chips):
            copy(1 + j, (*chip, c), me).wait_recv()
            passed[j].start()
        copy(0, sibling, me).wait_recv()
        for j, chip in enumerate(chips):
            copy(4 + j, (*chip, 1 - c), me).wait_recv()
        for cp in first + passed:
            cp.wait_send()
        mine.wait()
        g = all_ref[0]
        for s in range(1, N_DEV):
            g = g + all_ref[s]
        d, nm, nv = _adamw(w_ref[...], g, m_ref[...], v_ref[...])
        g_ref[...] = g
        d_ref[...] = d
        nm_ref[...] = nm
        nv_ref[...] = nv

    vm = pl.BlockSpec(memory_space=pltpu.VMEM)
    return pl.pallas_call(
        body, name="small_allreduce_adamw",
        out_shape=[jax.ShapeDtypeStruct((R, W), F32)] * 4,
        in_specs=[vm] * 4, out_specs=[vm] * 4,
        scratch_shapes=[pltpu.VMEM((N_DEV, R, W), F32), pltpu.SemaphoreType.DMA((7,)), pltpu.SemaphoreType.DMA((7,)),
                        pltpu.SemaphoreType.DMA],
    )(part, w, m, v)


SHARDED = ("w_in", "conv_w", "pool_w", "w_attn_br", "w_pool_br", "w_ssm_br", "w_out", "w_gate_up", "w_down")
REPLICATED = ("ln1_w", "attn_sink", "conv_b", "dt_bias", "a_log", "d_skip", "ssm_norm_w", "pool_scale", "ln2_w")
CONV_ROWS = 8


class Pending:
    def __init__(self, name, started, layout):
        self.name, self.started, self.layout = name, started, layout


def need(fw, n, after):
    if n not in fw:
        n_src = "w_in"
        fw.update(fw[n_src].layout(copies_wait(fw[n_src].name, fw[n_src].started, after)))
        del fw[n_src]
    elif isinstance(fw[n], Pending):
        fw[n] = fw[n].layout(copies_wait(fw[n].name, fw[n].started, after))
    return fw[n]


def _w_in_layout(dm, g_in):
    win = jnp.concatenate([g_in[d] for d in range(N_DEV)], axis=1)
    pts, acc = [], 0
    for wd in dm.in_widths:
        pts.append((acc, acc + wd))
        acc += wd
    cols = lambda k: win[:, pts[k][0]:pts[k][1]]
    fw = {"w_qkv": win[:, :pts[2][1]], "w_u": cols(3), "w_z": cols(4), "w_xbc": cols(5), "w_gl": cols(7)}
    fw["w_dt"] = jnp.pad(cols(6), ((0, 0), (0, dm.DTP - dm.H)))
    fw["w_in_int"] = jnp.concatenate([win[:, :pts[5][1]], fw["w_dt"], fw["w_gl"]], axis=1)
    return fw


def _gather_weights(dm, W, li, w_in_now, tokens):
    tag = f"ag_l{li}_"
    fw = {}

    def start(n, shard, layout):
        st = copies_start(tag + n + "_start", shard, True, after=tokens[-1:])
        tokens.append(st[4])
        fw[n] = Pending(tag + n + "_wait", st, layout)

    if w_in_now:
        g_in = all_gather_hbm(tag + "w_in", W["w_in"][li].astype(BF16))
        tokens.append(g_in)
        fw.update(_w_in_layout(dm, g_in))
    else:
        start("w_in", W["w_in"][li].astype(BF16), functools.partial(_w_in_layout, dm))
    start("pool_w", W["pool_w"][li].astype(BF16).reshape(dm.PG * dm.PC // N_DEV, dm.PC),
          lambda g: g.reshape(N_DEV, dm.PG, dm.PC // N_DEV, dm.PC).transpose(1, 0, 2, 3)
          .reshape(dm.PG, dm.PC, dm.PC).astype(F32))
    start("conv_w", jnp.pad(W["conv_w"][li], ((0, CONV_ROWS - CONV_K), (0, 0))),
          lambda g: g[:, :CONV_K].transpose(1, 0, 2).reshape(CONV_K, dm.CC))
    for n in ("w_attn_br", "w_pool_br", "w_ssm_br", "w_out", "w_gate_up", "w_down"):
        if n in ("w_ssm_br", "w_out", "w_down"):
            layout = lambda g: g.reshape(g.shape[0] * g.shape[1], g.shape[2])
        else:
            layout = lambda g: g
        start(n, W[n][li].astype(BF16), layout)
    return fw


def _grad_blocks(dm, n, g):
    if n == "w_in":
        o = [0]
        for wd in dm.seg:
            o.append(o[-1] + wd)
        ref_cols = jnp.concatenate([g[:, :o[4]], g[:, o[4]:o[4] + dm.H], g[:, o[5]:]], axis=1)
        per = dm.IN_COLS // N_DEV
        return jnp.stack([ref_cols[:, d * per:(d + 1) * per] for d in range(N_DEV)])
    if n in ("w_attn_br", "w_pool_br", "w_gate_up"):
        return g
    if n in ("w_ssm_br", "w_out", "w_down"):
        return g.reshape(N_DEV, g.shape[0] // N_DEV, g.shape[1])
    if n == "pool_w":
        g = g.reshape(dm.PG, N_DEV, dm.PC // N_DEV, dm.PC).transpose(1, 0, 2, 3)
        return g.reshape(N_DEV, dm.PG * dm.PC // N_DEV, dm.PC).astype(BF16)
    assert n == "conv_w"
    g = g.reshape(CONV_K, N_DEV, dm.CC // N_DEV).transpose(1, 0, 2)
    return jnp.pad(g, ((0, 0), (0, CONV_ROWS - CONV_K), (0, 0)))


def _as_rows(n, a):
    if n == "pool_w":
        return a.reshape(a.shape[0], a.shape[1] * a.shape[2], a.shape[3])
    if n == "conv_w":
        return jnp.pad(a, ((0, 0), (0, CONV_ROWS - CONV_K), (0, 0)))
    return a


def _from_rows(n, a, like):
    if n == "pool_w":
        return a.reshape(like.shape)
    if n == "conv_w":
        return a[:, :CONV_K]
    return a


def _size(shape):
    n = 1
    for s in shape:
        n *= s
    return n


def _layer_forward(dm, x, wts, rep, li):
    tag = f"l{li}_"
    sv = {"x": x}
    (h,) = row_call(tag + "rms1", f_rms, [x], [rep["ln1_w"]], [(dm.D, BF16)], 256)
    sv["h"] = h
    qkv = matmul(tag + "p_qkv", h, need(wts, "w_qkv", x), "nn", BF16)
    u = matmul(tag + "p_u", h, wts["w_u"], "nn", F32)
    z = matmul(tag + "p_z", h, wts["w_z"], "nn", BF16)
    xbc = matmul(tag + "p_xbc", h, wts["w_xbc"], "nn", F32)
    dtr = matmul(tag + "p_dt", h, wts["w_dt"], "nn", F32)
    gl = matmul(tag + "p_gl", h, wts["w_gl"], "nn", BF16)
    sv.update(qkv=qkv, u=u, z=z, xbc=xbc, dtr=dtr, gl=gl)
    kvi = dm.AW // dm.KVW
    (att,) = halo_call(tag + "attn", f_attn,
                       [(qkv, dm.AW, 0, 0), (qkv, dm.KVW, kvi, WINDOW), (qkv, dm.KVW, kvi + 1, WINDOW)],
                       [rep["attn_sink"]], [(dm.AW, BF16)], WINDOW)
    (pool,) = halo_call(tag + "pool", f_pool, [(u, dm.PW, 0, POOL_HALO)],
                        [need(wts, "pool_w", att), rep["pool_scale"]], [(dm.PW, BF16)], 256)
    (xc,) = halo_call(tag + "conv", f_conv, [(xbc, dm.CC, 0, CONV_HALO)],
                      [need(wts, "conv_w", pool), rep["conv_b"]], [(dm.CC, F32)], 256)
    y, hts = ssd_fwd(tag + "ssd", xc, dtr, rep["dt_bias"], rep["a_log"], rep["d_skip"])
    (ssm,) = row_call(tag + "gnorm", f_gnorm, [y, z], [rep["ssm_norm_w"]], [(dm.DI, BF16)], 256)
    sv.update(att=att, pool=pool, xc=xc, y=y, hts=hts, ssm=ssm)
    ba = matmul(tag + "br_a", att, need(wts, "w_attn_br", ssm), "nn", BF16, b_blocked=True)
    bp = matmul(tag + "br_p", pool, need(wts, "w_pool_br", ba), "nn", BF16, b_blocked=True)
    bs = matmul(tag + "br_s", ssm, need(wts, "w_ssm_br", bp), "nn", BF16)
    (merged,) = row_call(tag + "merge", f_merge, [gl, ba, bp, bs], [], [(dm.D, BF16)], 256)
    x1 = matmul(tag + "out", merged, need(wts, "w_out", merged), "nn", F32, add=x)
    (h2,) = row_call(tag + "rms2", f_rms, [x1], [rep["ln2_w"]], [(dm.D, BF16)], 256)
    gu = matmul(tag + "gu", h2, need(wts, "w_gate_up", h2), "nn", BF16, b_blocked=True)
    (act,) = row_call(tag + "swiglu", f_swiglu, [gu], [], [(dm.DFF, BF16)], 256)
    x2 = matmul(tag + "down", act, need(wts, "w_down", act), "nn", F32, add=x1)
    sv.update(ba=ba, bp=bp, bs=bs, merged=merged, x1=x1, h2=h2, gu=gu, act=act)
    return x2, sv


class GradSink:
    def __init__(self, dm, li):
        self.dm, self.li, self.started, self.tokens = dm, li, {}, []

    def __setitem__(self, n, g):
        self.started[n] = copies_start(f"rs_l{self.li}_{n}_start", _grad_blocks(self.dm, n, g), False)
        self.tokens.append(self.started[n][4])

    def take(self):
        t, self.tokens = tuple(self.tokens), []
        return t


def _layer_backward(dm, dx2, sv, wts, rep, li, prev_tokens):
    tag = f"l{li}_b_"
    gw, gr = GradSink(dm, li), {}
    dact = matmul(tag + "d_act", dx2, wts["w_down"], "nt", BF16, after=prev_tokens)
    gw["w_down"] = matmul(tag + "g_down", sv["act"], dx2, "tn", BF16)
    (dgu,), _ = row_vjp_call(tag + "swiglu", f_swiglu, [sv["gu"]], [], [dact], [BF16], [], 256)
    dh2 = matmul(tag + "d_h2", dgu, wts["w_gate_up"], "nt", BF16, b_blocked=True, after=gw.take())
    gw["w_gate_up"] = matmul(tag + "g_gu", sv["h2"], dgu, "tn", BF16, out_blocks=N_DEV)
    (dx1,), (gr["ln2_w"],) = row_vjp_call(tag + "rms2", f_rms, [sv["x1"]], [rep["ln2_w"]], [dh2], [F32], [True], 256,
                                          adds={0: dx2})
    dmerged = matmul(tag + "d_merged", dx1, wts["w_out"], "nt", BF16, after=gw.take())
    gw["w_out"] = matmul(tag + "g_out", sv["merged"], dx1, "tn", BF16)
    (dgl, dba, dbp, dbs), _ = row_vjp_call(tag + "merge", f_merge, [sv["gl"], sv["ba"], sv["bp"], sv["bs"]], [],
                                           [dmerged], [BF16, BF16, BF16, BF16], [], 256)
    datt = matmul(tag + "d_att", dba, wts["w_attn_br"], "nt", BF16, b_blocked=True, after=gw.take())
    gw["w_attn_br"] = matmul(tag + "g_br_a", sv["att"], dba, "tn", BF16, out_blocks=N_DEV)
    dpool = matmul(tag + "d_pool", dbp, wts["w_pool_br"], "nt", BF16, b_blocked=True, after=gw.take())
    gw["w_pool_br"] = matmul(tag + "g_br_p", sv["pool"], dbp, "tn", BF16, out_blocks=N_DEV)
    dssm = matmul(tag + "d_ssm", dbs, wts["w_ssm_br"], "nt", BF16, after=gw.take())
    gw["w_ssm_br"] = matmul(tag + "g_br_s", sv["ssm"], dbs, "tn", BF16)
    (dy, dz), (gr["ssm_norm_w"],) = row_vjp_call(tag + "gnorm", f_gnorm, [sv["y"], sv["z"]], [rep["ssm_norm_w"]],
                                                 [dssm], [F32, BF16], [True], 256)
    dxc, ddtr, gr["dt_bias"], gr["a_log"], gr["d_skip"] = ssd_bwd(
        tag + "ssd", sv["xc"], sv["dtr"], sv["hts"], rep["dt_bias"], rep["a_log"], rep["d_skip"], dy)
    dxbc, gw["conv_w"], gr["conv_b"] = conv_bwd(tag + "conv", sv["xbc"], wts["conv_w"], rep["conv_b"], dxc)
    (du,), (gw["pool_w"], gr["pool_scale"]) = halo_vjp_call(
        tag + "pool", f_pool, [(sv["u"], dm.PW, 0, POOL_HALO)], [wts["pool_w"], rep["pool_scale"]], [dpool],
        [BF16], [True, True], 256)
    kvi = dm.AW // dm.KVW
    qkv = sv["qkv"]
    (dq, dk, dv), (gr["attn_sink"],) = halo_vjp_call(
        tag + "attn", f_attn, [(qkv, dm.AW, 0, 0), (qkv, dm.KVW, kvi, WINDOW), (qkv, dm.KVW, kvi + 1, WINDOW)],
        [rep["attn_sink"]], [datt], [BF16, BF16, BF16], [True], WINDOW)
    dproj = jnp.concatenate([dq, dk, dv, du, dz, dxbc, ddtr, dgl], axis=1)
    gw["w_in"] = matmul(tag + "g_in", sv["h"], dproj, "tn", BF16, after=gw.take())
    dh = matmul(tag + "d_h", dproj, wts["w_in_int"], "nt", BF16, after=gw.take())
    (dx,), (gr["ln1_w"],) = row_vjp_call(tag + "rms1", f_rms, [sv["x"]], [rep["ln1_w"]], [dh], [F32], [True], 256,
                                         adds={0: dx1})
    return dx, gw.started, gr, gw.take()


def kernel(x, ln1_w, w_in, attn_sink, conv_w, conv_b, dt_bias, a_log, d_skip, ssm_norm_w, pool_w, pool_scale, w_attn_br, w_pool_br, w_ssm_br, w_out, ln2_w, w_gate_up, w_down, final_w, loss_target, m_ln1_w, m_w_in, m_attn_sink, m_conv_w, m_conv_b, m_dt_bias, m_a_log, m_d_skip, m_ssm_norm_w, m_pool_w, m_pool_scale, m_w_attn_br, m_w_pool_br, m_w_ssm_br, m_w_out, m_ln2_w, m_w_gate_up, m_w_down, m_final_w, v_ln1_w, v_w_in, v_attn_sink, v_conv_w, v_conv_b, v_dt_bias, v_a_log, v_d_skip, v_ssm_norm_w, v_pool_w, v_pool_scale, v_w_attn_br, v_w_pool_br, v_w_ssm_br, v_w_out, v_ln2_w, v_w_gate_up, v_w_down, v_final_w):
    dm = Dims()
    W = dict(ln1_w=ln1_w, w_in=w_in, attn_sink=attn_sink, conv_w=conv_w, conv_b=conv_b, dt_bias=dt_bias, a_log=a_log,
             d_skip=d_skip, ssm_norm_w=ssm_norm_w, pool_w=pool_w, pool_scale=pool_scale, w_attn_br=w_attn_br,
             w_pool_br=w_pool_br, w_ssm_br=w_ssm_br, w_out=w_out, ln2_w=ln2_w, w_gate_up=w_gate_up, w_down=w_down,
             final_w=final_w)
    M = dict(ln1_w=m_ln1_w, w_in=m_w_in, attn_sink=m_attn_sink, conv_w=m_conv_w, conv_b=m_conv_b, dt_bias=m_dt_bias,
             a_log=m_a_log, d_skip=m_d_skip, ssm_norm_w=m_ssm_norm_w, pool_w=m_pool_w, pool_scale=m_pool_scale,
             w_attn_br=m_w_attn_br, w_pool_br=m_w_pool_br, w_ssm_br=m_w_ssm_br, w_out=m_w_out, ln2_w=m_ln2_w,
             w_gate_up=m_w_gate_up, w_down=m_w_down, final_w=m_final_w)
    V = dict(ln1_w=v_ln1_w, w_in=v_w_in, attn_sink=v_attn_sink, conv_w=v_conv_w, conv_b=v_conv_b, dt_bias=v_dt_bias,
             a_log=v_a_log, d_skip=v_d_skip, ssm_norm_w=v_ssm_norm_w, pool_w=v_pool_w, pool_scale=v_pool_scale,
             w_attn_br=v_w_attn_br, w_pool_br=v_w_pool_br, w_ssm_br=v_w_ssm_br, w_out=v_w_out, ln2_w=v_ln2_w,
             w_gate_up=v_w_gate_up, w_down=v_w_down, final_w=v_final_w)
    xl = x[0]
    target = loss_target[0]

    tokens = []
    full = [_gather_weights(dm, W, li, li == 0, tokens) for li in range(DEPTH)]
    rep = [{n: W[n][li].reshape(1, -1) for n in REPLICATED} for li in range(DEPTH)]
    rep[0]["ln1_w"] = rep[0]["ln1_w"] + tokens[-1][0, 0]

    saved = []
    xa = xl
    for li in range(DEPTH):
        xa, sv = _layer_forward(dm, xa, full[li], rep[li], li)
        saved.append(sv)
    dxa, g_final, loss_blk = loss_head(xa, W["final_w"].reshape(1, -1), target, 256)

    gws, grs, left = [None] * DEPTH, [None] * DEPTH, ()
    for li in reversed(range(DEPTH)):
        dxa, gws[li], grs[li], left = _layer_backward(dm, dxa, saved[li], full[li], rep[li], li, left)
    grad_x = dxa[None]

    sharded_out = {n: None for n in SHARDED}
    for li in reversed(range(DEPTH)):
        for n in SHARDED:
            parts = copies_wait(f"rs_l{li}_{n}_wait", gws[li][n], dxa)
            sharded_out[n] = adamw_sharded(f"adamw_l{li}_{n}", parts, _as_rows(n, W[n]), _as_rows(n, M[n]),
                                           _as_rows(n, V[n]), li, sharded_out[n])
    g_sh, d_sh, m_sh, v_sh = [{n: _from_rows(n, sharded_out[n][k], W[n]) for n in SHARDED} for k in range(4)]

    small_names = [(n, li) for li in range(DEPTH) for n in REPLICATED] + [("final_w", None)]
    small_shape = lambda n, li: W[n].shape if li is None else W[n].shape[1:]
    small_rows = [-(-_size(small_shape(n, li)) // (8 * 128)) * 8 for n, li in small_names]
    loss_row = sum(small_rows)

    def small_pack(get, last=None):
        rows = []
        for (n, li), r in zip(small_names, small_rows):
            a = get(n, li).reshape(-1).astype(F32)
            rows.append(jnp.pad(a, (0, r * 128 - a.size)).reshape(r, 128))
        rows.append(jnp.zeros((8, 128), F32) if last is None else last)
        return jnp.concatenate(rows, axis=0)

    part = small_pack(lambda n, li: g_final if li is None else grs[li][n], loss_blk)
    wsm = small_pack(lambda n, li: W[n] if li is None else W[n][li])
    msm = small_pack(lambda n, li: M[n] if li is None else M[n][li])
    vsm = small_pack(lambda n, li: V[n] if li is None else V[n][li])
    sm = small_allreduce_adamw(part, wsm, msm, vsm)
    loss = sm[0][loss_row, 0]

    def small_unpack(buf):
        out, r0 = {}, 0
        for (n, li), r in zip(small_names, small_rows):
            shp = small_shape(n, li)
            out[(n, li)] = buf[r0:r0 + r].reshape(-1)[:_size(shp)].reshape(shp)
            r0 += r
        return out

    sm_g, sm_d, sm_m, sm_v = [small_unpack(b) for b in sm]

    def assemble(sharded_list, small):
        outs = []
        for n in ("ln1_w", "w_in", "attn_sink", "conv_w", "conv_b", "dt_bias", "a_log", "d_skip", "ssm_norm_w", "pool_w",
                  "pool_scale", "w_attn_br", "w_pool_br", "w_ssm_br", "w_out", "ln2_w", "w_gate_up", "w_down"):
            if n in SHARDED:
                outs.append(sharded_list[n])
            else:
                outs.append(jnp.stack([small[(n, li)] for li in range(DEPTH)]))
        outs.append(small[("final_w", None)])
        return outs

    return (loss, grad_x, *assemble(g_sh, sm_g), *assemble(d_sh, sm_d), *assemble(m_sh, sm_m), *assemble(v_sh, sm_v))
```
